```python
import math
import jax, jax.numpy as jnp
from jax import lax
import numpy as np

D_MODEL = 2048
BATCH = 4
SEQ = 4096
DEPTH = 2

N_EVEN = (DEPTH + 1) // 2
N_ODD = DEPTH // 2
EPS = 1e-6
D_FF = 5632

A_WIDTH = D_MODEL // 2
A_CHUNK = 128
A_GROUPS = 8
A_GROUP_DIM = A_WIDTH // A_GROUPS

B_HEADS = 4
B_WIDTH = D_MODEL // 2
B_HEAD_DIM = B_WIDTH // B_HEADS
B_CHUNK = 128
B_CONV = 4

P_AB = 2 * A_WIDTH + 4 * B_WIDTH + 2 * B_HEADS

C_HEADS = 16
C_KV = 4
C_REP = C_HEADS // C_KV
C_HEAD_DIM = D_MODEL // C_HEADS
CMP_LEN = 32
CMP_STRIDE = 16
SEL_LEN = 64
SEL_TOPK = 16
WINDOW = 512
C_QBLOCK = 32
P_C = C_HEADS * C_HEAD_DIM + 3 * 2 * C_KV * C_HEAD_DIM + 3 * C_HEADS

ROPE_THETA = 500000.0
ROPE_DIMS = C_HEAD_DIM // 4

NEG_INF = -1e30
BIG = 1e9

kernel_name = "hybrid_gmlp_mlstm_nsa_macaron"


def rmsnorm(x, g):
    xf = x.astype(jnp.float32)
    y = xf * lax.rsqrt(jnp.mean(xf * xf, axis=-1, keepdims=True) + EPS)
    return (y * g.astype(jnp.float32)).astype(x.dtype)


def swiglu(x, w_gate, w_up, w_down):
    return (jax.nn.silu(x @ w_gate) * (x @ w_up)) @ w_down


def partial_rope(x):
    S = x.shape[1]
    half = ROPE_DIMS // 2
    inv = 1.0 / (ROPE_THETA ** (jnp.arange(half, dtype=jnp.float32) / half))
    ang = jnp.arange(S, dtype=jnp.float32)[:, None] * inv[None, :]
    cos = jnp.cos(ang)[None, :, None, :]
    sin = jnp.sin(ang)[None, :, None, :]
    xr = x[..., :ROPE_DIMS].astype(jnp.float32)
    x1, x2 = xr[..., :half], xr[..., half:]
    rot = jnp.concatenate([x1 * cos - x2 * sin, x1 * sin + x2 * cos], axis=-1).astype(x.dtype)
    return jnp.concatenate([rot, x[..., ROPE_DIMS:]], axis=-1)


def causal_depthwise_conv(x, w):
    K, C = w.shape
    return lax.conv_general_dilated(x, w[:, None, :].astype(x.dtype), window_strides=(1,),
                                    padding=[(K - 1, 0)], dimension_numbers=('NWC', 'WIO', 'NWC'),
                                    feature_group_count=C)


def masked_softmax(s, valid):
    p = jax.nn.softmax(jnp.where(valid, s, NEG_INF), axis=-1)
    return jnp.where(valid, p, 0.0)


def gmlp_chunk_gating(z, norm_g, w_s, b_s):
    B_, S, _ = z.shape
    z = jax.nn.gelu(z)
    u, v = jnp.split(z, 2, axis=-1)
    v = rmsnorm(v, norm_g)
    nc = S // A_CHUNK
    v = v.reshape(B_, nc, A_CHUNK, A_GROUPS, A_GROUP_DIM)
    mask = jnp.tril(jnp.ones((A_CHUNK, A_CHUNK), dtype=bool))
    w = jnp.where(mask, w_s, 0.0).astype(v.dtype)
    sv = jnp.einsum('gts,bcsge->bctge', w, v) + b_s.T.astype(v.dtype)[None, None, :, :, None]
    return u * sv.reshape(B_, S, A_WIDTH)


def mlstm_chunkwise(q, k, v, i_pre, f_pre):
    B_, S, H, d = q.shape
    L = B_CHUNK
    nc = S // L
    f32 = jnp.float32
    def chunks(a):
        return a.astype(f32).reshape(B_, nc, L, H, d).transpose(1, 0, 3, 2, 4)
    def gchunks(a):
        return a.astype(f32).reshape(B_, nc, L, H).transpose(1, 0, 3, 2)
    qc, kc, vc = chunks(q), chunks(k) * (d ** -0.5), chunks(v)
    ic, lfc = gchunks(i_pre), gchunks(jax.nn.log_sigmoid(f_pre.astype(f32)))
    causal = jnp.tril(jnp.ones((L, L), dtype=bool))

    def step(carry, inp):
        C, n, m = carry
        qb, kb, vb, ib, lfb = inp
        b = jnp.cumsum(lfb, axis=-1)
        logD = jnp.where(causal, b[..., :, None] - b[..., None, :] + ib[..., None, :], -jnp.inf)
        inter = b + m[..., None]
        m_t = jnp.maximum(jnp.max(logD, axis=-1), inter)
        Dm = jnp.exp(logD - m_t[..., None])
        w_inter = jnp.exp(inter - m_t)
        s = jnp.einsum('bhtd,bhsd->bhts', qb, kb) * Dm
        num = jnp.einsum('bhts,bhse->bhte', s, vb) + w_inter[..., None] * jnp.einsum('bhed,bhtd->bhte', C, qb)
        den = jnp.sum(s, axis=-1) + w_inter * jnp.einsum('bhd,bhtd->bht', n, qb)
        h = num / jnp.maximum(jnp.abs(den), jnp.exp(-m_t))[..., None]
        m_new = m_t[..., -1]
        w_s = jnp.exp(b[..., -1:] - b + ib - m_new[..., None])
        w_prev = jnp.exp(b[..., -1] + m - m_new)
        C_new = w_prev[..., None, None] * C + jnp.einsum('bhs,bhse,bhsd->bhed', w_s, vb, kb)
        n_new = w_prev[..., None] * n + jnp.einsum('bhs,bhsd->bhd', w_s, kb)
        return (C_new, n_new, m_new), h

    init = (jnp.zeros((B_, H, d, d), f32), jnp.zeros((B_, H, d), f32), jnp.zeros((B_, H), f32))
    _, hs = lax.scan(step, init, (qc, kc, vc, ic, lfc))
    return hs.transpose(1, 0, 3, 2, 4).reshape(B_, S, H, d)


def gmlp_mlstm_mixer(h, w_in, gate_b, g_norm, g_ws, g_bs, conv_w, m_norm, w_out):
    B_, S, _ = h.shape
    p = h @ w_in
    o1 = 2 * A_WIDTH
    o2 = o1 + 2 * B_WIDTH
    o3 = o2 + B_WIDTH
    o4 = o3 + B_WIDTH
    z_a, qk, v, o, gates = jnp.split(p, [o1, o2, o3, o4], axis=-1)
    y_a = gmlp_chunk_gating(z_a, g_norm, g_ws, g_bs)
    qk = jax.nn.silu(causal_depthwise_conv(qk, conv_w))
    q, k = jnp.split(qk, 2, axis=-1)
    gates = gates.reshape(B_, S, 2, B_HEADS) + gate_b.astype(gates.dtype)
    shp = (B_, S, B_HEADS, B_HEAD_DIM)
    ht = mlstm_chunkwise(q.reshape(shp), k.reshape(shp), v.reshape(shp), gates[:, :, 0], gates[:, :, 1])
    ht = rmsnorm(ht, m_norm.reshape(B_HEADS, B_HEAD_DIM)).reshape(B_, S, B_WIDTH)
    y_b = (jax.nn.sigmoid(o.astype(jnp.float32)) * ht).astype(h.dtype)
    return jnp.concatenate([y_a, y_b], axis=-1) @ w_out


def compress_blocks(x, pos_emb, w1, w2):
    B_, S, G, dh = x.shape
    n_cmp = (S - CMP_LEN) // CMP_STRIDE + 1
    idx = jnp.arange(n_cmp)[:, None] * CMP_STRIDE + jnp.arange(CMP_LEN)[None, :]
    blocks = x[:, idx] + pos_emb.astype(x.dtype)[None, None, :, None, :]
    flat = blocks.transpose(0, 1, 3, 2, 4).reshape(B_, n_cmp, G, CMP_LEN * dh)
    return jax.nn.gelu(flat @ w1) @ w2


def gather_blocks(blocks, idx):
    return jax.vmap(jax.vmap(lambda bl, ix: bl[ix]))(blocks, idx)


def nsa_mixer(h, w_in, cmp_pos, cmp_w1, cmp_w2, w_out):
    B_, S, _ = h.shape
    G, R, dh = C_KV, C_REP, C_HEAD_DIM
    f32 = jnp.float32
    p = h @ w_in
    q, kvs, g = jnp.split(p, [C_HEADS * dh, C_HEADS * dh + 6 * G * dh], axis=-1)
    q = partial_rope(q.reshape(B_, S, C_HEADS, dh)) * (dh ** -0.5)
    kvs = kvs.reshape(B_, S, 3, 2, G, dh)
    gates = jax.nn.sigmoid(g.astype(f32)).reshape(B_, S, 3, G, R).astype(h.dtype)

    k_cmp = compress_blocks(partial_rope(kvs[:, :, 0, 0]), cmp_pos[0], cmp_w1[0], cmp_w2[0])
    v_cmp = compress_blocks(kvs[:, :, 0, 1], cmp_pos[1], cmp_w1[1], cmp_w2[1])
    n_cmp = k_cmp.shape[1]
    cmp_end = jnp.arange(n_cmp) * CMP_STRIDE + CMP_LEN - 1

    n_slc = S // SEL_LEN
    n_top = min(SEL_TOPK, n_slc)
    k_sel = partial_rope(kvs[:, :, 1, 0]).reshape(B_, n_slc, SEL_LEN, G, dh).transpose(0, 3, 1, 2, 4)
    v_sel = kvs[:, :, 1, 1].reshape(B_, n_slc, SEL_LEN, G, dh).transpose(0, 3, 1, 2, 4)
    ci = jnp.arange(n_cmp)[:, None] * CMP_STRIDE
    sj = jnp.arange(n_slc)[None, :] * SEL_LEN
    overlap = jnp.clip(jnp.minimum(ci + CMP_LEN, sj + SEL_LEN) - jnp.maximum(ci, sj), 0).astype(f32) / CMP_LEN
    blk = jnp.arange(n_slc)

    pad = ((0, 0), (WINDOW, 0), (0, 0), (0, 0))
    k_win = jnp.pad(partial_rope(kvs[:, :, 2, 0]), pad)
    v_win = jnp.pad(kvs[:, :, 2, 1], pad)

    def q_block(i):
        t0 = i * C_QBLOCK
        pos = t0 + jnp.arange(C_QBLOCK)
        qb = lax.dynamic_slice_in_dim(q, t0, C_QBLOCK, axis=1).reshape(B_, C_QBLOCK, G, R, dh)
        gb = lax.dynamic_slice_in_dim(gates, t0, C_QBLOCK, axis=1)
        s_c = jnp.einsum('bqgrd,bcgd->bgrqc', qb, k_cmp).astype(f32)
        p_c = masked_softmax(s_c, cmp_end[None, :] <= pos[:, None])
        o_c = jnp.einsum('bgrqc,bcgd->bqgrd', p_c.astype(v_cmp.dtype), v_cmp)
        imp = jnp.einsum('bgrqc,cj->bgqj', p_c, overlap)
        cur = pos // SEL_LEN
        forced = (blk[None, :] == 0) | (blk[None, :] == cur[:, None]) | (blk[None, :] == cur[:, None] - 1)
        future = blk[None, :] > cur[:, None]
        imp = jnp.where(forced, BIG, jnp.where(future, -BIG, imp))
        _, top = lax.top_k(imp, n_top)
        ks = gather_blocks(k_sel, top).reshape(B_, G, C_QBLOCK, n_top * SEL_LEN, dh)
        vs = gather_blocks(v_sel, top).reshape(B_, G, C_QBLOCK, n_top * SEL_LEN, dh)
        tok = (top[..., None] * SEL_LEN + jnp.arange(SEL_LEN)).reshape(B_, G, C_QBLOCK, n_top * SEL_LEN)
        valid_s = (tok <= pos[None, None, :, None])[:, :, None]
        s_s = jnp.einsum('bqgrd,bgqnd->bgrqn', qb, ks).astype(f32)
        p_s = masked_softmax(s_s, valid_s)
        o_s = jnp.einsum('bgrqn,bgqnd->bqgrd', p_s.astype(vs.dtype), vs)
        kw = lax.dynamic_slice_in_dim(k_win, t0, WINDOW + C_QBLOCK, axis=1)
        vw = lax.dynamic_slice_in_dim(v_win, t0, WINDOW + C_QBLOCK, axis=1)
        kpos = t0 - WINDOW + jnp.arange(WINDOW + C_QBLOCK)
        valid_w = (kpos[None, :] <= pos[:, None]) & (pos[:, None] - kpos[None, :] < WINDOW) & (kpos[None, :] >= 0)
        s_w = jnp.einsum('bqgrd,bkgd->bgrqk', qb, kw).astype(f32)
        p_w = masked_softmax(s_w, valid_w)
        o_w = jnp.einsum('bgrqk,bkgd->bqgrd', p_w.astype(vw.dtype), vw)
        out = gb[:, :, 0, :, :, None] * o_c + gb[:, :, 1, :, :, None] * o_s + gb[:, :, 2, :, :, None] * o_w
        return out.reshape(B_, C_QBLOCK, C_HEADS * dh)

    outs = lax.map(q_block, jnp.arange(S // C_QBLOCK))
    y = outs.transpose(1, 0, 2, 3).reshape(B_, S, C_HEADS * dh)
    return y @ w_out


def setup_inputs(seed: int = 0) -> dict:
    key = jax.random.key(seed)
    ks = jax.random.split(key, 24)
    nrm = lambda k, shape, s: jax.random.normal(k, shape, jnp.float32) * s
    f_bias = jnp.broadcast_to(jnp.linspace(3.0, 6.0, B_HEADS, dtype=jnp.float32), (N_EVEN, B_HEADS))
    return {
        "x": nrm(ks[0], (BATCH, SEQ, D_MODEL), 1.0),
        "ffn_norm": 1.0 + nrm(ks[1], (DEPTH, 2, D_MODEL), 0.01),
        "ffn_w_gate": nrm(ks[2], (DEPTH, 2, D_MODEL, D_FF), D_MODEL ** -0.5),
        "ffn_w_up": nrm(ks[3], (DEPTH, 2, D_MODEL, D_FF), D_MODEL ** -0.5),
        "ffn_w_down": nrm(ks[4], (DEPTH, 2, D_FF, D_MODEL), D_FF ** -0.5),
        "mix_norm": 1.0 + nrm(ks[5], (DEPTH, D_MODEL), 0.01),
        "ab_w_in": nrm(ks[6], (N_EVEN, D_MODEL, P_AB), D_MODEL ** -0.5),
        "mlstm_gate_bias": jnp.stack([nrm(ks[7], (N_EVEN, B_HEADS), 0.1),
                                      f_bias + nrm(ks[8], (N_EVEN, B_HEADS), 0.1)], axis=1),
        "gmlp_norm": 1.0 + nrm(ks[9], (N_EVEN, A_WIDTH), 0.01),
        "gmlp_w_s": nrm(ks[10], (N_EVEN, A_GROUPS, A_CHUNK, A_CHUNK), A_CHUNK ** -0.5),
        "gmlp_b_s": 1.0 + nrm(ks[11], (N_EVEN, A_GROUPS, A_CHUNK), 0.02),
        "mlstm_conv": nrm(ks[12], (N_EVEN, B_CONV, 2 * B_WIDTH), B_CONV ** -0.5),
        "mlstm_norm": 1.0 + nrm(ks[13], (N_EVEN, B_WIDTH), 0.01),
        "ab_w_out": nrm(ks[14], (N_EVEN, A_WIDTH + B_WIDTH, D_MODEL), (A_WIDTH + B_WIDTH) ** -0.5),
        "nsa_w_in": nrm(ks[15], (N_ODD, D_MODEL, P_C), D_MODEL ** -0.5),
        "nsa_cmp_pos": nrm(ks[16], (N_ODD, 2, CMP_LEN, C_HEAD_DIM), 0.02),
        "nsa_cmp_w1": nrm(ks[17], (N_ODD, 2, CMP_LEN * C_HEAD_DIM, C_HEAD_DIM), (CMP_LEN * C_HEAD_DIM) ** -0.5),
        "nsa_cmp_w2": nrm(ks[18], (N_ODD, 2, C_HEAD_DIM, C_HEAD_DIM), C_HEAD_DIM ** -0.5),
        "nsa_w_out": nrm(ks[19], (N_ODD, C_HEADS * C_HEAD_DIM, D_MODEL), (C_HEADS * C_HEAD_DIM) ** -0.5),
        "final_norm": 1.0 + nrm(ks[20], (D_MODEL,), 0.01),
    }


def reference(x, ffn_norm, ffn_w_gate, ffn_w_up, ffn_w_down, mix_norm, ab_w_in, mlstm_gate_bias,
              gmlp_norm, gmlp_w_s, gmlp_b_s, mlstm_conv, mlstm_norm, ab_w_out, nsa_w_in, nsa_cmp_pos,
              nsa_cmp_w1, nsa_cmp_w2, nsa_w_out, final_norm):
    h = x
    for layer in range(DEPTH):
        j = layer // 2
        h = h + 0.5 * swiglu(rmsnorm(h, ffn_norm[layer, 0]), ffn_w_gate[layer, 0], ffn_w_up[layer, 0], ffn_w_down[layer, 0])
        hn = rmsnorm(h, mix_norm[layer])
        if layer % 2 == 0:
            h = h + gmlp_mlstm_mixer(hn, ab_w_in[j], mlstm_gate_bias[j], gmlp_norm[j], gmlp_w_s[j], gmlp_b_s[j],
                                     mlstm_conv[j], mlstm_norm[j], ab_w_out[j])
        else:
            h = h + nsa_mixer(hn, nsa_w_in[j], nsa_cmp_pos[j], nsa_cmp_w1[j], nsa_cmp_w2[j], nsa_w_out[j])
        h = h + 0.5 * swiglu(rmsnorm(h, ffn_norm[layer, 1]), ffn_w_gate[layer, 1], ffn_w_up[layer, 1], ffn_w_down[layer, 1])
    return rmsnorm(h, final_norm)
```

```python
import functools
import math

import jax
import jax.numpy as jnp
from jax import lax
from jax.experimental import pallas as pl
from jax.experimental.pallas import tpu as pltpu

F32 = jnp.float32
BF16 = jnp.bfloat16

D_MODEL = 2048
D_FF = 5632
EPS = 1e-6

A_WIDTH = D_MODEL // 2
A_CHUNK = 128
A_GROUPS = 8
A_GROUP_DIM = A_WIDTH // A_GROUPS

B_HEADS = 4
B_WIDTH = D_MODEL // 2
B_HEAD_DIM = B_WIDTH // B_HEADS
B_CHUNK = 128
B_CONV = 4
AB_MAIN = 2 * A_WIDTH + 4 * B_WIDTH

C_HEADS = 16
C_KV = 4
C_REP = C_HEADS // C_KV
C_HEAD_DIM = D_MODEL // C_HEADS
CMP_LEN = 32
CMP_STRIDE = 16
SEL_LEN = 64
SEL_SHIFT = SEL_LEN.bit_length() - 1
SEL_TOPK = 16
WINDOW = 512
C_MAIN = C_HEADS * C_HEAD_DIM + 6 * C_KV * C_HEAD_DIM
ROPE_THETA = 500000.0
ROPE_DIMS = C_HEAD_DIM // 4
NEG_INF = -1e30
BIG = 1e9

LANES = 128
VMEM_LIMIT = 56 * 1024 * 1024

HM_HEADS = C_HEADS + 6 * C_KV
HM_KCMP, HM_VCMP, HM_KSEL, HM_VSEL, HM_KWIN, HM_VWIN = (C_HEADS + i * C_KV for i in range(6))


def _params(*sem):
    return pltpu.CompilerParams(dimension_semantics=sem, vmem_limit_bytes=VMEM_LIMIT)


def _rms(x, g):
    return x * lax.rsqrt(jnp.mean(x * x, axis=-1, keepdims=True) + EPS) * g


def _dot(a, b):
    return jnp.dot(a, b, preferred_element_type=F32)


def _dot_nt(a, b):
    return lax.dot_general(a, b, (((1,), (1,)), ((), ())), preferred_element_type=F32)


def _dot_tn(a, b):
    return lax.dot_general(a, b, (((0,), (0,)), ((), ())), preferred_element_type=F32)


def _dot_f32(a, b):
    return jnp.dot(a, b, preferred_element_type=F32, precision=lax.Precision.HIGHEST)


def _ffn_kernel(h_ref, g_ref, wg_ref, wu_ref, wd_ref, fg_ref, o_ref, n_ref, *, final):
    j = pl.program_id(1)

    @pl.when(j == 0)
    def _():
        x = h_ref[...]
        n_ref[...] = _rms(x, g_ref[...]).astype(BF16)
        o_ref[...] = x

    n = n_ref[...]
    a = _dot(n, wg_ref[...])
    b = _dot(n, wu_ref[...])
    t = (a * jax.nn.sigmoid(a) * (0.5 * b)).astype(BF16)
    o_ref[...] += _dot(t, wd_ref[...])

    if final:
        @pl.when(j == pl.num_programs(1) - 1)
        def _():
            o_ref[...] = _rms(o_ref[...], fg_ref[...])


def _ffn(h, g, wg, wu, wd, fg, *, final, tm=512, tf=512):
    T = h.shape[0]
    return pl.pallas_call(
        functools.partial(_ffn_kernel, final=final),
        grid=(T // tm, D_FF // tf),
        in_specs=[
            pl.BlockSpec((tm, D_MODEL), lambda i, j: (i, 0)),
            pl.BlockSpec((1, D_MODEL), lambda i, j: (0, 0)),
            pl.BlockSpec((D_MODEL, tf), lambda i, j: (0, j)),
            pl.BlockSpec((D_MODEL, tf), lambda i, j: (0, j)),
            pl.BlockSpec((tf, D_MODEL), lambda i, j: (j, 0)),
            pl.BlockSpec((1, D_MODEL), lambda i, j: (0, 0)),
        ],
        out_specs=pl.BlockSpec((tm, D_MODEL), lambda i, j: (i, 0)),
        out_shape=jax.ShapeDtypeStruct((T, D_MODEL), F32),
        scratch_shapes=[pltpu.VMEM((tm, D_MODEL), BF16)],
        compiler_params=_params("parallel", "arbitrary"),
        name="ffn_final" if final else "ffn",
    )(h, g, wg, wu, wd, fg)


def _ab_proj_kernel(h_ref, g_ref, w_ref, wgate_ref, o_ref, og_ref, n_ref):
    j = pl.program_id(1)

    @pl.when(j == 0)
    def _():
        n = _rms(h_ref[...], g_ref[...]).astype(BF16)
        n_ref[...] = n
        og_ref[...] = _dot(n, wgate_ref[...])

    o_ref[...] = _dot(n_ref[...], w_ref[...])


def _ab_proj(h, g, w, wgate, *, tm=512, tn=512):
    T = h.shape[0]
    return pl.pallas_call(
        _ab_proj_kernel,
        grid=(T // tm, AB_MAIN // tn),
        in_specs=[
            pl.BlockSpec((tm, D_MODEL), lambda i, j: (i, 0)),
            pl.BlockSpec((1, D_MODEL), lambda i, j: (0, 0)),
            pl.BlockSpec((D_MODEL, tn), lambda i, j: (0, j)),
            pl.BlockSpec((D_MODEL, LANES), lambda i, j: (0, 0)),
        ],
        out_specs=[
            pl.BlockSpec((tm, tn), lambda i, j: (i, j)),
            pl.BlockSpec((tm, LANES), lambda i, j: (i, 0)),
        ],
        out_shape=[
            jax.ShapeDtypeStruct((T, AB_MAIN), F32),
            jax.ShapeDtypeStruct((T, LANES), F32),
        ],
        scratch_shapes=[pltpu.VMEM((tm, D_MODEL), BF16)],
        compiler_params=_params("parallel", "arbitrary"),
        name="ab_proj",
    )(h, g, w, wgate)


def _gmlp_kernel(u_ref, v_ref, g_ref, ws_ref, bs_ref, o_ref):
    zu = jax.nn.gelu(u_ref[...])
    vn = _rms(jax.nn.gelu(v_ref[...]), g_ref[...]).astype(BF16)
    row = lax.broadcasted_iota(jnp.int32, (A_CHUNK, A_CHUNK), 0)
    col = lax.broadcasted_iota(jnp.int32, (A_CHUNK, A_CHUNK), 1)
    causal = col <= row
    bs = bs_ref[...]
    for gi in range(A_GROUPS):
        sl = slice(gi * A_GROUP_DIM, (gi + 1) * A_GROUP_DIM)
        w = jnp.where(causal, ws_ref[gi], 0.0).astype(BF16)
        sv = _dot(w, vn[:, sl]) + bs[:, gi:gi + 1]
        o_ref[:, sl] = (zu[:, sl] * sv).astype(BF16)


def _gmlp(p, g, ws, bs_t):
    T = p.shape[0]
    return pl.pallas_call(
        _gmlp_kernel,
        grid=(T // A_CHUNK,),
        in_specs=[
            pl.BlockSpec((A_CHUNK, A_WIDTH), lambda c: (c, 0)),
            pl.BlockSpec((A_CHUNK, A_WIDTH), lambda c: (c, 1)),
            pl.BlockSpec((1, A_WIDTH), lambda c: (0, 0)),
            pl.BlockSpec((A_GROUPS, A_CHUNK, A_CHUNK), lambda c: (0, 0, 0)),
            pl.BlockSpec((A_CHUNK, A_GROUPS), lambda c: (0, 0)),
        ],
        out_specs=pl.BlockSpec((A_CHUNK, A_WIDTH), lambda c: (c, 0)),
        out_shape=jax.ShapeDtypeStruct((T, A_WIDTH), BF16),
        compiler_params=_params("parallel"),
        name="gmlp",
    )(p, p, g, ws, bs_t)


def _log_sigmoid(x):
    return jnp.minimum(x, 0.0) - jnp.log1p(jnp.exp(-jnp.abs(x)))


def _mlstm_kernel(qk_ref, v_ref, o_ref, gate_ref, gb_ref, cw_ref, mn_ref, y_ref,
                  xx_ref, c_ref, n_ref, m_ref):
    L, d = B_CHUNK, B_HEAD_DIM
    c = pl.program_id(1)
    tail = 8

    @pl.when(c == 0)
    def _():
        xx_ref[0:tail, :] = jnp.zeros((tail, 2 * B_WIDTH), F32)
        c_ref[...] = jnp.zeros_like(c_ref)
        n_ref[...] = jnp.zeros_like(n_ref)
        m_ref[...] = jnp.zeros_like(m_ref)

    xx_ref[tail:tail + L, :] = qk_ref[...]
    cw = cw_ref[...]
    conv = jnp.zeros((L, 2 * B_WIDTH), F32)
    for kk in range(B_CONV):
        off = tail - (B_CONV - 1) + kk
        conv = conv + cw[kk:kk + 1, :] * xx_ref[off:off + L, :]
    carry_rows = xx_ref[L:L + tail, :]
    xx_ref[0:tail, :] = carry_rows
    qk = conv * jax.nn.sigmoid(conv)

    gcol = gate_ref[...] + gb_ref[...]
    grow = gcol.T
    r_i = lax.broadcasted_iota(jnp.int32, (L, L), 0)
    c_i = lax.broadcasted_iota(jnp.int32, (L, L), 1)
    causal = c_i <= r_i
    tril = causal.astype(F32)
    triu = (r_i <= c_i).astype(F32)
    bcol_all = _dot_f32(tril, _log_sigmoid(gcol))
    brow_all = _dot_f32(_log_sigmoid(grow), triu)

    for hh in range(B_HEADS):
        q = qk[:, hh * d:(hh + 1) * d]
        k = qk[:, B_WIDTH + hh * d:B_WIDTH + (hh + 1) * d] * (d ** -0.5)
        v = v_ref[:, hh * d:(hh + 1) * d]
        qb, kb, vb = q.astype(BF16), k.astype(BF16), v.astype(BF16)
        i_col = gcol[:, hh:hh + 1]
        i_row = grow[hh:hh + 1, :]
        b_col = bcol_all[:, B_HEADS + hh:B_HEADS + hh + 1]
        b_row = brow_all[B_HEADS + hh:B_HEADS + hh + 1, :]
        b_last = b_col[L - 1:L, :]
        m_old = m_ref[hh:hh + 1, 0:1]

        logd = jnp.where(causal, b_col - b_row + i_row, NEG_INF)
        inter = b_col + m_old
        m_t = jnp.maximum(jnp.max(logd, axis=-1, keepdims=True), inter)
        dm = jnp.where(causal, jnp.exp(logd - m_t), 0.0)
        w_inter = jnp.exp(inter - m_t)
        s = _dot_nt(qb, kb) * dm
        ct = c_ref[hh]
        nvec = n_ref[hh:hh + 1, :]
        num = _dot(s.astype(BF16), vb) + w_inter * _dot(qb, ct.astype(BF16))
        den = jnp.sum(s, axis=-1, keepdims=True) + w_inter * jnp.sum(q * nvec, axis=-1, keepdims=True)
        hout = num / jnp.maximum(jnp.abs(den), jnp.exp(-m_t))

        m_new = m_t[L - 1:L, :]
        w_s = jnp.exp(b_last - b_col + i_col - m_new)
        w_prev = jnp.exp(b_last + m_old - m_new)
        c_ref[hh] = w_prev * ct + _dot_tn(kb, (w_s * v).astype(BF16))
        n_ref[hh:hh + 1, :] = w_prev * nvec + jnp.sum(w_s * k, axis=0, keepdims=True)
        m_ref[hh:hh + 1, :] = jnp.broadcast_to(m_new, (1, LANES))

        hn = _rms(hout, mn_ref[:, hh * d:(hh + 1) * d])
        og = jax.nn.sigmoid(o_ref[:, hh * d:(hh + 1) * d])
        y_ref[:, hh * d:(hh + 1) * d] = (og * hn).astype(BF16)


def _mlstm(p, gates, gate_b, conv_w, m_norm, batch):
    T = p.shape[0]
    L = B_CHUNK
    nc = T // batch // L
    qk_blk = (2 * A_WIDTH) // (2 * B_WIDTH)
    v_blk = (2 * A_WIDTH + 2 * B_WIDTH) // B_WIDTH
    row = lambda b, c: b * nc + c
    return pl.pallas_call(
        _mlstm_kernel,
        grid=(batch, nc),
        in_specs=[
            pl.BlockSpec((L, 2 * B_WIDTH), lambda b, c: (row(b, c), qk_blk)),
            pl.BlockSpec((L, B_WIDTH), lambda b, c: (row(b, c), v_blk)),
            pl.BlockSpec((L, B_WIDTH), lambda b, c: (row(b, c), v_blk + 1)),
            pl.BlockSpec((L, LANES), lambda b, c: (row(b, c), 0)),
            pl.BlockSpec((1, LANES), lambda b, c: (0, 0)),
            pl.BlockSpec((B_CONV, 2 * B_WIDTH), lambda b, c: (0, 0)),
            pl.BlockSpec((1, B_WIDTH), lambda b, c: (0, 0)),
        ],
        out_specs=pl.BlockSpec((L, B_WIDTH), lambda b, c: (row(b, c), 0)),
        out_shape=jax.ShapeDtypeStruct((T, B_WIDTH), BF16),
        scratch_shapes=[
            pltpu.VMEM((L + 8, 2 * B_WIDTH), F32),
            pltpu.VMEM((B_HEADS, B_HEAD_DIM, B_HEAD_DIM), F32),
            pltpu.VMEM((8, B_HEAD_DIM), F32),
            pltpu.VMEM((8, LANES), F32),
        ],
        compiler_params=_params("arbitrary", "arbitrary"),
        name="mlstm",
    )(p, p, p, gates, gate_b, conv_w, m_norm)


def _outproj_kernel(*refs, n_lhs):
    h_ref = refs[0]
    o_ref = refs[1 + 2 * n_lhs]
    acc = h_ref[...]
    for i in range(n_lhs):
        acc = acc + _dot(refs[1 + i][...], refs[1 + n_lhs + i][...])
    o_ref[...] = acc


def _outproj(h, lhs, ws, *, tm=512):
    T = h.shape[0]
    n = len(lhs)
    in_specs = [pl.BlockSpec((tm, D_MODEL), lambda i: (i, 0))]
    in_specs += [pl.BlockSpec((tm, a.shape[1]), lambda i: (i, 0)) for a in lhs]
    in_specs += [pl.BlockSpec(w.shape, lambda i: (0, 0)) for w in ws]
    return pl.pallas_call(
        functools.partial(_outproj_kernel, n_lhs=n),
        grid=(T // tm,),
        in_specs=in_specs,
        out_specs=pl.BlockSpec((tm, D_MODEL), lambda i: (i, 0)),
        out_shape=jax.ShapeDtypeStruct((T, D_MODEL), F32),
        compiler_params=_params("parallel"),
        name="outproj",
    )(h, *lhs, *ws)


def _rope(x, ca, sp, sm):
    return x * ca + pltpu.roll(x, ROPE_DIMS // 2, 1) * sp + pltpu.roll(x, LANES - ROPE_DIMS // 2, 1) * sm


def _nsa_proj_kernel(h_ref, g_ref, w_ref, wgate_ref, ca_ref, sp_ref, sm_ref, o_ref, og_ref, n_ref):
    j = pl.program_id(1)

    @pl.when(j == 0)
    def _():
        n = _rms(h_ref[...], g_ref[...]).astype(BF16)
        n_ref[...] = n
        og_ref[...] = jax.nn.sigmoid(_dot(n, wgate_ref[...]))

    res = _dot(n_ref[...], w_ref[...])
    n_q = C_HEADS // C_KV
    is_q = j < n_q
    is_rope = jnp.logical_or(is_q, jnp.logical_and(j >= n_q, (j - n_q) % 2 == 0))

    @pl.when(is_rope)
    def _():
        scale = jnp.where(is_q, C_HEAD_DIM ** -0.5, 1.0).astype(F32)
        ca, sp, sm = ca_ref[...], sp_ref[...], sm_ref[...]
        for gi in range(C_KV):
            x = res[:, gi * LANES:(gi + 1) * LANES]
            o_ref[0, gi] = (_rope(x, ca, sp, sm) * scale).astype(BF16)

    @pl.when(jnp.logical_not(is_rope))
    def _():
        for gi in range(C_KV):
            o_ref[0, gi] = res[:, gi * LANES:(gi + 1) * LANES].astype(BF16)


def _nsa_proj(h, g, w, wgate, ca, sp, sm, batch, *, tm=512):
    T = h.shape[0]
    S = T // batch
    nts = S // tm
    tn = C_KV * C_HEAD_DIM
    return pl.pallas_call(
        _nsa_proj_kernel,
        grid=(T // tm, C_MAIN // tn),
        in_specs=[
            pl.BlockSpec((tm, D_MODEL), lambda i, j: (i, 0)),
            pl.BlockSpec((1, D_MODEL), lambda i, j: (0, 0)),
            pl.BlockSpec((D_MODEL, tn), lambda i, j: (0, j)),
            pl.BlockSpec((D_MODEL, C_KV * LANES), lambda i, j: (0, 0)),
            pl.BlockSpec((tm, LANES), lambda i, j: (i % nts, 0)),
            pl.BlockSpec((tm, LANES), lambda i, j: (i % nts, 0)),
            pl.BlockSpec((tm, LANES), lambda i, j: (i % nts, 0)),
        ],
        out_specs=[
            pl.BlockSpec((1, C_KV, tm, C_HEAD_DIM), lambda i, j: (i // nts, j, i % nts, 0)),
            pl.BlockSpec((tm, C_KV * LANES), lambda i, j: (i, 0)),
        ],
        out_shape=[
            jax.ShapeDtypeStruct((batch, HM_HEADS, S, C_HEAD_DIM), BF16),
            jax.ShapeDtypeStruct((T, C_KV * LANES), F32),
        ],
        scratch_shapes=[pltpu.VMEM((tm, D_MODEL), BF16)],
        compiler_params=_params("parallel", "arbitrary"),
        name="nsa_proj",
    )(h, g, w, wgate, ca, sp, sm)


def _compress_kernel(x_ref, pos_ref, w1_ref, w2_ref, o_ref):
    x = x_ref[0, 0].astype(F32)
    pos = pos_ref[0]
    ya = _dot((x + pos[0:1, :]).astype(BF16), w1_ref[0, 0])
    yb = _dot((x + pos[1:2, :]).astype(BF16), w1_ref[0, 1])
    n_rows = x.shape[0]
    pre = ya + pltpu.roll(yb, n_rows - 1, 0)
    out = _dot(jax.nn.gelu(pre).astype(BF16), w2_ref[0])
    rows = lax.broadcasted_iota(jnp.int32, out.shape, 0)
    o_ref[0, 0, 0] = jnp.where(rows < n_rows - 1, out, 0.0).astype(BF16)


def _compress(qkv, pos, w1, w2):
    B_, _, S, dh = qkv.shape
    rows = S // CMP_STRIDE
    wide = CMP_STRIDE * dh
    x = qkv.reshape(B_, HM_HEADS, rows, wide)
    return pl.pallas_call(
        _compress_kernel,
        grid=(B_, 2, C_KV),
        in_specs=[
            pl.BlockSpec((1, 1, rows, wide), lambda b, kv, g: (b, HM_KCMP + kv * C_KV + g, 0, 0)),
            pl.BlockSpec((1, 2, wide), lambda b, kv, g: (kv, 0, 0)),
            pl.BlockSpec((1, 2, wide, dh), lambda b, kv, g: (kv, 0, 0, 0)),
            pl.BlockSpec((1, dh, dh), lambda b, kv, g: (kv, 0, 0)),
        ],
        out_specs=pl.BlockSpec((1, 1, 1, rows, dh), lambda b, kv, g: (b, kv, g, 0, 0)),
        out_shape=jax.ShapeDtypeStruct((B_, 2, C_KV, rows, dh), BF16),
        compiler_params=_params("parallel", "parallel", "parallel"),
        name="nsa_compress",
    )(x, pos, w1, w2)


def _masked_softmax_parts(s, ok):
    m = jnp.max(jnp.where(ok, s, NEG_INF), axis=-1, keepdims=True)
    p = jnp.where(ok, jnp.exp(s - m), 0.0)
    l = jnp.sum(p, axis=-1, keepdims=True)
    return p, jnp.where(l > 0.0, 1.0 / l, 0.0)


def _nsa_attn_kernel(q_ref, kc_ref, vc_ref, ks_ref, vs_ref, kw_ref, vw_ref, gate_ref, ov_ref, y_ref,
                     m_ref, l_ref, acc_ref, *, tq, tk, n_slc, n_top):
    R, dh = C_REP, C_HEAD_DIM
    t0 = pl.program_id(2) * tq
    q = q_ref[0].reshape(R * tq, dh)
    pos = t0 + lax.broadcasted_iota(jnp.int32, (tq, 1), 0)

    n_cmp_rows = kc_ref.shape[3]
    s_c = _dot_nt(q, kc_ref[0, 0, 0]).reshape(R, tq, n_cmp_rows)
    cmp_end = lax.broadcasted_iota(jnp.int32, (tq, n_cmp_rows), 1) * CMP_STRIDE + (CMP_LEN - 1)
    ok_c = jnp.logical_and(cmp_end <= pos, cmp_end < n_cmp_rows * CMP_STRIDE)[None]
    p_c, inv_c = _masked_softmax_parts(s_c, ok_c)
    p_c = p_c * inv_c
    o_c = _dot(p_c.reshape(R * tq, n_cmp_rows).astype(BF16), vc_ref[0, 0, 0])

    imp = _dot_f32(jnp.sum(p_c, axis=0), ov_ref[...])
    blk = lax.broadcasted_iota(jnp.int32, (tq, n_slc), 1)
    cur = jnp.right_shift(pos, SEL_SHIFT)
    forced = jnp.logical_or(blk == 0, jnp.logical_or(blk == cur, blk == cur - 1))
    imp = jnp.where(forced, BIG, jnp.where(blk > cur, -BIG, imp))
    rank = jnp.zeros((tq, n_slc), F32)
    for i in range(n_slc):
        vi = imp[:, i:i + 1]
        ahead = jnp.logical_or(vi > imp, jnp.logical_and(vi == imp, blk > i))
        rank = rank + jnp.where(ahead, 1.0, 0.0)
    sel = jnp.where(rank < n_top, 1.0, 0.0).astype(BF16)

    m_ref[...] = jnp.full(m_ref.shape, NEG_INF, F32)
    l_ref[...] = jnp.zeros(l_ref.shape, F32)
    acc_ref[...] = jnp.zeros(acc_ref.shape, F32)
    blocks_per_tile = tk // SEL_LEN

    def sweep(kt, carry):
        k0 = pl.multiple_of(kt * tk, tk)
        kblk = ks_ref[0, 0, pl.ds(k0, tk), :]
        vblk = vs_ref[0, 0, pl.ds(k0, tk), :]
        s = _dot_nt(q, kblk).reshape(R, tq, tk)
        e_r = lax.broadcasted_iota(jnp.int32, (n_slc, tk), 0)
        e_c = lax.broadcasted_iota(jnp.int32, (n_slc, tk), 1)
        expand = jnp.where(e_r == kt * blocks_per_tile + jnp.right_shift(e_c, SEL_SHIFT), 1.0, 0.0).astype(BF16)
        member = _dot(sel, expand)
        kpos = k0 + lax.broadcasted_iota(jnp.int32, (tq, tk), 1)
        ok = jnp.logical_and(member > 0.5, kpos <= pos)[None]
        m_old = m_ref[...]
        m_new = jnp.maximum(m_old, jnp.max(jnp.where(ok, s, NEG_INF), axis=-1, keepdims=True))
        p = jnp.where(ok, jnp.exp(s - m_new), 0.0)
        alpha = jnp.exp(m_old - m_new)
        l_ref[...] = alpha * l_ref[...] + jnp.sum(p, axis=-1, keepdims=True)
        pv = _dot(p.reshape(R * tq, tk).astype(BF16), vblk)
        acc_ref[...] = alpha * acc_ref[...] + pv.reshape(R, tq, dh)
        m_ref[...] = m_new
        return carry

    n_tiles = (t0 + tq - 1) // tk + 1
    lax.fori_loop(0, n_tiles, sweep, 0)
    l_s = l_ref[...]
    o_s = acc_ref[...] * jnp.where(l_s > 0.0, 1.0 / l_s, 0.0)

    slab = WINDOW + tq
    w0 = pl.multiple_of(jnp.maximum(t0 - WINDOW, 0), tq)
    kwin = kw_ref[0, 0, pl.ds(w0, slab), :]
    vwin = vw_ref[0, 0, pl.ds(w0, slab), :]
    s_w = _dot_nt(q, kwin).reshape(R, tq, slab)
    kpos_w = w0 + lax.broadcasted_iota(jnp.int32, (tq, slab), 1)
    ok_w = jnp.logical_and(kpos_w <= pos, pos - kpos_w < WINDOW)[None]
    p_w, inv_w = _masked_softmax_parts(s_w, ok_w)
    o_w = _dot(p_w.reshape(R * tq, slab).astype(BF16), vwin).reshape(R, tq, dh) * inv_w

    gates = gate_ref[...]
    o_c = o_c.reshape(R, tq, dh)
    for r in range(R):
        g_c = gates[:, r:r + 1]
        g_s = gates[:, R + r:R + r + 1]
        g_w = gates[:, 2 * R + r:2 * R + r + 1]
        y_ref[:, r * dh:(r + 1) * dh] = (g_c * o_c[r] + g_s * o_s[r] + g_w * o_w[r]).astype(BF16)


def _nsa_attn(qkv, cmp_kv, gates, overlap, *, tq=128, tk=512):
    B_, _, S, dh = qkv.shape
    nq = S // tq
    n_slc = S // SEL_LEN
    n_top = min(SEL_TOPK, n_slc)
    tk = min(tk, S)
    n_cmp_rows = cmp_kv.shape[3]
    R = C_REP
    seq = lambda base: pl.BlockSpec((1, 1, S, dh), lambda b, g, i: (b, base + g, 0, 0))
    return pl.pallas_call(
        functools.partial(_nsa_attn_kernel, tq=tq, tk=tk, n_slc=n_slc, n_top=n_top),
        grid=(B_, C_KV, nq),
        in_specs=[
            pl.BlockSpec((1, R, tq, dh), lambda b, g, i: (b, g, i, 0)),
            pl.BlockSpec((1, 1, 1, n_cmp_rows, dh), lambda b, g, i: (b, 0, g, 0, 0)),
            pl.BlockSpec((1, 1, 1, n_cmp_rows, dh), lambda b, g, i: (b, 1, g, 0, 0)),
            seq(HM_KSEL), seq(HM_VSEL), seq(HM_KWIN), seq(HM_VWIN),
            pl.BlockSpec((tq, LANES), lambda b, g, i: (b * nq + i, g)),
            pl.BlockSpec((n_cmp_rows, n_slc), lambda b, g, i: (0, 0)),
        ],
        out_specs=pl.BlockSpec((tq, R * dh), lambda b, g, i: (b * nq + i, g)),
        out_shape=jax.ShapeDtypeStruct((B_ * S, C_HEADS * dh), BF16),
        scratch_shapes=[
            pltpu.VMEM((R, tq, 1), F32),
            pltpu.VMEM((R, tq, 1), F32),
            pltpu.VMEM((R, tq, dh), F32),
        ],
        compiler_params=_params("parallel", "parallel", "arbitrary"),
        name="nsa_attn",
    )(qkv, cmp_kv, cmp_kv, qkv, qkv, qkv, qkv, gates, overlap)


def _rope_tables(S):
    half = ROPE_DIMS // 2
    inv = 1.0 / (ROPE_THETA ** (jnp.arange(half, dtype=F32) / half))
    ang = jnp.arange(S, dtype=F32)[:, None] * inv[None, :]
    cos, sin = jnp.cos(ang), jnp.sin(ang)
    rest = LANES - ROPE_DIMS
    ca = jnp.concatenate([cos, cos, jnp.ones((S, rest), F32)], axis=-1)
    sp = jnp.concatenate([jnp.zeros((S, half), F32), sin, jnp.zeros((S, rest), F32)], axis=-1)
    sm = jnp.concatenate([-sin, jnp.zeros((S, half + rest), F32)], axis=-1)
    return ca, sp, sm


def _overlap_matrix(n_cmp_rows, n_slc):
    ci = jnp.arange(n_cmp_rows)[:, None] * CMP_STRIDE
    sj = jnp.arange(n_slc)[None, :] * SEL_LEN
    ov = jnp.clip(jnp.minimum(ci + CMP_LEN, sj + SEL_LEN) - jnp.maximum(ci, sj), 0).astype(F32) / CMP_LEN
    return ov


def _gmlp_mlstm_layer(h, mix_g, w_in, gate_b, g_norm, g_ws, g_bs, conv_w, m_norm, w_out, batch):
    w_main = w_in[:, :AB_MAIN].astype(BF16)
    w_gate = jnp.pad(w_in[:, AB_MAIN:], ((0, 0), (0, LANES - 2 * B_HEADS))).astype(BF16)
    p, gates = _ab_proj(h, mix_g.reshape(1, D_MODEL), w_main, w_gate)
    y_a = _gmlp(p, g_norm.reshape(1, A_WIDTH), g_ws, g_bs.T)
    gb = jnp.pad(gate_b.reshape(1, 2 * B_HEADS), ((0, 0), (0, LANES - 2 * B_HEADS)))
    y_b = _mlstm(p, gates, gb, conv_w, m_norm.reshape(1, B_WIDTH), batch)
    w_o = w_out.astype(BF16)
    return _outproj(h, [y_a, y_b], [w_o[:A_WIDTH], w_o[A_WIDTH:]])


def _nsa_layer(h, mix_g, w_in, cmp_pos, cmp_w1, cmp_w2, w_out, batch):
    T = h.shape[0]
    S = T // batch
    R = C_REP
    w_main = w_in[:, :C_MAIN].astype(BF16)
    wg = w_in[:, C_MAIN:].reshape(D_MODEL, 3, C_KV, R).transpose(0, 2, 1, 3).reshape(D_MODEL, C_KV, 3 * R)
    wg = jnp.pad(wg, ((0, 0), (0, 0), (0, LANES - 3 * R))).reshape(D_MODEL, C_KV * LANES).astype(BF16)
    ca, sp, sm = _rope_tables(S)
    qkv, gates = _nsa_proj(h, mix_g.reshape(1, D_MODEL), w_main, wg, ca, sp, sm, batch)
    wide = CMP_STRIDE * C_HEAD_DIM
    cmp_kv = _compress(qkv, cmp_pos.reshape(2, 2, wide), cmp_w1.reshape(2, 2, wide, C_HEAD_DIM).astype(BF16),
                       cmp_w2.astype(BF16))
    overlap = _overlap_matrix(S // CMP_STRIDE, S // SEL_LEN)
    y = _nsa_attn(qkv, cmp_kv, gates, overlap)
    return _outproj(h, [y], [w_out.astype(BF16)])


def kernel(x, ffn_norm, ffn_w_gate, ffn_w_up, ffn_w_down, mix_norm, ab_w_in, mlstm_gate_bias, gmlp_norm, gmlp_w_s,
           gmlp_b_s, mlstm_conv, mlstm_norm, ab_w_out, nsa_w_in, nsa_cmp_pos, nsa_cmp_w1, nsa_cmp_w2, nsa_w_out,
           final_norm):
    batch, S, _ = x.shape
    depth = ffn_norm.shape[0]
    h = x.reshape(batch * S, D_MODEL)
    fg = final_norm.reshape(1, D_MODEL)

    def ffn(h, layer, half, final=False):
        return _ffn(h, ffn_norm[layer, half].reshape(1, D_MODEL), ffn_w_gate[layer, half].astype(BF16),
                    ffn_w_up[layer, half].astype(BF16), ffn_w_down[layer, half].astype(BF16), fg, final=final)

    for layer in range(depth):
        j = layer // 2
        h = ffn(h, layer, 0)
        if layer % 2 == 0:
            h = _gmlp_mlstm_layer(h, mix_norm[layer], ab_w_in[j], mlstm_gate_bias[j], gmlp_norm[j], gmlp_w_s[j],
                                  gmlp_b_s[j], mlstm_conv[j], mlstm_norm[j], ab_w_out[j], batch)
        else:
            h = _nsa_layer(h, mix_norm[layer], nsa_w_in[j], nsa_cmp_pos[j], nsa_cmp_w1[j], nsa_cmp_w2[j],
                           nsa_w_out[j], batch)
        h = ffn(h, layer, 1, final=(layer == depth - 1))
    return h.reshape(batch, S, D_MODEL)
```

```python
import functools
import math

import jax
import jax.numpy as jnp
from jax import lax
from jax.experimental import pallas as pl
from jax.experimental.pallas import tpu as pltpu

F32 = jnp.float32
BF16 = jnp.bfloat16

D_MODEL = 2048
D_FF = 5632
EPS = 1e-6

A_WIDTH = D_MODEL // 2
A_CHUNK = 128
A_GROUPS = 8
A_GROUP_DIM = A_WIDTH // A_GROUPS

B_HEADS = 4
B_WIDTH = D_MODEL // 2
B_HEAD_DIM = B_WIDTH // B_HEADS
B_CHUNK = 128
B_CONV = 4
AB_MAIN = 2 * A_WIDTH + 4 * B_WIDTH

C_HEADS = 16
C_KV = 4
C_REP = C_HEADS // C_KV
C_HEAD_DIM = D_MODEL // C_HEADS
CMP_LEN = 32
CMP_STRIDE = 16
SEL_LEN = 64
SEL_SHIFT = SEL_LEN.bit_length() - 1
SEL_TOPK = 16
WINDOW = 512
C_MAIN = C_HEADS * C_HEAD_DIM + 6 * C_KV * C_HEAD_DIM
ROPE_THETA = 500000.0
ROPE_DIMS = C_HEAD_DIM // 4
NEG_INF = -1e30
BIG = 1e9
LOG2E = math.log2(math.e)

LANES = 128
SUBLANES = 8
VMEM_LIMIT = 56 * 1024 * 1024

HM_HEADS = C_HEADS + 6 * C_KV
HM_KCMP, HM_VCMP, HM_KSEL, HM_VSEL, HM_KWIN, HM_VWIN = (C_HEADS + i * C_KV for i in range(6))


def _params(*sem):
    return pltpu.CompilerParams(dimension_semantics=sem, vmem_limit_bytes=VMEM_LIMIT)


def _rms(x, g):
    return x * lax.rsqrt(jnp.mean(x * x, axis=-1, keepdims=True) + EPS) * g


def _dot(a, b):
    return jnp.dot(a, b, preferred_element_type=F32)


def _dot_nt(a, b):
    return lax.dot_general(a, b, (((1,), (1,)), ((), ())), preferred_element_type=F32)


def _dot_tn(a, b):
    return lax.dot_general(a, b, (((0,), (0,)), ((), ())), preferred_element_type=F32)


def _dot_f32(a, b):
    return jnp.dot(a, b, preferred_element_type=F32, precision=lax.Precision.HIGHEST)


def _ffn_kernel(h_ref, g_ref, wg_ref, wu_ref, wd_ref, fg_ref, o_ref, n_ref, *, final):
    j = pl.program_id(1)

    @pl.when(j == 0)
    def _():
        x = h_ref[...]
        n_ref[...] = _rms(x, g_ref[...]).astype(BF16)
        o_ref[...] = x

    n = n_ref[...]
    a = _dot(n, wg_ref[...])
    b = _dot(n, wu_ref[...])
    t = (a * jax.nn.sigmoid(a) * (0.5 * b)).astype(BF16)
    o_ref[...] += _dot(t, wd_ref[...])

    if final:
        @pl.when(j == pl.num_programs(1) - 1)
        def _():
            o_ref[...] = _rms(o_ref[...], fg_ref[...])


def _ffn(h, g, wg, wu, wd, fg, *, final, tm=512, tf=512):
    T = h.shape[0]
    return pl.pallas_call(
        functools.partial(_ffn_kernel, final=final),
        grid=(T // tm, D_FF // tf),
        in_specs=[
            pl.BlockSpec((tm, D_MODEL), lambda i, j: (i, 0)),
            pl.BlockSpec((1, D_MODEL), lambda i, j: (0, 0)),
            pl.BlockSpec((D_MODEL, tf), lambda i, j: (0, j)),
            pl.BlockSpec((D_MODEL, tf), lambda i, j: (0, j)),
            pl.BlockSpec((tf, D_MODEL), lambda i, j: (j, 0)),
            pl.BlockSpec((1, D_MODEL), lambda i, j: (0, 0)),
        ],
        out_specs=pl.BlockSpec((tm, D_MODEL), lambda i, j: (i, 0)),
        out_shape=jax.ShapeDtypeStruct((T, D_MODEL), F32),
        scratch_shapes=[pltpu.VMEM((tm, D_MODEL), BF16)],
        compiler_params=_params("parallel", "arbitrary"),
        name="ffn_final" if final else "ffn",
    )(h, g, wg, wu, wd, fg)


def _ab_proj_kernel(h_ref, g_ref, w_ref, wgate_ref, o_ref, og_ref, n_ref):
    j = pl.program_id(1)

    @pl.when(j == 0)
    def _():
        n = _rms(h_ref[...], g_ref[...]).astype(BF16)
        n_ref[...] = n
        og_ref[...] = _dot(n, wgate_ref[...])

    o_ref[...] = _dot(n_ref[...], w_ref[...])


def _ab_proj(h, g, w, wgate, *, tm=1024, tn=1024):
    T = h.shape[0]
    return pl.pallas_call(
        _ab_proj_kernel,
        grid=(T // tm, AB_MAIN // tn),
        in_specs=[
            pl.BlockSpec((tm, D_MODEL), lambda i, j: (i, 0)),
            pl.BlockSpec((1, D_MODEL), lambda i, j: (0, 0)),
            pl.BlockSpec((D_MODEL, tn), lambda i, j: (0, j)),
            pl.BlockSpec((D_MODEL, LANES), lambda i, j: (0, 0)),
        ],
        out_specs=[
            pl.BlockSpec((tm, tn), lambda i, j: (i, j)),
            pl.BlockSpec((tm, LANES), lambda i, j: (i, 0)),
        ],
        out_shape=[
            jax.ShapeDtypeStruct((T, AB_MAIN), F32),
            jax.ShapeDtypeStruct((T, LANES), F32),
        ],
        scratch_shapes=[pltpu.VMEM((tm, D_MODEL), BF16)],
        compiler_params=_params("parallel", "arbitrary"),
        name="ab_proj",
    )(h, g, w, wgate)


def _gmlp_kernel(u_ref, v_ref, g_ref, ws_ref, bs_ref, o_ref):
    zu = jax.nn.gelu(u_ref[...])
    vn = _rms(jax.nn.gelu(v_ref[...]), g_ref[...]).astype(BF16)
    row = lax.broadcasted_iota(jnp.int32, (A_CHUNK, A_CHUNK), 0)
    col = lax.broadcasted_iota(jnp.int32, (A_CHUNK, A_CHUNK), 1)
    causal = col <= row
    bs = bs_ref[...]
    for gi in range(A_GROUPS):
        sl = slice(gi * A_GROUP_DIM, (gi + 1) * A_GROUP_DIM)
        w = jnp.where(causal, ws_ref[gi], 0.0).astype(BF16)
        sv = _dot(w, vn[:, sl]) + bs[:, gi:gi + 1]
        o_ref[:, sl] = (zu[:, sl] * sv).astype(BF16)


def _gmlp(p, g, ws, bs_t):
    T = p.shape[0]
    return pl.pallas_call(
        _gmlp_kernel,
        grid=(T // A_CHUNK,),
        in_specs=[
            pl.BlockSpec((A_CHUNK, A_WIDTH), lambda c: (c, 0)),
            pl.BlockSpec((A_CHUNK, A_WIDTH), lambda c: (c, 1)),
            pl.BlockSpec((1, A_WIDTH), lambda c: (0, 0)),
            pl.BlockSpec((A_GROUPS, A_CHUNK, A_CHUNK), lambda c: (0, 0, 0)),
            pl.BlockSpec((A_CHUNK, A_GROUPS), lambda c: (0, 0)),
        ],
        out_specs=pl.BlockSpec((A_CHUNK, A_WIDTH), lambda c: (c, 0)),
        out_shape=jax.ShapeDtypeStruct((T, A_WIDTH), BF16),
        compiler_params=_params("parallel"),
        name="gmlp",
    )(p, p, g, ws, bs_t)


def _log_sigmoid(x):
    return jnp.minimum(x, 0.0) - jnp.log1p(jnp.exp(-jnp.abs(x)))


def _mlstm_kernel(qk_ref, v_ref, o_ref, gate_ref, gb_ref, cw_ref, mn_ref, y_ref,
                  xx_ref, c_ref, n_ref, m_ref):
    L, d = B_CHUNK, B_HEAD_DIM
    c = pl.program_id(1)
    tail = 8

    @pl.when(c == 0)
    def _():
        xx_ref[0:tail, :] = jnp.zeros((tail, 2 * B_WIDTH), F32)
        c_ref[...] = jnp.zeros_like(c_ref)
        n_ref[...] = jnp.zeros_like(n_ref)
        m_ref[...] = jnp.zeros_like(m_ref)

    xx_ref[tail:tail + L, :] = qk_ref[...]
    cw = cw_ref[...]
    conv = jnp.zeros((L, 2 * B_WIDTH), F32)
    for kk in range(B_CONV):
        off = tail - (B_CONV - 1) + kk
        conv = conv + cw[kk:kk + 1, :] * xx_ref[off:off + L, :]
    carry_rows = xx_ref[L:L + tail, :]
    xx_ref[0:tail, :] = carry_rows
    qk = conv * jax.nn.sigmoid(conv)

    gcol = gate_ref[...] + gb_ref[...]
    grow = gcol.T
    r_i = lax.broadcasted_iota(jnp.int32, (L, L), 0)
    c_i = lax.broadcasted_iota(jnp.int32, (L, L), 1)
    causal = c_i <= r_i
    tril = causal.astype(F32)
    triu = (r_i <= c_i).astype(F32)
    bcol_all = _dot_f32(tril, _log_sigmoid(gcol))
    brow_all = _dot_f32(_log_sigmoid(grow), triu)

    for hh in range(B_HEADS):
        q = qk[:, hh * d:(hh + 1) * d]
        k = qk[:, B_WIDTH + hh * d:B_WIDTH + (hh + 1) * d] * (d ** -0.5)
        v = v_ref[:, hh * d:(hh + 1) * d]
        qb, kb, vb = q.astype(BF16), k.astype(BF16), v.astype(BF16)
        i_col = gcol[:, hh:hh + 1]
        i_row = grow[hh:hh + 1, :]
        b_col = bcol_all[:, B_HEADS + hh:B_HEADS + hh + 1]
        b_row = brow_all[B_HEADS + hh:B_HEADS + hh + 1, :]
        b_last = b_col[L - 1:L, :]
        m_old = m_ref[hh:hh + 1, 0:1]

        logd = jnp.where(causal, b_col - b_row + i_row, NEG_INF)
        inter = b_col + m_old
        m_t = jnp.maximum(jnp.max(logd, axis=-1, keepdims=True), inter)
        dm = jnp.where(causal, jnp.exp(logd - m_t), 0.0)
        w_inter = jnp.exp(inter - m_t)
        s = _dot_nt(qb, kb) * dm
        ct = c_ref[hh]
        nvec = n_ref[hh:hh + 1, :]
        num = _dot(s.astype(BF16), vb) + w_inter * _dot(qb, ct.astype(BF16))
        den = jnp.sum(s, axis=-1, keepdims=True) + w_inter * jnp.sum(q * nvec, axis=-1, keepdims=True)
        hout = num / jnp.maximum(jnp.abs(den), jnp.exp(-m_t))

        m_new = m_t[L - 1:L, :]
        w_s = jnp.exp(b_last - b_col + i_col - m_new)
        w_prev = jnp.exp(b_last + m_old - m_new)
        c_ref[hh] = w_prev * ct + _dot_tn(kb, (w_s * v).astype(BF16))
        n_ref[hh:hh + 1, :] = w_prev * nvec + jnp.sum(w_s * k, axis=0, keepdims=True)
        m_ref[hh:hh + 1, :] = jnp.broadcast_to(m_new, (1, LANES))

        hn = _rms(hout, mn_ref[:, hh * d:(hh + 1) * d])
        og = jax.nn.sigmoid(o_ref[:, hh * d:(hh + 1) * d])
        y_ref[:, hh * d:(hh + 1) * d] = (og * hn).astype(BF16)


def _mlstm(p, gates, gate_b, conv_w, m_norm, batch):
    T = p.shape[0]
    L = B_CHUNK
    nc = T // batch // L
    qk_blk = (2 * A_WIDTH) // (2 * B_WIDTH)
    v_blk = (2 * A_WIDTH + 2 * B_WIDTH) // B_WIDTH
    row = lambda b, c: b * nc + c
    return pl.pallas_call(
        _mlstm_kernel,
        grid=(batch, nc),
        in_specs=[
            pl.BlockSpec((L, 2 * B_WIDTH), lambda b, c: (row(b, c), qk_blk)),
            pl.BlockSpec((L, B_WIDTH), lambda b, c: (row(b, c), v_blk)),
            pl.BlockSpec((L, B_WIDTH), lambda b, c: (row(b, c), v_blk + 1)),
            pl.BlockSpec((L, LANES), lambda b, c: (row(b, c), 0)),
            pl.BlockSpec((1, LANES), lambda b, c: (0, 0)),
            pl.BlockSpec((B_CONV, 2 * B_WIDTH), lambda b, c: (0, 0)),
            pl.BlockSpec((1, B_WIDTH), lambda b, c: (0, 0)),
        ],
        out_specs=pl.BlockSpec((L, B_WIDTH), lambda b, c: (row(b, c), 0)),
        out_shape=jax.ShapeDtypeStruct((T, B_WIDTH), BF16),
        scratch_shapes=[
            pltpu.VMEM((L + 8, 2 * B_WIDTH), F32),
            pltpu.VMEM((B_HEADS, B_HEAD_DIM, B_HEAD_DIM), F32),
            pltpu.VMEM((8, B_HEAD_DIM), F32),
            pltpu.VMEM((8, LANES), F32),
        ],
        compiler_params=_params("arbitrary", "arbitrary"),
        name="mlstm",
    )(p, p, p, gates, gate_b, conv_w, m_norm)


def _outproj_kernel(*refs, n_lhs):
    h_ref = refs[0]
    o_ref = refs[1 + 2 * n_lhs]
    acc = h_ref[...]
    for i in range(n_lhs):
        acc = acc + _dot(refs[1 + i][...], refs[1 + n_lhs + i][...])
    o_ref[...] = acc


def _outproj(h, lhs, ws, *, tm=512):
    T = h.shape[0]
    n = len(lhs)
    in_specs = [pl.BlockSpec((tm, D_MODEL), lambda i: (i, 0))]
    in_specs += [pl.BlockSpec((tm, a.shape[1]), lambda i: (i, 0)) for a in lhs]
    in_specs += [pl.BlockSpec(w.shape, lambda i: (0, 0)) for w in ws]
    return pl.pallas_call(
        functools.partial(_outproj_kernel, n_lhs=n),
        grid=(T // tm,),
        in_specs=in_specs,
        out_specs=pl.BlockSpec((tm, D_MODEL), lambda i: (i, 0)),
        out_shape=jax.ShapeDtypeStruct((T, D_MODEL), F32),
        compiler_params=_params("parallel"),
        name="outproj",
    )(h, *lhs, *ws)


def _rope(x, ca, sp, sm):
    return x * ca + pltpu.roll(x, ROPE_DIMS // 2, 1) * sp + pltpu.roll(x, LANES - ROPE_DIMS // 2, 1) * sm


def _nsa_proj_kernel(h_ref, g_ref, w_ref, wgate_ref, ca_ref, sp_ref, sm_ref, o_ref, og_ref, n_ref):
    j = pl.program_id(1)

    @pl.when(j == 0)
    def _():
        n = _rms(h_ref[...], g_ref[...]).astype(BF16)
        n_ref[...] = n
        og_ref[...] = jax.nn.sigmoid(_dot(n, wgate_ref[...]))

    res = _dot(n_ref[...], w_ref[...])
    n_q = C_HEADS // C_KV
    is_q = j < n_q
    is_rope = jnp.logical_or(is_q, jnp.logical_and(j >= n_q, (j - n_q) % 2 == 0))

    @pl.when(is_rope)
    def _():
        scale = jnp.where(is_q, C_HEAD_DIM ** -0.5 * LOG2E, 1.0).astype(F32)
        ca, sp, sm = ca_ref[...], sp_ref[...], sm_ref[...]
        for gi in range(C_KV):
            x = res[:, gi * LANES:(gi + 1) * LANES]
            o_ref[0, gi] = (_rope(x, ca, sp, sm) * scale).astype(BF16)

    @pl.when(jnp.logical_not(is_rope))
    def _():
        for gi in range(C_KV):
            o_ref[0, gi] = res[:, gi * LANES:(gi + 1) * LANES].astype(BF16)


def _nsa_proj(h, g, w, wgate, ca, sp, sm, batch, *, tm=1024):
    T = h.shape[0]
    S = T // batch
    nts = S // tm
    tn = C_KV * C_HEAD_DIM
    return pl.pallas_call(
        _nsa_proj_kernel,
        grid=(T // tm, C_MAIN // tn),
        in_specs=[
            pl.BlockSpec((tm, D_MODEL), lambda i, j: (i, 0)),
            pl.BlockSpec((1, D_MODEL), lambda i, j: (0, 0)),
            pl.BlockSpec((D_MODEL, tn), lambda i, j: (0, j)),
            pl.BlockSpec((D_MODEL, C_KV * LANES), lambda i, j: (0, 0)),
            pl.BlockSpec((tm, LANES), lambda i, j: (i % nts, 0)),
            pl.BlockSpec((tm, LANES), lambda i, j: (i % nts, 0)),
            pl.BlockSpec((tm, LANES), lambda i, j: (i % nts, 0)),
        ],
        out_specs=[
            pl.BlockSpec((1, C_KV, tm, C_HEAD_DIM), lambda i, j: (i // nts, j, i % nts, 0)),
            pl.BlockSpec((tm, C_KV * LANES), lambda i, j: (i, 0)),
        ],
        out_shape=[
            jax.ShapeDtypeStruct((batch, HM_HEADS, S, C_HEAD_DIM), BF16),
            jax.ShapeDtypeStruct((T, C_KV * LANES), F32),
        ],
        scratch_shapes=[pltpu.VMEM((tm, D_MODEL), BF16)],
        compiler_params=_params("parallel", "arbitrary"),
        name="nsa_proj",
    )(h, g, w, wgate, ca, sp, sm)


def _compress_kernel(x_ref, pos_ref, w1_ref, w2_ref, o_ref, xs_ref):
    xs_ref[...] = x_ref[0, 0].astype(F32)
    n_rows = xs_ref.shape[0] // CMP_STRIDE
    dh = xs_ref.shape[1]
    ya = jnp.zeros((n_rows, dh), F32)
    yb = jnp.zeros((n_rows, dh), F32)
    for l in range(CMP_STRIDE):
        xl = xs_ref[pl.ds(l, n_rows, stride=CMP_STRIDE), :]
        ya = ya + _dot((xl + pos_ref[0, l:l + 1, :]).astype(BF16), w1_ref[0, l])
        lb = CMP_STRIDE + l
        yb = yb + _dot((xl + pos_ref[0, lb:lb + 1, :]).astype(BF16), w1_ref[0, lb])
    pre = ya + pltpu.roll(yb, n_rows - 1, 0)
    out = _dot(jax.nn.gelu(pre).astype(BF16), w2_ref[0])
    rows = lax.broadcasted_iota(jnp.int32, out.shape, 0)
    o_ref[0, 0, 0] = jnp.where(rows < n_rows - 1, out, 0.0).astype(BF16)


def _compress(qkv, pos, w1, w2):
    B_, _, S, dh = qkv.shape
    rows = S // CMP_STRIDE
    return pl.pallas_call(
        _compress_kernel,
        grid=(B_, 2, C_KV),
        in_specs=[
            pl.BlockSpec((1, 1, S, dh), lambda b, kv, g: (b, HM_KCMP + kv * C_KV + g, 0, 0)),
            pl.BlockSpec((1, CMP_LEN, dh), lambda b, kv, g: (kv, 0, 0)),
            pl.BlockSpec((1, CMP_LEN, dh, dh), lambda b, kv, g: (kv, 0, 0, 0)),
            pl.BlockSpec((1, dh, dh), lambda b, kv, g: (kv, 0, 0)),
        ],
        out_specs=pl.BlockSpec((1, 1, 1, rows, dh), lambda b, kv, g: (b, kv, g, 0, 0)),
        out_shape=jax.ShapeDtypeStruct((B_, 2, C_KV, rows, dh), BF16),
        scratch_shapes=[pltpu.VMEM((S, dh), F32)],
        compiler_params=_params("parallel", "parallel", "parallel"),
        name="nsa_compress",
    )(qkv, pos, w1, w2)


def _softmax_parts_t(s):
    m = jnp.max(s, axis=0, keepdims=True)
    p = jnp.exp2(s - m)
    return p, jnp.sum(p, axis=0, keepdims=True)


def _topk_membership_t(imp_t, n_top):
    n_blk, tq = imp_t.shape
    groups = n_blk // SUBLANES
    va = [imp_t[a * SUBLANES:(a + 1) * SUBLANES] for a in range(groups)]
    cnt = [jnp.zeros((SUBLANES, tq), F32) for _ in range(groups)]
    sub = lax.broadcasted_iota(jnp.int32, (SUBLANES, tq), 0)
    for i in range(n_blk):
        vi = jnp.broadcast_to(imp_t[i:i + 1, :], (SUBLANES, tq))
        for a in range(groups):
            if i < a * SUBLANES:
                ahead = jnp.where(vi >= va[a], 1.0, 0.0)
            elif i >= (a + 1) * SUBLANES:
                ahead = jnp.where(vi > va[a], 1.0, 0.0)
            else:
                ahead = jnp.where(sub > i - a * SUBLANES, jnp.where(vi >= va[a], 1.0, 0.0),
                                  jnp.where(vi > va[a], 1.0, 0.0))
            cnt[a] = cnt[a] + ahead
    return jnp.concatenate([jnp.where(c < n_top, 1.0, 0.0) for c in cnt], axis=0)


def _nsa_attn_kernel(q_ref, kc_ref, vc_ref, ks_ref, vs_ref, kw_ref, vw_ref, gate_ref, ovt_ref, y_ref,
                     selt_ref, m_ref, l_ref, acc_ref, *, tq, tk, n_slc, n_top):
    R, dh = C_REP, C_HEAD_DIM
    t0 = pl.program_id(2) * tq
    q = q_ref[0].reshape(R * tq, dh)
    pos = t0 + lax.broadcasted_iota(jnp.int32, (1, tq), 1)
    gates_t = gate_ref[...].T

    def per_head(x):
        return jnp.concatenate([x] * R, axis=1)

    n_cmp_rows = kc_ref.shape[3]
    cmp_end = lax.broadcasted_iota(jnp.int32, (n_cmp_rows, tq), 0) * CMP_STRIDE + (CMP_LEN - 1)
    ok_c = jnp.logical_and(cmp_end <= pos, cmp_end < n_cmp_rows * CMP_STRIDE)
    p_c, l_c = _softmax_parts_t(_dot_nt(kc_ref[0, 0, 0], q) + per_head(jnp.where(ok_c, 0.0, NEG_INF)))
    has_c = per_head(pos >= CMP_LEN - 1)
    p_c = p_c * jnp.where(has_c, 1.0 / l_c, 0.0)
    oc_t = _dot_tn(vc_ref[0, 0, 0], p_c.astype(BF16))

    p_sum = p_c[:, 0:tq]
    for r in range(1, R):
        p_sum = p_sum + p_c[:, r * tq:(r + 1) * tq]
    imp_t = _dot_f32(ovt_ref[...], p_sum)
    blk_t = lax.broadcasted_iota(jnp.int32, (n_slc, tq), 0)
    cur_t = jnp.right_shift(pos, SEL_SHIFT)
    forced = jnp.logical_or(blk_t == 0, jnp.logical_or(blk_t == cur_t, blk_t == cur_t - 1))
    imp_t = jnp.where(forced, BIG, jnp.where(blk_t > cur_t, -BIG, imp_t))
    selt_ref[...] = _topk_membership_t(imp_t, n_top)

    m_ref[...] = jnp.full(m_ref.shape, NEG_INF, F32)
    l_ref[...] = jnp.zeros(l_ref.shape, F32)
    acc_ref[...] = jnp.zeros(acc_ref.shape, F32)
    blocks_per_tile = tk // SEL_LEN

    def sweep(kt, carry):
        k0 = pl.multiple_of(kt * tk, tk)
        kblk = ks_ref[0, 0, pl.ds(k0, tk), :]
        vblk = vs_ref[0, 0, pl.ds(k0, tk), :]
        sel_rows = selt_ref[pl.ds(pl.multiple_of(kt * blocks_per_tile, blocks_per_tile), blocks_per_tile), :]
        member = jnp.concatenate([jnp.broadcast_to(sel_rows[j:j + 1, :], (SEL_LEN, tq))
                                  for j in range(blocks_per_tile)], axis=0)
        kpos = k0 + lax.broadcasted_iota(jnp.int32, (tk, tq), 0)
        bias = jnp.where(jnp.logical_and(member > 0.5, kpos <= pos), 0.0, NEG_INF)
        s = _dot_nt(kblk, q) + per_head(bias)
        m_old = m_ref[...]
        m_new = jnp.maximum(m_old, jnp.max(s, axis=0, keepdims=True))
        p = jnp.exp2(s - m_new)
        alpha = jnp.exp2(m_old - m_new)
        l_ref[...] = alpha * l_ref[...] + jnp.sum(p, axis=0, keepdims=True)
        acc_ref[...] = alpha * acc_ref[...] + _dot_tn(vblk, p.astype(BF16))
        m_ref[...] = m_new
        return carry

    n_tiles = (t0 + tq - 1) // tk + 1
    lax.fori_loop(0, n_tiles, sweep, 0)
    os_t = acc_ref[...] * (1.0 / l_ref[...])

    slab = WINDOW + tq
    w0 = pl.multiple_of(jnp.maximum(t0 - WINDOW, 0), tq)
    kwin = kw_ref[0, 0, pl.ds(w0, slab), :]
    vwin = vw_ref[0, 0, pl.ds(w0, slab), :]
    kpos_w = w0 + lax.broadcasted_iota(jnp.int32, (slab, tq), 0)
    ok_w = jnp.logical_and(kpos_w <= pos, pos - kpos_w < WINDOW)
    p_w, l_w = _softmax_parts_t(_dot_nt(kwin, q) + per_head(jnp.where(ok_w, 0.0, NEG_INF)))
    ow_t = _dot_tn(vwin, p_w.astype(BF16)) * (1.0 / l_w)

    for r in range(R):
        cols = slice(r * tq, (r + 1) * tq)
        merged_t = (gates_t[r:r + 1, :] * oc_t[:, cols] + gates_t[R + r:R + r + 1, :] * os_t[:, cols]
                    + gates_t[2 * R + r:2 * R + r + 1, :] * ow_t[:, cols])
        y_ref[:, r * dh:(r + 1) * dh] = merged_t.T.astype(BF16)


def _nsa_attn(qkv, cmp_kv, gates, overlap_t, *, tq=128, tk=512):
    B_, _, S, dh = qkv.shape
    nq = S // tq
    n_slc = S // SEL_LEN
    n_top = min(SEL_TOPK, n_slc)
    assert n_top >= 3 and n_slc % SUBLANES == 0
    tk = min(tk, S)
    assert (tk // SEL_LEN) % SUBLANES == 0 and tq == LANES
    n_cmp_rows = cmp_kv.shape[3]
    R = C_REP
    seq = lambda base: pl.BlockSpec((1, 1, S, dh), lambda b, g, i: (b, base + g, 0, 0))
    return pl.pallas_call(
        functools.partial(_nsa_attn_kernel, tq=tq, tk=tk, n_slc=n_slc, n_top=n_top),
        grid=(B_, C_KV, nq),
        in_specs=[
            pl.BlockSpec((1, R, tq, dh), lambda b, g, i: (b, g, i, 0)),
            pl.BlockSpec((1, 1, 1, n_cmp_rows, dh), lambda b, g, i: (b, 0, g, 0, 0)),
            pl.BlockSpec((1, 1, 1, n_cmp_rows, dh), lambda b, g, i: (b, 1, g, 0, 0)),
            seq(HM_KSEL), seq(HM_VSEL), seq(HM_KWIN), seq(HM_VWIN),
            pl.BlockSpec((tq, LANES), lambda b, g, i: (b * nq + i, g)),
            pl.BlockSpec((n_slc, n_cmp_rows), lambda b, g, i: (0, 0)),
        ],
        out_specs=pl.BlockSpec((tq, R * dh), lambda b, g, i: (b * nq + i, g)),
        out_shape=jax.ShapeDtypeStruct((B_ * S, C_HEADS * dh), BF16),
        scratch_shapes=[
            pltpu.VMEM((n_slc, tq), F32),
            pltpu.VMEM((1, R * tq), F32),
            pltpu.VMEM((1, R * tq), F32),
            pltpu.VMEM((dh, R * tq), F32),
        ],
        compiler_params=_params("parallel", "parallel", "arbitrary"),
        name="nsa_attn",
    )(qkv, cmp_kv, cmp_kv, qkv, qkv, qkv, qkv, gates, overlap_t)


def _rope_tables(S):
    half = ROPE_DIMS // 2
    inv = 1.0 / (ROPE_THETA ** (jnp.arange(half, dtype=F32) / half))
    ang = jnp.arange(S, dtype=F32)[:, None] * inv[None, :]
    cos, sin = jnp.cos(ang), jnp.sin(ang)
    rest = LANES - ROPE_DIMS
    ca = jnp.concatenate([cos, cos, jnp.ones((S, rest), F32)], axis=-1)
    sp = jnp.concatenate([jnp.zeros((S, half), F32), sin, jnp.zeros((S, rest), F32)], axis=-1)
    sm = jnp.concatenate([-sin, jnp.zeros((S, half + rest), F32)], axis=-1)
    return ca, sp, sm


def _overlap_matrix(n_cmp_rows, n_slc):
    ci = jnp.arange(n_cmp_rows)[:, None] * CMP_STRIDE
    sj = jnp.arange(n_slc)[None, :] * SEL_LEN
    ov = jnp.clip(jnp.minimum(ci + CMP_LEN, sj + SEL_LEN) - jnp.maximum(ci, sj), 0).astype(F32) / CMP_LEN
    return ov


def _gmlp_mlstm_layer(h, mix_g, w_in, gate_b, g_norm, g_ws, g_bs, conv_w, m_norm, w_out, batch):
    w_main = w_in[:, :AB_MAIN].astype(BF16)
    w_gate = jnp.pad(w_in[:, AB_MAIN:], ((0, 0), (0, LANES - 2 * B_HEADS))).astype(BF16)
    p, gates = _ab_proj(h, mix_g.reshape(1, D_MODEL), w_main, w_gate)
    y_a = _gmlp(p, g_norm.reshape(1, A_WIDTH), g_ws, g_bs.T)
    gb = jnp.pad(gate_b.reshape(1, 2 * B_HEADS), ((0, 0), (0, LANES - 2 * B_HEADS)))
    y_b = _mlstm(p, gates, gb, conv_w, m_norm.reshape(1, B_WIDTH), batch)
    w_o = w_out.astype(BF16)
    return _outproj(h, [y_a, y_b], [w_o[:A_WIDTH], w_o[A_WIDTH:]])


def _nsa_layer(h, mix_g, w_in, cmp_pos, cmp_w1, cmp_w2, w_out, batch):
    T = h.shape[0]
    S = T // batch
    R = C_REP
    w_main = w_in[:, :C_MAIN].astype(BF16)
    wg = w_in[:, C_MAIN:].reshape(D_MODEL, 3, C_KV, R).transpose(0, 2, 1, 3).reshape(D_MODEL, C_KV, 3 * R)
    wg = jnp.pad(wg, ((0, 0), (0, 0), (0, LANES - 3 * R))).reshape(D_MODEL, C_KV * LANES).astype(BF16)
    ca, sp, sm = _rope_tables(S)
    qkv, gates = _nsa_proj(h, mix_g.reshape(1, D_MODEL), w_main, wg, ca, sp, sm, batch)
    cmp_kv = _compress(qkv, cmp_pos, cmp_w1.reshape(2, CMP_LEN, C_HEAD_DIM, C_HEAD_DIM).astype(BF16),
                       cmp_w2.astype(BF16))
    overlap_t = _overlap_matrix(S // CMP_STRIDE, S // SEL_LEN).T
    y = _nsa_attn(qkv, cmp_kv, gates, overlap_t)
    return _outproj(h, [y], [w_out.astype(BF16)])


def kernel(x, ffn_norm, ffn_w_gate, ffn_w_up, ffn_w_down, mix_norm, ab_w_in, mlstm_gate_bias, gmlp_norm, gmlp_w_s,
           gmlp_b_s, mlstm_conv, mlstm_norm, ab_w_out, nsa_w_in, nsa_cmp_pos, nsa_cmp_w1, nsa_cmp_w2, nsa_w_out,
           final_norm):
    batch, S, _ = x.shape
    depth = ffn_norm.shape[0]
    h = x.reshape(batch * S, D_MODEL)
    fg = final_norm.reshape(1, D_MODEL)

    def ffn(h, layer, half, final=False):
        return _ffn(h, ffn_norm[layer, half].reshape(1, D_MODEL), ffn_w_gate[layer, half].astype(BF16),
                    ffn_w_up[layer, half].astype(BF16), ffn_w_down[layer, half].astype(BF16), fg, final=final)

    for layer in range(depth):
        j = layer // 2
        h = ffn(h, layer, 0)
        if layer % 2 == 0:
            h = _gmlp_mlstm_layer(h, mix_norm[layer], ab_w_in[j], mlstm_gate_bias[j], gmlp_norm[j], gmlp_w_s[j],
                                  gmlp_b_s[j], mlstm_conv[j], mlstm_norm[j], ab_w_out[j], batch)
        else:
            h = _nsa_layer(h, mix_norm[layer], nsa_w_in[j], nsa_cmp_pos[j], nsa_cmp_w1[j], nsa_cmp_w2[j],
                           nsa_w_out[j], batch)
        h = ffn(h, layer, 1, final=(layer == depth - 1))
    return h.reshape(batch, S, D_MODEL)
```

```python
import functools
import math

import jax
import jax.numpy as jnp
from jax import lax
from jax.experimental import pallas as pl
from jax.experimental.pallas import tpu as pltpu

F32 = jnp.float32
BF16 = jnp.bfloat16

D_MODEL = 2048
D_FF = 5632
EPS = 1e-6

A_WIDTH = D_MODEL // 2
A_CHUNK = 128
A_GROUPS = 8
A_GROUP_DIM = A_WIDTH // A_GROUPS

B_HEADS = 4
B_WIDTH = D_MODEL // 2
B_HEAD_DIM = B_WIDTH // B_HEADS
B_CHUNK = 128
B_CONV = 4
AB_MAIN = 2 * A_WIDTH + 4 * B_WIDTH

C_HEADS = 16
C_KV = 4
C_REP = C_HEADS // C_KV
C_HEAD_DIM = D_MODEL // C_HEADS
CMP_LEN = 32
CMP_STRIDE = 16
SEL_LEN = 64
SEL_SHIFT = SEL_LEN.bit_length() - 1
SEL_TOPK = 16
WINDOW = 512
C_MAIN = C_HEADS * C_HEAD_DIM + 6 * C_KV * C_HEAD_DIM
ROPE_THETA = 500000.0
ROPE_DIMS = C_HEAD_DIM // 4
NEG_INF = -1e30
BIG = 1e9
LOG2E = math.log2(math.e)

LANES = 128
SUBLANES = 8
VMEM_LIMIT = 56 * 1024 * 1024

HM_HEADS = C_HEADS + 6 * C_KV
HM_KCMP, HM_VCMP, HM_KSEL, HM_VSEL, HM_KWIN, HM_VWIN = (C_HEADS + i * C_KV for i in range(6))


def _params(*sem):
    return pltpu.CompilerParams(dimension_semantics=sem, vmem_limit_bytes=VMEM_LIMIT)


def _rms(x, g):
    return x * lax.rsqrt(jnp.mean(x * x, axis=-1, keepdims=True) + EPS) * g


def _dot(a, b):
    return jnp.dot(a, b, preferred_element_type=F32)


def _dot_nt(a, b):
    return lax.dot_general(a, b, (((1,), (1,)), ((), ())), preferred_element_type=F32)


def _dot_tn(a, b):
    return lax.dot_general(a, b, (((0,), (0,)), ((), ())), preferred_element_type=F32)


def _dot_f32(a, b):
    return jnp.dot(a, b, preferred_element_type=F32, precision=lax.Precision.HIGHEST)


def _ffn_kernel(h_ref, g_ref, wg_ref, wu_ref, wd_ref, fg_ref, o_ref, n_ref, *, final):
    j = pl.program_id(1)

    @pl.when(j == 0)
    def _():
        x = h_ref[...]
        n_ref[...] = _rms(x, g_ref[...]).astype(BF16)
        o_ref[...] = x

    n = n_ref[...]
    a = _dot(n, wg_ref[...])
    b = _dot(n, wu_ref[...])
    t = (a * jax.nn.sigmoid(a) * (0.5 * b)).astype(BF16)
    o_ref[...] += _dot(t, wd_ref[...])

    if final:
        @pl.when(j == pl.num_programs(1) - 1)
        def _():
            o_ref[...] = _rms(o_ref[...], fg_ref[...])


def _ffn(h, g, wg, wu, wd, fg, layer, half, *, final, tm=1024, tf=512):
    T = h.shape[0]
    return pl.pallas_call(
        functools.partial(_ffn_kernel, final=final),
        grid=(T // tm, D_FF // tf),
        in_specs=[
            pl.BlockSpec((tm, D_MODEL), lambda i, j: (i, 0)),
            pl.BlockSpec((1, D_MODEL), lambda i, j: (0, 0)),
            pl.BlockSpec((None, None, D_MODEL, tf), lambda i, j: (layer, half, 0, j)),
            pl.BlockSpec((None, None, D_MODEL, tf), lambda i, j: (layer, half, 0, j)),
            pl.BlockSpec((None, None, tf, D_MODEL), lambda i, j: (layer, half, j, 0)),
            pl.BlockSpec((1, D_MODEL), lambda i, j: (0, 0)),
        ],
        out_specs=pl.BlockSpec((tm, D_MODEL), lambda i, j: (i, 0)),
        out_shape=jax.ShapeDtypeStruct((T, D_MODEL), F32),
        scratch_shapes=[pltpu.VMEM((tm, D_MODEL), BF16)],
        compiler_params=_params("parallel", "arbitrary"),
        name="ffn_final" if final else "ffn",
    )(h, g, wg, wu, wd, fg)


def _ab_proj_kernel(h_ref, g_ref, w_ref, wgate_ref, o_ref, og_ref, n_ref):
    j = pl.program_id(1)

    @pl.when(j == 0)
    def _():
        n = _rms(h_ref[...], g_ref[...]).astype(BF16)
        n_ref[...] = n
        og_ref[...] = _dot(n, wgate_ref[...])

    o_ref[...] = _dot(n_ref[...], w_ref[...])


def _ab_proj(h, g, w, wgate, *, tm=1024, tn=1024):
    T = h.shape[0]
    return pl.pallas_call(
        _ab_proj_kernel,
        grid=(T // tm, AB_MAIN // tn),
        in_specs=[
            pl.BlockSpec((tm, D_MODEL), lambda i, j: (i, 0)),
            pl.BlockSpec((1, D_MODEL), lambda i, j: (0, 0)),
            pl.BlockSpec((D_MODEL, tn), lambda i, j: (0, j)),
            pl.BlockSpec((D_MODEL, LANES), lambda i, j: (0, 0)),
        ],
        out_specs=[
            pl.BlockSpec((tm, tn), lambda i, j: (i, j)),
            pl.BlockSpec((tm, LANES), lambda i, j: (i, 0)),
        ],
        out_shape=[
            jax.ShapeDtypeStruct((T, AB_MAIN), F32),
            jax.ShapeDtypeStruct((T, LANES), F32),
        ],
        scratch_shapes=[pltpu.VMEM((tm, D_MODEL), BF16)],
        compiler_params=_params("parallel", "arbitrary"),
        name="ab_proj",
    )(h, g, w, wgate)


def _gmlp_kernel(u_ref, v_ref, g_ref, ws_ref, bs_ref, o_ref):
    zu = jax.nn.gelu(u_ref[...])
    vn = _rms(jax.nn.gelu(v_ref[...]), g_ref[...]).astype(BF16)
    row = lax.broadcasted_iota(jnp.int32, (A_CHUNK, A_CHUNK), 0)
    col = lax.broadcasted_iota(jnp.int32, (A_CHUNK, A_CHUNK), 1)
    causal = col <= row
    bs = bs_ref[...]
    for gi in range(A_GROUPS):
        sl = slice(gi * A_GROUP_DIM, (gi + 1) * A_GROUP_DIM)
        w = jnp.where(causal, ws_ref[gi], 0.0).astype(BF16)
        sv = _dot(w, vn[:, sl]) + bs[:, gi:gi + 1]
        o_ref[:, sl] = (zu[:, sl] * sv).astype(BF16)


def _gmlp(p, g, ws, bs_t):
    T = p.shape[0]
    return pl.pallas_call(
        _gmlp_kernel,
        grid=(T // A_CHUNK,),
        in_specs=[
            pl.BlockSpec((A_CHUNK, A_WIDTH), lambda c: (c, 0)),
            pl.BlockSpec((A_CHUNK, A_WIDTH), lambda c: (c, 1)),
            pl.BlockSpec((1, A_WIDTH), lambda c: (0, 0)),
            pl.BlockSpec((A_GROUPS, A_CHUNK, A_CHUNK), lambda c: (0, 0, 0)),
            pl.BlockSpec((A_CHUNK, A_GROUPS), lambda c: (0, 0)),
        ],
        out_specs=pl.BlockSpec((A_CHUNK, A_WIDTH), lambda c: (c, 0)),
        out_shape=jax.ShapeDtypeStruct((T, A_WIDTH), BF16),
        compiler_params=_params("parallel"),
        name="gmlp",
    )(p, p, g, ws, bs_t)


def _log_sigmoid(x):
    return jnp.minimum(x, 0.0) - jnp.log1p(jnp.exp(-jnp.abs(x)))


def _mlstm_kernel(qk_ref, v_ref, o_ref, gate_ref, gb_ref, cw_ref, mn_ref, y_ref,
                  xx_ref, c_ref, n_ref, m_ref):
    L, d = B_CHUNK, B_HEAD_DIM
    c = pl.program_id(1)
    tail = 8

    @pl.when(c == 0)
    def _():
        xx_ref[0:tail, :] = jnp.zeros((tail, 2 * B_WIDTH), F32)
        c_ref[...] = jnp.zeros_like(c_ref)
        n_ref[...] = jnp.zeros_like(n_ref)
        m_ref[...] = jnp.zeros_like(m_ref)

    xx_ref[tail:tail + L, :] = qk_ref[...]
    cw = cw_ref[...]
    conv = jnp.zeros((L, 2 * B_WIDTH), F32)
    for kk in range(B_CONV):
        off = tail - (B_CONV - 1) + kk
        conv = conv + cw[kk:kk + 1, :] * xx_ref[off:off + L, :]
    carry_rows = xx_ref[L:L + tail, :]
    xx_ref[0:tail, :] = carry_rows
    qk = conv * jax.nn.sigmoid(conv)

    gcol = gate_ref[...] + gb_ref[...]
    grow = gcol.T
    r_i = lax.broadcasted_iota(jnp.int32, (L, L), 0)
    c_i = lax.broadcasted_iota(jnp.int32, (L, L), 1)
    causal = c_i <= r_i
    tril = causal.astype(F32)
    triu = (r_i <= c_i).astype(F32)
    bcol_all = _dot_f32(tril, _log_sigmoid(gcol))
    brow_all = _dot_f32(_log_sigmoid(grow), triu)

    for hh in range(B_HEADS):
        q = qk[:, hh * d:(hh + 1) * d]
        k = qk[:, B_WIDTH + hh * d:B_WIDTH + (hh + 1) * d] * (d ** -0.5)
        v = v_ref[:, hh * d:(hh + 1) * d]
        qb, kb, vb = q.astype(BF16), k.astype(BF16), v.astype(BF16)
        i_col = gcol[:, hh:hh + 1]
        i_row = grow[hh:hh + 1, :]
        b_col = bcol_all[:, B_HEADS + hh:B_HEADS + hh + 1]
        b_row = brow_all[B_HEADS + hh:B_HEADS + hh + 1, :]
        b_last = b_col[L - 1:L, :]
        m_old = m_ref[hh:hh + 1, 0:1]

        logd = jnp.where(causal, b_col - b_row + i_row, NEG_INF)
        inter = b_col + m_old
        m_t = jnp.maximum(jnp.max(logd, axis=-1, keepdims=True), inter)
        dm = jnp.where(causal, jnp.exp(logd - m_t), 0.0)
        w_inter = jnp.exp(inter - m_t)
        s = _dot_nt(qb, kb) * dm
        ct = c_ref[hh]
        nvec = n_ref[hh:hh + 1, :]
        num = _dot(s.astype(BF16), vb) + w_inter * _dot(qb, ct.astype(BF16))
        den = jnp.sum(s, axis=-1, keepdims=True) + w_inter * jnp.sum(q * nvec, axis=-1, keepdims=True)
        hout = num / jnp.maximum(jnp.abs(den), jnp.exp(-m_t))

        m_new = m_t[L - 1:L, :]
        w_s = jnp.exp(b_last - b_col + i_col - m_new)
        w_prev = jnp.exp(b_last + m_old - m_new)
        c_ref[hh] = w_prev * ct + _dot_tn(kb, (w_s * v).astype(BF16))
        n_ref[hh:hh + 1, :] = w_prev * nvec + jnp.sum(w_s * k, axis=0, keepdims=True)
        m_ref[hh:hh + 1, :] = jnp.broadcast_to(m_new, (1, LANES))

        hn = _rms(hout, mn_ref[:, hh * d:(hh + 1) * d])
        og = jax.nn.sigmoid(o_ref[:, hh * d:(hh + 1) * d])
        y_ref[:, hh * d:(hh + 1) * d] = (og * hn).astype(BF16)


def _mlstm(p, gates, gate_b, conv_w, m_norm, batch):
    T = p.shape[0]
    L = B_CHUNK
    nc = T // batch // L
    qk_blk = (2 * A_WIDTH) // (2 * B_WIDTH)
    v_blk = (2 * A_WIDTH + 2 * B_WIDTH) // B_WIDTH
    row = lambda b, c: b * nc + c
    return pl.pallas_call(
        _mlstm_kernel,
        grid=(batch, nc),
        in_specs=[
            pl.BlockSpec((L, 2 * B_WIDTH), lambda b, c: (row(b, c), qk_blk)),
            pl.BlockSpec((L, B_WIDTH), lambda b, c: (row(b, c), v_blk)),
            pl.BlockSpec((L, B_WIDTH), lambda b, c: (row(b, c), v_blk + 1)),
            pl.BlockSpec((L, LANES), lambda b, c: (row(b, c), 0)),
            pl.BlockSpec((1, LANES), lambda b, c: (0, 0)),
            pl.BlockSpec((B_CONV, 2 * B_WIDTH), lambda b, c: (0, 0)),
            pl.BlockSpec((1, B_WIDTH), lambda b, c: (0, 0)),
        ],
        out_specs=pl.BlockSpec((L, B_WIDTH), lambda b, c: (row(b, c), 0)),
        out_shape=jax.ShapeDtypeStruct((T, B_WIDTH), BF16),
        scratch_shapes=[
            pltpu.VMEM((L + 8, 2 * B_WIDTH), F32),
            pltpu.VMEM((B_HEADS, B_HEAD_DIM, B_HEAD_DIM), F32),
            pltpu.VMEM((8, B_HEAD_DIM), F32),
            pltpu.VMEM((8, LANES), F32),
        ],
        compiler_params=_params("arbitrary", "arbitrary"),
        name="mlstm",
    )(p, p, p, gates, gate_b, conv_w, m_norm)


def _outproj_kernel(*refs, n_lhs):
    h_ref = refs[0]
    o_ref = refs[1 + 2 * n_lhs]
    acc = h_ref[...]
    for i in range(n_lhs):
        acc = acc + _dot(refs[1 + i][...], refs[1 + n_lhs + i][...])
    o_ref[...] = acc


def _outproj(h, lhs, w, *, tm=512):
    T = h.shape[0]
    n = len(lhs)
    width = lhs[0].shape[1]
    assert all(a.shape[1] == width for a in lhs) and w.shape[0] == n * width
    in_specs = [pl.BlockSpec((tm, D_MODEL), lambda i: (i, 0))]
    in_specs += [pl.BlockSpec((tm, width), lambda i: (i, 0)) for _ in lhs]
    in_specs += [pl.BlockSpec((width, D_MODEL), functools.partial(lambda i, k: (k, 0), k=k)) for k in range(n)]
    return pl.pallas_call(
        functools.partial(_outproj_kernel, n_lhs=n),
        grid=(T // tm,),
        in_specs=in_specs,
        out_specs=pl.BlockSpec((tm, D_MODEL), lambda i: (i, 0)),
        out_shape=jax.ShapeDtypeStruct((T, D_MODEL), F32),
        compiler_params=_params("parallel"),
        name="outproj",
    )(h, *lhs, *([w] * n))


def _rope(x, ca, sp, sm):
    return x * ca + pltpu.roll(x, ROPE_DIMS // 2, 1) * sp + pltpu.roll(x, LANES - ROPE_DIMS // 2, 1) * sm


def _nsa_proj_kernel(h_ref, g_ref, w_ref, wgate_ref, ca_ref, sp_ref, sm_ref, o_ref, og_ref, n_ref):
    j = pl.program_id(1)

    @pl.when(j == 0)
    def _():
        n = _rms(h_ref[...], g_ref[...]).astype(BF16)
        n_ref[...] = n
        og_ref[...] = jax.nn.sigmoid(_dot(n, wgate_ref[...]))

    res = _dot(n_ref[...], w_ref[...])
    n_q = C_HEADS // C_KV
    is_q = j < n_q
    is_rope = jnp.logical_or(is_q, jnp.logical_and(j >= n_q, (j - n_q) % 2 == 0))

    @pl.when(is_rope)
    def _():
        scale = jnp.where(is_q, C_HEAD_DIM ** -0.5 * LOG2E, 1.0).astype(F32)
        ca, sp, sm = ca_ref[...], sp_ref[...], sm_ref[...]
        for gi in range(C_KV):
            x = res[:, gi * LANES:(gi + 1) * LANES]
            o_ref[0, gi] = (_rope(x, ca, sp, sm) * scale).astype(BF16)

    @pl.when(jnp.logical_not(is_rope))
    def _():
        for gi in range(C_KV):
            o_ref[0, gi] = res[:, gi * LANES:(gi + 1) * LANES].astype(BF16)


def _nsa_proj(h, g, w, wgate, ca, sp, sm, batch, *, tm=1024):
    T = h.shape[0]
    S = T // batch
    nts = S // tm
    tn = C_KV * C_HEAD_DIM
    return pl.pallas_call(
        _nsa_proj_kernel,
        grid=(T // tm, C_MAIN // tn),
        in_specs=[
            pl.BlockSpec((tm, D_MODEL), lambda i, j: (i, 0)),
            pl.BlockSpec((1, D_MODEL), lambda i, j: (0, 0)),
            pl.BlockSpec((D_MODEL, tn), lambda i, j: (0, j)),
            pl.BlockSpec((D_MODEL, C_KV * LANES), lambda i, j: (0, 0)),
            pl.BlockSpec((tm, LANES), lambda i, j: (i % nts, 0)),
            pl.BlockSpec((tm, LANES), lambda i, j: (i % nts, 0)),
            pl.BlockSpec((tm, LANES), lambda i, j: (i % nts, 0)),
        ],
        out_specs=[
            pl.BlockSpec((1, C_KV, tm, C_HEAD_DIM), lambda i, j: (i // nts, j, i % nts, 0)),
            pl.BlockSpec((tm, C_KV * LANES), lambda i, j: (i, 0)),
        ],
        out_shape=[
            jax.ShapeDtypeStruct((batch, HM_HEADS, S, C_HEAD_DIM), BF16),
            jax.ShapeDtypeStruct((T, C_KV * LANES), F32),
        ],
        scratch_shapes=[pltpu.VMEM((tm, D_MODEL), BF16)],
        compiler_params=_params("parallel", "arbitrary"),
        name="nsa_proj",
    )(h, g, w, wgate, ca, sp, sm)


def _compress_kernel(x_ref, pos_ref, w1_ref, w2_ref, o_ref, xs_ref):
    xs_ref[...] = x_ref[0, 0].astype(F32)
    n_rows = xs_ref.shape[0] // CMP_STRIDE
    dh = xs_ref.shape[1]
    ya = jnp.zeros((n_rows, dh), F32)
    yb = jnp.zeros((n_rows, dh), F32)
    for l in range(CMP_STRIDE):
        xl = xs_ref[pl.ds(l, n_rows, stride=CMP_STRIDE), :]
        ya = ya + _dot((xl + pos_ref[0, l:l + 1, :]).astype(BF16), w1_ref[0, l])
        lb = CMP_STRIDE + l
        yb = yb + _dot((xl + pos_ref[0, lb:lb + 1, :]).astype(BF16), w1_ref[0, lb])
    pre = ya + pltpu.roll(yb, n_rows - 1, 0)
    out = _dot(jax.nn.gelu(pre).astype(BF16), w2_ref[0])
    rows = lax.broadcasted_iota(jnp.int32, out.shape, 0)
    o_ref[0, 0, 0] = jnp.where(rows < n_rows - 1, out, 0.0).astype(BF16)


def _compress(qkv, pos, w1, w2):
    B_, _, S, dh = qkv.shape
    rows = S // CMP_STRIDE
    return pl.pallas_call(
        _compress_kernel,
        grid=(B_, 2, C_KV),
        in_specs=[
            pl.BlockSpec((1, 1, S, dh), lambda b, kv, g: (b, HM_KCMP + kv * C_KV + g, 0, 0)),
            pl.BlockSpec((1, CMP_LEN, dh), lambda b, kv, g: (kv, 0, 0)),
            pl.BlockSpec((1, CMP_LEN, dh, dh), lambda b, kv, g: (kv, 0, 0, 0)),
            pl.BlockSpec((1, dh, dh), lambda b, kv, g: (kv, 0, 0)),
        ],
        out_specs=pl.BlockSpec((1, 1, 1, rows, dh), lambda b, kv, g: (b, kv, g, 0, 0)),
        out_shape=jax.ShapeDtypeStruct((B_, 2, C_KV, rows, dh), BF16),
        scratch_shapes=[pltpu.VMEM((S, dh), F32)],
        compiler_params=_params("parallel", "parallel", "parallel"),
        name="nsa_compress",
    )(qkv, pos, w1, w2)


def _softmax_parts_t(s):
    m = jnp.max(s, axis=0, keepdims=True)
    p = jnp.exp2(s - m)
    return p, jnp.sum(p, axis=0, keepdims=True)


def _topk_membership_t(imp_t, n_top):
    n_blk, tq = imp_t.shape
    groups = n_blk // SUBLANES
    va = [imp_t[a * SUBLANES:(a + 1) * SUBLANES] for a in range(groups)]
    cnt = [jnp.zeros((SUBLANES, tq), F32) for _ in range(groups)]
    sub = lax.broadcasted_iota(jnp.int32, (SUBLANES, tq), 0)
    for i in range(n_blk):
        vi = jnp.broadcast_to(imp_t[i:i + 1, :], (SUBLANES, tq))
        for a in range(groups):
            if i < a * SUBLANES:
                ahead = jnp.where(vi >= va[a], 1.0, 0.0)
            elif i >= (a + 1) * SUBLANES:
                ahead = jnp.where(vi > va[a], 1.0, 0.0)
            else:
                ahead = jnp.where(sub > i - a * SUBLANES, jnp.where(vi >= va[a], 1.0, 0.0),
                                  jnp.where(vi > va[a], 1.0, 0.0))
            cnt[a] = cnt[a] + ahead
    return jnp.concatenate([jnp.where(c < n_top, 1.0, 0.0) for c in cnt], axis=0)


def _nsa_attn_kernel(q_ref, kc_ref, vc_ref, ks_ref, vs_ref, kw_ref, vw_ref, gate_ref, ovt_ref, y_ref,
                     selt_ref, m_ref, l_ref, acc_ref, *, tq, tk, n_slc, n_top):
    R, dh = C_REP, C_HEAD_DIM
    t0 = pl.program_id(2) * tq
    q = q_ref[0].reshape(R * tq, dh)
    pos = t0 + lax.broadcasted_iota(jnp.int32, (1, tq), 1)
    gates_t = gate_ref[...].T

    def per_head(x):
        return jnp.concatenate([x] * R, axis=1)

    n_cmp_rows = kc_ref.shape[3]
    cmp_end = lax.broadcasted_iota(jnp.int32, (n_cmp_rows, tq), 0) * CMP_STRIDE + (CMP_LEN - 1)
    ok_c = jnp.logical_and(cmp_end <= pos, cmp_end < n_cmp_rows * CMP_STRIDE)
    p_c, l_c = _softmax_parts_t(_dot_nt(kc_ref[0, 0, 0], q) + per_head(jnp.where(ok_c, 0.0, NEG_INF)))
    has_c = per_head(pos >= CMP_LEN - 1)
    p_c = p_c * jnp.where(has_c, 1.0 / l_c, 0.0)
    oc_t = _dot_tn(vc_ref[0, 0, 0], p_c.astype(BF16))

    slab = WINDOW + tq
    w0 = pl.multiple_of(jnp.maximum(t0 - WINDOW, 0), tq)
    kpos_w = w0 + lax.broadcasted_iota(jnp.int32, (slab, tq), 0)
    ok_w = jnp.logical_and(kpos_w <= pos, pos - kpos_w < WINDOW)
    p_w, l_w = _softmax_parts_t(_dot_nt(kw_ref[0, 0, pl.ds(w0, slab), :], q) + per_head(jnp.where(ok_w, 0.0, NEG_INF)))
    ow_t = _dot_tn(vw_ref[0, 0, pl.ds(w0, slab), :], p_w.astype(BF16)) * (1.0 / l_w)

    p_sum = p_c[:, 0:tq]
    for r in range(1, R):
        p_sum = p_sum + p_c[:, r * tq:(r + 1) * tq]
    imp_t = _dot_f32(ovt_ref[...], p_sum)
    blk_t = lax.broadcasted_iota(jnp.int32, (n_slc, tq), 0)
    cur_t = jnp.right_shift(pos, SEL_SHIFT)
    forced = jnp.logical_or(blk_t == 0, jnp.logical_or(blk_t == cur_t, blk_t == cur_t - 1))
    imp_t = jnp.where(forced, BIG, jnp.where(blk_t > cur_t, -BIG, imp_t))
    selt_ref[...] = _topk_membership_t(imp_t, n_top)

    m_ref[...] = jnp.full(m_ref.shape, NEG_INF, F32)
    l_ref[...] = jnp.zeros(l_ref.shape, F32)
    acc_ref[...] = jnp.zeros(acc_ref.shape, F32)
    blocks_per_tile = tk // SEL_LEN

    def scores(kt):
        k0 = pl.multiple_of(kt * tk, tk)
        sel_rows = selt_ref[pl.ds(pl.multiple_of(kt * blocks_per_tile, blocks_per_tile), blocks_per_tile), :]
        member = jnp.concatenate([jnp.broadcast_to(sel_rows[j:j + 1, :], (SEL_LEN, tq))
                                  for j in range(blocks_per_tile)], axis=0)
        kpos = k0 + lax.broadcasted_iota(jnp.int32, (tk, tq), 0)
        bias = jnp.where(jnp.logical_and(member > 0.5, kpos <= pos), 0.0, NEG_INF)
        return _dot_nt(ks_ref[0, 0, pl.ds(k0, tk), :], q) + per_head(bias)

    def sweep(it, carry):
        tiles = (2 * it, 2 * it + 1)
        s_tiles = [scores(kt) for kt in tiles]
        m, l, acc = m_ref[...], l_ref[...], acc_ref[...]
        for kt, s in zip(tiles, s_tiles):
            m_new = jnp.maximum(m, jnp.max(s, axis=0, keepdims=True))
            p = jnp.exp2(s - m_new)
            alpha = jnp.exp2(m - m_new)
            l = alpha * l + jnp.sum(p, axis=0, keepdims=True)
            vblk = vs_ref[0, 0, pl.ds(pl.multiple_of(kt * tk, tk), tk), :]
            acc = alpha * acc + _dot_tn(vblk, p.astype(BF16))
            m = m_new
        m_ref[...], l_ref[...], acc_ref[...] = m, l, acc
        return carry

    n_tiles = (t0 + tq - 1) // tk + 1
    lax.fori_loop(0, (n_tiles + 1) // 2, sweep, 0)
    os_t = acc_ref[...] * (1.0 / l_ref[...])

    for r in range(R):
        cols = slice(r * tq, (r + 1) * tq)
        merged_t = (gates_t[r:r + 1, :] * oc_t[:, cols] + gates_t[R + r:R + r + 1, :] * os_t[:, cols]
                    + gates_t[2 * R + r:2 * R + r + 1, :] * ow_t[:, cols])
        y_ref[:, r * dh:(r + 1) * dh] = merged_t.T.astype(BF16)


def _nsa_attn(qkv, cmp_kv, gates, overlap_t, *, tq=128, tk=512):
    B_, _, S, dh = qkv.shape
    nq = S // tq
    n_slc = S // SEL_LEN
    n_top = min(SEL_TOPK, n_slc)
    assert n_top >= 3 and n_slc % SUBLANES == 0
    tk = min(tk, S)
    assert (tk // SEL_LEN) % SUBLANES == 0 and tq == LANES
    assert (S // tk) % 2 == 0
    n_cmp_rows = cmp_kv.shape[3]
    R = C_REP
    seq = lambda base: pl.BlockSpec((1, 1, S, dh), lambda b, g, i: (b, base + g, 0, 0))
    return pl.pallas_call(
        functools.partial(_nsa_attn_kernel, tq=tq, tk=tk, n_slc=n_slc, n_top=n_top),
        grid=(B_, C_KV, nq),
        in_specs=[
            pl.BlockSpec((1, R, tq, dh), lambda b, g, i: (b, g, i, 0)),
            pl.BlockSpec((1, 1, 1, n_cmp_rows, dh), lambda b, g, i: (b, 0, g, 0, 0)),
            pl.BlockSpec((1, 1, 1, n_cmp_rows, dh), lambda b, g, i: (b, 1, g, 0, 0)),
            seq(HM_KSEL), seq(HM_VSEL), seq(HM_KWIN), seq(HM_VWIN),
            pl.BlockSpec((tq, LANES), lambda b, g, i: (b * nq + i, g)),
            pl.BlockSpec((n_slc, n_cmp_rows), lambda b, g, i: (0, 0)),
        ],
        out_specs=pl.BlockSpec((tq, R * dh), lambda b, g, i: (b * nq + i, g)),
        out_shape=jax.ShapeDtypeStruct((B_ * S, C_HEADS * dh), BF16),
        scratch_shapes=[
            pltpu.VMEM((n_slc, tq), F32),
            pltpu.VMEM((1, R * tq), F32),
            pltpu.VMEM((1, R * tq), F32),
            pltpu.VMEM((dh, R * tq), F32),
        ],
        compiler_params=_params("parallel", "parallel", "arbitrary"),
        name="nsa_attn",
    )(qkv, cmp_kv, cmp_kv, qkv, qkv, qkv, qkv, gates, overlap_t)


def _rope_tables(S):
    half = ROPE_DIMS // 2
    inv = 1.0 / (ROPE_THETA ** (jnp.arange(half, dtype=F32) / half))
    ang = jnp.arange(S, dtype=F32)[:, None] * inv[None, :]
    cos, sin = jnp.cos(ang), jnp.sin(ang)
    rest = LANES - ROPE_DIMS
    ca = jnp.concatenate([cos, cos, jnp.ones((S, rest), F32)], axis=-1)
    sp = jnp.concatenate([jnp.zeros((S, half), F32), sin, jnp.zeros((S, rest), F32)], axis=-1)
    sm = jnp.concatenate([-sin, jnp.zeros((S, half + rest), F32)], axis=-1)
    return ca, sp, sm


def _overlap_matrix(n_cmp_rows, n_slc):
    ci = jnp.arange(n_cmp_rows)[:, None] * CMP_STRIDE
    sj = jnp.arange(n_slc)[None, :] * SEL_LEN
    ov = jnp.clip(jnp.minimum(ci + CMP_LEN, sj + SEL_LEN) - jnp.maximum(ci, sj), 0).astype(F32) / CMP_LEN
    return ov


def _gmlp_mlstm_layer(h, mix_g, w_in, gate_b, g_norm, g_ws, g_bs, conv_w, m_norm, w_out, batch):
    w_main = w_in.astype(BF16)
    w_gate = jnp.pad(w_in[:, AB_MAIN:], ((0, 0), (0, LANES - 2 * B_HEADS))).astype(BF16)
    p, gates = _ab_proj(h, mix_g.reshape(1, D_MODEL), w_main, w_gate)
    y_a = _gmlp(p, g_norm.reshape(1, A_WIDTH), g_ws, g_bs.T)
    gb = jnp.pad(gate_b.reshape(1, 2 * B_HEADS), ((0, 0), (0, LANES - 2 * B_HEADS)))
    y_b = _mlstm(p, gates, gb, conv_w, m_norm.reshape(1, B_WIDTH), batch)
    return _outproj(h, [y_a, y_b], w_out.astype(BF16))


def _nsa_layer(h, mix_g, w_in, cmp_pos, cmp_w1, cmp_w2, w_out, batch):
    T = h.shape[0]
    S = T // batch
    R = C_REP
    w_main = w_in.astype(BF16)
    wg = w_in[:, C_MAIN:].reshape(D_MODEL, 3, C_KV, R).transpose(0, 2, 1, 3).reshape(D_MODEL, C_KV, 3 * R)
    wg = jnp.pad(wg, ((0, 0), (0, 0), (0, LANES - 3 * R))).reshape(D_MODEL, C_KV * LANES).astype(BF16)
    ca, sp, sm = _rope_tables(S)
    qkv, gates = _nsa_proj(h, mix_g.reshape(1, D_MODEL), w_main, wg, ca, sp, sm, batch)
    cmp_kv = _compress(qkv, cmp_pos, cmp_w1.reshape(2, CMP_LEN, C_HEAD_DIM, C_HEAD_DIM).astype(BF16),
                       cmp_w2.astype(BF16))
    overlap_t = _overlap_matrix(S // CMP_STRIDE, S // SEL_LEN).T
    y = _nsa_attn(qkv, cmp_kv, gates, overlap_t)
    return _outproj(h, [y], w_out.astype(BF16))


def kernel(x, ffn_norm, ffn_w_gate, ffn_w_up, ffn_w_down, mix_norm, ab_w_in, mlstm_gate_bias, gmlp_norm, gmlp_w_s,
           gmlp_b_s, mlstm_conv, mlstm_norm, ab_w_out, nsa_w_in, nsa_cmp_pos, nsa_cmp_w1, nsa_cmp_w2, nsa_w_out,
           final_norm):
    batch, S, _ = x.shape
    depth = ffn_norm.shape[0]
    h = x.reshape(batch * S, D_MODEL)
    fg = final_norm.reshape(1, D_MODEL)

    wg, wu, wd = ffn_w_gate.astype(BF16), ffn_w_up.astype(BF16), ffn_w_down.astype(BF16)

    def ffn(h, layer, half, final=False):
        return _ffn(h, ffn_norm[layer, half].reshape(1, D_MODEL), wg, wu, wd, fg, layer, half, final=final)

    for layer in range(depth):
        j = layer // 2
        h = ffn(h, layer, 0)
        if layer % 2 == 0:
            h = _gmlp_mlstm_layer(h, mix_norm[layer], ab_w_in[j], mlstm_gate_bias[j], gmlp_norm[j], gmlp_w_s[j],
                                  gmlp_b_s[j], mlstm_conv[j], mlstm_norm[j], ab_w_out[j], batch)
        else:
            h = _nsa_layer(h, mix_norm[layer], nsa_w_in[j], nsa_cmp_pos[j], nsa_cmp_w1[j], nsa_cmp_w2[j],
                           nsa_w_out[j], batch)
        h = ffn(h, layer, 1, final=(layer == depth - 1))
    return h.reshape(batch, S, D_MODEL)
```

```python
import functools
import math

import jax
import jax.numpy as jnp
from jax import lax
from jax.experimental import pallas as pl
from jax.experimental.pallas import tpu as pltpu

F32 = jnp.float32
BF16 = jnp.bfloat16

D_MODEL = 2048
D_FF = 5632
EPS = 1e-6

A_WIDTH = D_MODEL // 2
A_CHUNK = 128
A_GROUPS = 8
A_GROUP_DIM = A_WIDTH // A_GROUPS

B_HEADS = 4
B_WIDTH = D_MODEL // 2
B_HEAD_DIM = B_WIDTH // B_HEADS
B_CHUNK = 128
B_CONV = 4
AB_MAIN = 2 * A_WIDTH + 4 * B_WIDTH

C_HEADS = 16
C_KV = 4
C_REP = C_HEADS // C_KV
C_HEAD_DIM = D_MODEL // C_HEADS
CMP_LEN = 32
CMP_STRIDE = 16
SEL_LEN = 64
SEL_SHIFT = SEL_LEN.bit_length() - 1
SEL_TOPK = 16
WINDOW = 512
C_MAIN = C_HEADS * C_HEAD_DIM + 6 * C_KV * C_HEAD_DIM
ROPE_THETA = 500000.0
ROPE_DIMS = C_HEAD_DIM // 4
NEG_INF = -1e30
BIG = 1e9
LOG2E = math.log2(math.e)

LANES = 128
SUBLANES = 8
VMEM_LIMIT = 56 * 1024 * 1024

HM_HEADS = C_HEADS + 6 * C_KV
HM_KCMP, HM_VCMP, HM_KSEL, HM_VSEL, HM_KWIN, HM_VWIN = (C_HEADS + i * C_KV for i in range(6))


def _params(*sem):
    return pltpu.CompilerParams(dimension_semantics=sem, vmem_limit_bytes=VMEM_LIMIT)


def _rms(x, g):
    return x * lax.rsqrt(jnp.mean(x * x, axis=-1, keepdims=True) + EPS) * g


def _dot(a, b):
    return jnp.dot(a, b, preferred_element_type=F32)


def _dot_nt(a, b):
    return lax.dot_general(a, b, (((1,), (1,)), ((), ())), preferred_element_type=F32)


def _dot_tn(a, b):
    return lax.dot_general(a, b, (((0,), (0,)), ((), ())), preferred_element_type=F32)


def _dot_f32(a, b):
    return jnp.dot(a, b, preferred_element_type=F32, precision=lax.Precision.HIGHEST)


def _ffn_kernel(h_ref, g_ref, wg_ref, wu_ref, wd_ref, fg_ref, o_ref, n_ref, *, final):
    j = pl.program_id(1)

    @pl.when(j == 0)
    def _():
        x = h_ref[...]
        n_ref[...] = _rms(x, g_ref[...]).astype(BF16)
        o_ref[...] = x

    n = n_ref[...]
    a = _dot(n, wg_ref[...])
    b = _dot(n, wu_ref[...])
    t = (a * jax.nn.sigmoid(a) * (0.5 * b)).astype(BF16)
    o_ref[...] += _dot(t, wd_ref[...])

    if final:
        @pl.when(j == pl.num_programs(1) - 1)
        def _():
            o_ref[...] = _rms(o_ref[...], fg_ref[...])


def _ffn(h, g, wg, wu, wd, fg, layer, half, *, final, tm=1024, tf=512):
    T = h.shape[0]
    return pl.pallas_call(
        functools.partial(_ffn_kernel, final=final),
        grid=(T // tm, D_FF // tf),
        in_specs=[
            pl.BlockSpec((tm, D_MODEL), lambda i, j: (i, 0)),
            pl.BlockSpec((1, D_MODEL), lambda i, j: (0, 0)),
            pl.BlockSpec((None, None, D_MODEL, tf), lambda i, j: (layer, half, 0, j)),
            pl.BlockSpec((None, None, D_MODEL, tf), lambda i, j: (layer, half, 0, j)),
            pl.BlockSpec((None, None, tf, D_MODEL), lambda i, j: (layer, half, j, 0)),
            pl.BlockSpec((1, D_MODEL), lambda i, j: (0, 0)),
        ],
        out_specs=pl.BlockSpec((tm, D_MODEL), lambda i, j: (i, 0)),
        out_shape=jax.ShapeDtypeStruct((T, D_MODEL), F32),
        scratch_shapes=[pltpu.VMEM((tm, D_MODEL), BF16)],
        compiler_params=_params("parallel", "arbitrary"),
        name="ffn_final" if final else "ffn",
    )(h, g, wg, wu, wd, fg)


def _ab_proj_kernel(h_ref, g_ref, w_ref, wgate_ref, o_ref, og_ref, n_ref):
    j = pl.program_id(1)

    @pl.when(j == 0)
    def _():
        n = _rms(h_ref[...], g_ref[...]).astype(BF16)
        n_ref[...] = n
        og_ref[...] = _dot(n, wgate_ref[...])

    o_ref[...] = _dot(n_ref[...], w_ref[...])


def _ab_proj(h, g, w, wgate, *, tm=1024, tn=1024):
    T = h.shape[0]
    return pl.pallas_call(
        _ab_proj_kernel,
        grid=(T // tm, AB_MAIN // tn),
        in_specs=[
            pl.BlockSpec((tm, D_MODEL), lambda i, j: (i, 0)),
            pl.BlockSpec((1, D_MODEL), lambda i, j: (0, 0)),
            pl.BlockSpec((D_MODEL, tn), lambda i, j: (0, j)),
            pl.BlockSpec((D_MODEL, LANES), lambda i, j: (0, 0)),
        ],
        out_specs=[
            pl.BlockSpec((tm, tn), lambda i, j: (i, j)),
            pl.BlockSpec((tm, LANES), lambda i, j: (i, 0)),
        ],
        out_shape=[
            jax.ShapeDtypeStruct((T, AB_MAIN), F32),
            jax.ShapeDtypeStruct((T, LANES), F32),
        ],
        scratch_shapes=[pltpu.VMEM((tm, D_MODEL), BF16)],
        compiler_params=_params("parallel", "arbitrary"),
        name="ab_proj",
    )(h, g, w, wgate)


def _gmlp_kernel(u_ref, v_ref, g_ref, ws_ref, bs_ref, o_ref):
    zu = jax.nn.gelu(u_ref[...])
    vn = _rms(jax.nn.gelu(v_ref[...]), g_ref[...]).astype(BF16)
    row = lax.broadcasted_iota(jnp.int32, (A_CHUNK, A_CHUNK), 0)
    col = lax.broadcasted_iota(jnp.int32, (A_CHUNK, A_CHUNK), 1)
    causal = col <= row
    bs = bs_ref[...]
    for gi in range(A_GROUPS):
        sl = slice(gi * A_GROUP_DIM, (gi + 1) * A_GROUP_DIM)
        w = jnp.where(causal, ws_ref[gi], 0.0).astype(BF16)
        sv = _dot(w, vn[:, sl]) + bs[:, gi:gi + 1]
        o_ref[:, sl] = (zu[:, sl] * sv).astype(BF16)


def _gmlp(p, g, ws, bs_t):
    T = p.shape[0]
    return pl.pallas_call(
        _gmlp_kernel,
        grid=(T // A_CHUNK,),
        in_specs=[
            pl.BlockSpec((A_CHUNK, A_WIDTH), lambda c: (c, 0)),
            pl.BlockSpec((A_CHUNK, A_WIDTH), lambda c: (c, 1)),
            pl.BlockSpec((1, A_WIDTH), lambda c: (0, 0)),
            pl.BlockSpec((A_GROUPS, A_CHUNK, A_CHUNK), lambda c: (0, 0, 0)),
            pl.BlockSpec((A_CHUNK, A_GROUPS), lambda c: (0, 0)),
        ],
        out_specs=pl.BlockSpec((A_CHUNK, A_WIDTH), lambda c: (c, 0)),
        out_shape=jax.ShapeDtypeStruct((T, A_WIDTH), BF16),
        compiler_params=_params("parallel"),
        name="gmlp",
    )(p, p, g, ws, bs_t)


def _log_sigmoid(x):
    return jnp.minimum(x, 0.0) - jnp.log1p(jnp.exp(-jnp.abs(x)))


def _mlstm_kernel(qk_ref, v_ref, o_ref, gate_ref, gb_ref, cw_ref, mn_ref, y_ref,
                  xx_ref, c_ref, n_ref, m_ref):
    L, d = B_CHUNK, B_HEAD_DIM
    c = pl.program_id(1)
    tail = 8

    @pl.when(c == 0)
    def _():
        xx_ref[0:tail, :] = jnp.zeros((tail, 2 * B_WIDTH), F32)
        c_ref[...] = jnp.zeros_like(c_ref)
        n_ref[...] = jnp.zeros_like(n_ref)
        m_ref[...] = jnp.zeros_like(m_ref)

    xx_ref[tail:tail + L, :] = qk_ref[...]
    cw = cw_ref[...]
    conv = jnp.zeros((L, 2 * B_WIDTH), F32)
    for kk in range(B_CONV):
        off = tail - (B_CONV - 1) + kk
        conv = conv + cw[kk:kk + 1, :] * xx_ref[off:off + L, :]
    carry_rows = xx_ref[L:L + tail, :]
    xx_ref[0:tail, :] = carry_rows
    qk = conv * jax.nn.sigmoid(conv)

    gcol = gate_ref[...] + gb_ref[...]
    grow = gcol.T
    r_i = lax.broadcasted_iota(jnp.int32, (L, L), 0)
    c_i = lax.broadcasted_iota(jnp.int32, (L, L), 1)
    causal = c_i <= r_i
    tril = causal.astype(F32)
    triu = (r_i <= c_i).astype(F32)
    bcol_all = _dot_f32(tril, _log_sigmoid(gcol))
    brow_all = _dot_f32(_log_sigmoid(grow), triu)

    for hh in range(B_HEADS):
        q = qk[:, hh * d:(hh + 1) * d]
        k = qk[:, B_WIDTH + hh * d:B_WIDTH + (hh + 1) * d] * (d ** -0.5)
        v = v_ref[:, hh * d:(hh + 1) * d]
        qb, kb, vb = q.astype(BF16), k.astype(BF16), v.astype(BF16)
        i_col = gcol[:, hh:hh + 1]
        i_row = grow[hh:hh + 1, :]
        b_col = bcol_all[:, B_HEADS + hh:B_HEADS + hh + 1]
        b_row = brow_all[B_HEADS + hh:B_HEADS + hh + 1, :]
        b_last = b_col[L - 1:L, :]
        m_old = m_ref[hh:hh + 1, 0:1]

        logd = jnp.where(causal, b_col - b_row + i_row, NEG_INF)
        inter = b_col + m_old
        m_t = jnp.maximum(jnp.max(logd, axis=-1, keepdims=True), inter)
        dm = jnp.where(causal, jnp.exp(logd - m_t), 0.0)
        w_inter = jnp.exp(inter - m_t)
        s = _dot_nt(qb, kb) * dm
        ct = c_ref[hh]
        nvec = n_ref[hh:hh + 1, :]
        num = _dot(s.astype(BF16), vb) + w_inter * _dot(qb, ct.astype(BF16))
        den = jnp.sum(s, axis=-1, keepdims=True) + w_inter * jnp.sum(q * nvec, axis=-1, keepdims=True)
        hout = num / jnp.maximum(jnp.abs(den), jnp.exp(-m_t))

        m_new = m_t[L - 1:L, :]
        w_s = jnp.exp(b_last - b_col + i_col - m_new)
        w_prev = jnp.exp(b_last + m_old - m_new)
        c_ref[hh] = w_prev * ct + _dot_tn(kb, (w_s * v).astype(BF16))
        n_ref[hh:hh + 1, :] = w_prev * nvec + jnp.sum(w_s * k, axis=0, keepdims=True)
        m_ref[hh:hh + 1, :] = jnp.broadcast_to(m_new, (1, LANES))

        hn = _rms(hout, mn_ref[:, hh * d:(hh + 1) * d])
        og = jax.nn.sigmoid(o_ref[:, hh * d:(hh + 1) * d])
        y_ref[:, hh * d:(hh + 1) * d] = (og * hn).astype(BF16)


def _mlstm(p, gates, gate_b, conv_w, m_norm, batch):
    T = p.shape[0]
    L = B_CHUNK
    nc = T // batch // L
    qk_blk = (2 * A_WIDTH) // (2 * B_WIDTH)
    v_blk = (2 * A_WIDTH + 2 * B_WIDTH) // B_WIDTH
    row = lambda b, c: b * nc + c
    return pl.pallas_call(
        _mlstm_kernel,
        grid=(batch, nc),
        in_specs=[
            pl.BlockSpec((L, 2 * B_WIDTH), lambda b, c: (row(b, c), qk_blk)),
            pl.BlockSpec((L, B_WIDTH), lambda b, c: (row(b, c), v_blk)),
            pl.BlockSpec((L, B_WIDTH), lambda b, c: (row(b, c), v_blk + 1)),
            pl.BlockSpec((L, LANES), lambda b, c: (row(b, c), 0)),
            pl.BlockSpec((1, LANES), lambda b, c: (0, 0)),
            pl.BlockSpec((B_CONV, 2 * B_WIDTH), lambda b, c: (0, 0)),
            pl.BlockSpec((1, B_WIDTH), lambda b, c: (0, 0)),
        ],
        out_specs=pl.BlockSpec((L, B_WIDTH), lambda b, c: (row(b, c), 0)),
        out_shape=jax.ShapeDtypeStruct((T, B_WIDTH), BF16),
        scratch_shapes=[
            pltpu.VMEM((L + 8, 2 * B_WIDTH), F32),
            pltpu.VMEM((B_HEADS, B_HEAD_DIM, B_HEAD_DIM), F32),
            pltpu.VMEM((8, B_HEAD_DIM), F32),
            pltpu.VMEM((8, LANES), F32),
        ],
        compiler_params=_params("arbitrary", "arbitrary"),
        name="mlstm",
    )(p, p, p, gates, gate_b, conv_w, m_norm)


def _outproj_kernel(*refs, n_lhs):
    h_ref = refs[0]
    o_ref = refs[1 + 2 * n_lhs]
    acc = h_ref[...]
    for i in range(n_lhs):
        acc = acc + _dot(refs[1 + i][...], refs[1 + n_lhs + i][...])
    o_ref[...] = acc


def _outproj(h, lhs, w, *, tm=512):
    T = h.shape[0]
    n = len(lhs)
    width = lhs[0].shape[1]
    assert all(a.shape[1] == width for a in lhs) and w.shape[0] == n * width
    in_specs = [pl.BlockSpec((tm, D_MODEL), lambda i: (i, 0))]
    in_specs += [pl.BlockSpec((tm, width), lambda i: (i, 0)) for _ in lhs]
    in_specs += [pl.BlockSpec((width, D_MODEL), functools.partial(lambda i, k: (k, 0), k=k)) for k in range(n)]
    return pl.pallas_call(
        functools.partial(_outproj_kernel, n_lhs=n),
        grid=(T // tm,),
        in_specs=in_specs,
        out_specs=pl.BlockSpec((tm, D_MODEL), lambda i: (i, 0)),
        out_shape=jax.ShapeDtypeStruct((T, D_MODEL), F32),
        compiler_params=_params("parallel"),
        name="outproj",
    )(h, *lhs, *([w] * n))


def _rope(x, ca, sp, sm):
    return x * ca + pltpu.roll(x, ROPE_DIMS // 2, 1) * sp + pltpu.roll(x, LANES - ROPE_DIMS // 2, 1) * sm


def _nsa_proj_kernel(h_ref, g_ref, w_ref, wgate_ref, ca_ref, sp_ref, sm_ref, o_ref, og_ref, n_ref):
    j = pl.program_id(1)

    @pl.when(j == 0)
    def _():
        n = _rms(h_ref[...], g_ref[...]).astype(BF16)
        n_ref[...] = n
        og_ref[...] = jax.nn.sigmoid(_dot(n, wgate_ref[...]))

    res = _dot(n_ref[...], w_ref[...])
    n_q = C_HEADS // C_KV
    is_q = j < n_q
    is_rope = jnp.logical_or(is_q, jnp.logical_and(j >= n_q, (j - n_q) % 2 == 0))

    @pl.when(is_rope)
    def _():
        scale = jnp.where(is_q, C_HEAD_DIM ** -0.5 * LOG2E, 1.0).astype(F32)
        ca, sp, sm = ca_ref[...], sp_ref[...], sm_ref[...]
        for gi in range(C_KV):
            x = res[:, gi * LANES:(gi + 1) * LANES]
            o_ref[0, gi] = (_rope(x, ca, sp, sm) * scale).astype(BF16)

    @pl.when(jnp.logical_not(is_rope))
    def _():
        for gi in range(C_KV):
            o_ref[0, gi] = res[:, gi * LANES:(gi + 1) * LANES].astype(BF16)


def _nsa_proj(h, g, w, wgate, ca, sp, sm, batch, *, tm=1024):
    T = h.shape[0]
    S = T // batch
    nts = S // tm
    tn = C_KV * C_HEAD_DIM
    return pl.pallas_call(
        _nsa_proj_kernel,
        grid=(T // tm, C_MAIN // tn),
        in_specs=[
            pl.BlockSpec((tm, D_MODEL), lambda i, j: (i, 0)),
            pl.BlockSpec((1, D_MODEL), lambda i, j: (0, 0)),
            pl.BlockSpec((D_MODEL, tn), lambda i, j: (0, j)),
            pl.BlockSpec((D_MODEL, C_KV * LANES), lambda i, j: (0, 0)),
            pl.BlockSpec((tm, LANES), lambda i, j: (i % nts, 0)),
            pl.BlockSpec((tm, LANES), lambda i, j: (i % nts, 0)),
            pl.BlockSpec((tm, LANES), lambda i, j: (i % nts, 0)),
        ],
        out_specs=[
            pl.BlockSpec((1, C_KV, tm, C_HEAD_DIM), lambda i, j: (i // nts, j, i % nts, 0)),
            pl.BlockSpec((tm, C_KV * LANES), lambda i, j: (i, 0)),
        ],
        out_shape=[
            jax.ShapeDtypeStruct((batch, HM_HEADS, S, C_HEAD_DIM), BF16),
            jax.ShapeDtypeStruct((T, C_KV * LANES), F32),
        ],
        scratch_shapes=[pltpu.VMEM((tm, D_MODEL), BF16)],
        compiler_params=_params("parallel", "arbitrary"),
        name="nsa_proj",
    )(h, g, w, wgate, ca, sp, sm)


def _compress_kernel(x_ref, pos_ref, w1_ref, w2_ref, o_ref, xs_ref):
    xs_ref[...] = x_ref[0, 0].astype(F32)
    n_rows = xs_ref.shape[0] // CMP_STRIDE
    dh = xs_ref.shape[1]
    ya = jnp.zeros((n_rows, dh), F32)
    yb = jnp.zeros((n_rows, dh), F32)
    for l in range(CMP_STRIDE):
        xl = xs_ref[pl.ds(l, n_rows, stride=CMP_STRIDE), :]
        ya = ya + _dot((xl + pos_ref[0, l:l + 1, :]).astype(BF16), w1_ref[0, l])
        lb = CMP_STRIDE + l
        yb = yb + _dot((xl + pos_ref[0, lb:lb + 1, :]).astype(BF16), w1_ref[0, lb])
    pre = ya + pltpu.roll(yb, n_rows - 1, 0)
    out = _dot(jax.nn.gelu(pre).astype(BF16), w2_ref[0])
    rows = lax.broadcasted_iota(jnp.int32, out.shape, 0)
    o_ref[0, 0, 0] = jnp.where(rows < n_rows - 1, out, 0.0).astype(BF16)


def _compress(qkv, pos, w1, w2):
    B_, _, S, dh = qkv.shape
    rows = S // CMP_STRIDE
    return pl.pallas_call(
        _compress_kernel,
        grid=(B_, 2, C_KV),
        in_specs=[
            pl.BlockSpec((1, 1, S, dh), lambda b, kv, g: (b, HM_KCMP + kv * C_KV + g, 0, 0)),
            pl.BlockSpec((1, CMP_LEN, dh), lambda b, kv, g: (kv, 0, 0)),
            pl.BlockSpec((1, CMP_LEN, dh, dh), lambda b, kv, g: (kv, 0, 0, 0)),
            pl.BlockSpec((1, dh, dh), lambda b, kv, g: (kv, 0, 0)),
        ],
        out_specs=pl.BlockSpec((1, 1, 1, rows, dh), lambda b, kv, g: (b, kv, g, 0, 0)),
        out_shape=jax.ShapeDtypeStruct((B_, 2, C_KV, rows, dh), BF16),
        scratch_shapes=[pltpu.VMEM((S, dh), F32)],
        compiler_params=_params("parallel", "parallel", "parallel"),
        name="nsa_compress",
    )(qkv, pos, w1, w2)


def _softmax_parts_t(s):
    m = jnp.max(s, axis=0, keepdims=True)
    p = jnp.exp2(s - m)
    return p, jnp.sum(p, axis=0, keepdims=True)


def _topk_membership_t(imp_t, n_top):
    n_blk, tq = imp_t.shape
    groups = n_blk // SUBLANES
    va = [imp_t[a * SUBLANES:(a + 1) * SUBLANES] for a in range(groups)]
    cnt = [jnp.zeros((SUBLANES, tq), F32) for _ in range(groups)]
    sub = lax.broadcasted_iota(jnp.int32, (SUBLANES, tq), 0)
    for i in range(n_blk):
        vi = jnp.broadcast_to(imp_t[i:i + 1, :], (SUBLANES, tq))
        for a in range(groups):
            if i < a * SUBLANES:
                ahead = jnp.where(vi >= va[a], 1.0, 0.0)
            elif i >= (a + 1) * SUBLANES:
                ahead = jnp.where(vi > va[a], 1.0, 0.0)
            else:
                ahead = jnp.where(sub > i - a * SUBLANES, jnp.where(vi >= va[a], 1.0, 0.0),
                                  jnp.where(vi > va[a], 1.0, 0.0))
            cnt[a] = cnt[a] + ahead
    return jnp.concatenate([jnp.where(c < n_top, 1.0, 0.0) for c in cnt], axis=0)


def _nsa_attn_kernel(q_ref, kc_ref, vc_ref, ks_ref, vs_ref, kw_ref, vw_ref, gate_ref, ovt_ref, y_ref,
                     selt_ref, m_ref, l_ref, acc_ref, oc_ref, ow_ref, *, tq, tk, n_slc, n_top):
    G, R, dh = C_KV, C_REP, C_HEAD_DIM
    t0 = pl.program_id(1) * tq
    pos = t0 + lax.broadcasted_iota(jnp.int32, (1, tq), 1)

    def q_of(g):
        return q_ref[0, g * R:(g + 1) * R].reshape(R * tq, dh)

    def per_head(x):
        return jnp.concatenate([x] * R, axis=1)

    n_cmp_rows = kc_ref.shape[3]
    cmp_end = lax.broadcasted_iota(jnp.int32, (n_cmp_rows, tq), 0) * CMP_STRIDE + (CMP_LEN - 1)
    ok_c = jnp.logical_and(cmp_end <= pos, cmp_end < n_cmp_rows * CMP_STRIDE)
    bias_c = per_head(jnp.where(ok_c, 0.0, NEG_INF))
    has_c = per_head(pos >= CMP_LEN - 1)
    slab = WINDOW + tq
    w0 = pl.multiple_of(jnp.maximum(t0 - WINDOW, 0), tq)
    kpos_w = w0 + lax.broadcasted_iota(jnp.int32, (slab, tq), 0)
    ok_w = jnp.logical_and(kpos_w <= pos, pos - kpos_w < WINDOW)
    bias_w = per_head(jnp.where(ok_w, 0.0, NEG_INF))
    blk_t = lax.broadcasted_iota(jnp.int32, (n_slc, tq), 0)
    cur_t = jnp.right_shift(pos, SEL_SHIFT)
    forced = jnp.logical_or(blk_t == 0, jnp.logical_or(blk_t == cur_t, blk_t == cur_t - 1))
    future = blk_t > cur_t

    s_cmp = [_dot_nt(kc_ref[0, 0, g], q_of(g)) + bias_c for g in range(G)]
    s_win = [_dot_nt(kw_ref[0, g, pl.ds(w0, slab), :], q_of(g)) + bias_w for g in range(G)]
    for g in range(G):
        p_c, l_c = _softmax_parts_t(s_cmp[g])
        p_c = p_c * jnp.where(has_c, 1.0 / l_c, 0.0)
        oc_ref[g] = _dot_tn(vc_ref[0, 0, g], p_c.astype(BF16))

        p_w, l_w = _softmax_parts_t(s_win[g])
        ow_ref[g] = _dot_tn(vw_ref[0, g, pl.ds(w0, slab), :], p_w.astype(BF16)) * (1.0 / l_w)

        p_sum = p_c[:, 0:tq]
        for r in range(1, R):
            p_sum = p_sum + p_c[:, r * tq:(r + 1) * tq]
        imp_t = jnp.where(forced, BIG, jnp.where(future, -BIG, _dot_f32(ovt_ref[...], p_sum)))
        selt_ref[g] = _topk_membership_t(imp_t, n_top)

    m_ref[...] = jnp.full(m_ref.shape, NEG_INF, F32)
    l_ref[...] = jnp.zeros(l_ref.shape, F32)
    acc_ref[...] = jnp.zeros(acc_ref.shape, F32)
    blocks_per_tile = tk // SEL_LEN

    def sweep(kt, carry):
        k0 = pl.multiple_of(kt * tk, tk)
        causal = k0 + lax.broadcasted_iota(jnp.int32, (tk, tq), 0) <= pos
        first_blk = pl.multiple_of(kt * blocks_per_tile, blocks_per_tile)
        s_all = []
        for g in range(G):
            sel_rows = selt_ref[g, pl.ds(first_blk, blocks_per_tile), :]
            member = jnp.concatenate([jnp.broadcast_to(sel_rows[j:j + 1, :], (SEL_LEN, tq))
                                      for j in range(blocks_per_tile)], axis=0)
            bias = jnp.where(jnp.logical_and(member > 0.5, causal), 0.0, NEG_INF)
            s_all.append(_dot_nt(ks_ref[0, g, pl.ds(k0, tk), :], q_of(g)) + per_head(bias))
        for g, s in enumerate(s_all):
            m_old = m_ref[g]
            m_new = jnp.maximum(m_old, jnp.max(s, axis=0, keepdims=True))
            p = jnp.exp2(s - m_new)
            alpha = jnp.exp2(m_old - m_new)
            l_ref[g] = alpha * l_ref[g] + jnp.sum(p, axis=0, keepdims=True)
            acc_ref[g] = alpha * acc_ref[g] + _dot_tn(vs_ref[0, g, pl.ds(k0, tk), :], p.astype(BF16))
            m_ref[g] = m_new
        return carry

    lax.fori_loop(0, (t0 + tq - 1) // tk + 1, sweep, 0)

    for g in range(G):
        gates_t = gate_ref[:, g * LANES:(g + 1) * LANES].T
        os_t = acc_ref[g] * (1.0 / l_ref[g])
        oc_t, ow_t = oc_ref[g], ow_ref[g]
        for r in range(R):
            cols = slice(r * tq, (r + 1) * tq)
            merged_t = (gates_t[r:r + 1, :] * oc_t[:, cols] + gates_t[R + r:R + r + 1, :] * os_t[:, cols]
                        + gates_t[2 * R + r:2 * R + r + 1, :] * ow_t[:, cols])
            y_ref[:, (g * R + r) * dh:(g * R + r + 1) * dh] = merged_t.T.astype(BF16)


def _nsa_attn(qkv, cmp_kv, gates, overlap_t, *, tq=128, tk=512):
    B_, _, S, dh = qkv.shape
    nq = S // tq
    n_slc = S // SEL_LEN
    n_top = min(SEL_TOPK, n_slc)
    assert n_top >= 3 and n_slc % SUBLANES == 0
    tk = min(tk, S)
    assert (tk // SEL_LEN) % SUBLANES == 0 and tq % LANES == 0
    n_cmp_rows = cmp_kv.shape[3]
    G, R = C_KV, C_REP
    seq = lambda base: pl.BlockSpec((1, G, S, dh), lambda b, i: (b, base // G, 0, 0), pipeline_mode=pl.Buffered(1))
    return pl.pallas_call(
        functools.partial(_nsa_attn_kernel, tq=tq, tk=tk, n_slc=n_slc, n_top=n_top),
        grid=(B_, nq),
        in_specs=[
            pl.BlockSpec((1, C_HEADS, tq, dh), lambda b, i: (b, 0, i, 0)),
            pl.BlockSpec((1, 1, G, n_cmp_rows, dh), lambda b, i: (b, 0, 0, 0, 0)),
            pl.BlockSpec((1, 1, G, n_cmp_rows, dh), lambda b, i: (b, 1, 0, 0, 0)),
            seq(HM_KSEL), seq(HM_VSEL), seq(HM_KWIN), seq(HM_VWIN),
            pl.BlockSpec((tq, G * LANES), lambda b, i: (b * nq + i, 0)),
            pl.BlockSpec((n_slc, n_cmp_rows), lambda b, i: (0, 0)),
        ],
        out_specs=pl.BlockSpec((tq, C_HEADS * dh), lambda b, i: (b * nq + i, 0)),
        out_shape=jax.ShapeDtypeStruct((B_ * S, C_HEADS * dh), BF16),
        scratch_shapes=[
            pltpu.VMEM((G, n_slc, tq), F32),
            pltpu.VMEM((G, 1, R * tq), F32),
            pltpu.VMEM((G, 1, R * tq), F32),
            pltpu.VMEM((G, dh, R * tq), F32),
            pltpu.VMEM((G, dh, R * tq), F32),
            pltpu.VMEM((G, dh, R * tq), F32),
        ],
        compiler_params=_params("parallel", "arbitrary"),
        name="nsa_attn",
    )(qkv, cmp_kv, cmp_kv, qkv, qkv, qkv, qkv, gates, overlap_t)


def _rope_tables(S):
    half = ROPE_DIMS // 2
    inv = 1.0 / (ROPE_THETA ** (jnp.arange(half, dtype=F32) / half))
    ang = jnp.arange(S, dtype=F32)[:, None] * inv[None, :]
    cos, sin = jnp.cos(ang), jnp.sin(ang)
    rest = LANES - ROPE_DIMS
    ca = jnp.concatenate([cos, cos, jnp.ones((S, rest), F32)], axis=-1)
    sp = jnp.concatenate([jnp.zeros((S, half), F32), sin, jnp.zeros((S, rest), F32)], axis=-1)
    sm = jnp.concatenate([-sin, jnp.zeros((S, half + rest), F32)], axis=-1)
    return ca, sp, sm


def _overlap_matrix(n_cmp_rows, n_slc):
    ci = jnp.arange(n_cmp_rows)[:, None] * CMP_STRIDE
    sj = jnp.arange(n_slc)[None, :] * SEL_LEN
    ov = jnp.clip(jnp.minimum(ci + CMP_LEN, sj + SEL_LEN) - jnp.maximum(ci, sj), 0).astype(F32) / CMP_LEN
    return ov


def _gmlp_mlstm_layer(h, mix_g, w_in, gate_b, g_norm, g_ws, g_bs, conv_w, m_norm, w_out, batch):
    w_main = w_in.astype(BF16)
    w_gate = jnp.pad(w_in[:, AB_MAIN:], ((0, 0), (0, LANES - 2 * B_HEADS))).astype(BF16)
    p, gates = _ab_proj(h, mix_g.reshape(1, D_MODEL), w_main, w_gate)
    y_a = _gmlp(p, g_norm.reshape(1, A_WIDTH), g_ws, g_bs.T)
    gb = jnp.pad(gate_b.reshape(1, 2 * B_HEADS), ((0, 0), (0, LANES - 2 * B_HEADS)))
    y_b = _mlstm(p, gates, gb, conv_w, m_norm.reshape(1, B_WIDTH), batch)
    return _outproj(h, [y_a, y_b], w_out.astype(BF16))


def _nsa_layer(h, mix_g, w_in, cmp_pos, cmp_w1, cmp_w2, w_out, batch):
    T = h.shape[0]
    S = T // batch
    R = C_REP
    w_main = w_in.astype(BF16)
    wg = w_in[:, C_MAIN:].reshape(D_MODEL, 3, C_KV, R).transpose(0, 2, 1, 3).reshape(D_MODEL, C_KV, 3 * R)
    wg = jnp.pad(wg, ((0, 0), (0, 0), (0, LANES - 3 * R))).reshape(D_MODEL, C_KV * LANES).astype(BF16)
    ca, sp, sm = _rope_tables(S)
    qkv, gates = _nsa_proj(h, mix_g.reshape(1, D_MODEL), w_main, wg, ca, sp, sm, batch)
    cmp_kv = _compress(qkv, cmp_pos, cmp_w1.reshape(2, CMP_LEN, C_HEAD_DIM, C_HEAD_DIM).astype(BF16),
                       cmp_w2.astype(BF16))
    overlap_t = _overlap_matrix(S // CMP_STRIDE, S // SEL_LEN).T
    y = _nsa_attn(qkv, cmp_kv, gates, overlap_t)
    return _outproj(h, [y], w_out.astype(BF16))


def kernel(x, ffn_norm, ffn_w_gate, ffn_w_up, ffn_w_down, mix_norm, ab_w_in, mlstm_gate_bias, gmlp_norm, gmlp_w_s,
           gmlp_b_s, mlstm_conv, mlstm_norm, ab_w_out, nsa_w_in, nsa_cmp_pos, nsa_cmp_w1, nsa_cmp_w2, nsa_w_out,
           final_norm):
    batch, S, _ = x.shape
    depth = ffn_norm.shape[0]
    h = x.reshape(batch * S, D_MODEL)
    fg = final_norm.reshape(1, D_MODEL)

    wg, wu, wd = ffn_w_gate.astype(BF16), ffn_w_up.astype(BF16), ffn_w_down.astype(BF16)

    def ffn(h, layer, half, final=False):
        return _ffn(h, ffn_norm[layer, half].reshape(1, D_MODEL), wg, wu, wd, fg, layer, half, final=final)

    for layer in range(depth):
        j = layer // 2
        h = ffn(h, layer, 0)
        if layer % 2 == 0:
            h = _gmlp_mlstm_layer(h, mix_norm[layer], ab_w_in[j], mlstm_gate_bias[j], gmlp_norm[j], gmlp_w_s[j],
                                  gmlp_b_s[j], mlstm_conv[j], mlstm_norm[j], ab_w_out[j], batch)
        else:
            h = _nsa_layer(h, mix_norm[layer], nsa_w_in[j], nsa_cmp_pos[j], nsa_cmp_w1[j], nsa_cmp_w2[j],
                           nsa_w_out[j], batch)
        h = ffn(h, layer, 1, final=(layer == depth - 1))
    return h.reshape(batch, S, D_MODEL)
```

```python
import functools
import math

import jax
import jax.numpy as jnp
from jax import lax
from jax.experimental import pallas as pl
from jax.experimental.pallas import tpu as pltpu

F32 = jnp.float32
BF16 = jnp.bfloat16

D_MODEL = 2048
D_FF = 5632
EPS = 1e-6

A_WIDTH = D_MODEL // 2
A_CHUNK = 128
A_GROUPS = 8
A_GROUP_DIM = A_WIDTH // A_GROUPS

B_HEADS = 4
B_WIDTH = D_MODEL // 2
B_HEAD_DIM = B_WIDTH // B_HEADS
B_CHUNK = 128
B_CONV = 4
AB_MAIN = 2 * A_WIDTH + 4 * B_WIDTH

C_HEADS = 16
C_KV = 4
C_REP = C_HEADS // C_KV
C_HEAD_DIM = D_MODEL // C_HEADS
CMP_LEN = 32
CMP_STRIDE = 16
SEL_LEN = 64
SEL_SHIFT = SEL_LEN.bit_length() - 1
SEL_TOPK = 16
WINDOW = 512
C_MAIN = C_HEADS * C_HEAD_DIM + 6 * C_KV * C_HEAD_DIM
ROPE_THETA = 500000.0
ROPE_DIMS = C_HEAD_DIM // 4
NEG_INF = -1e30
BIG = 1e9
LOG2E = math.log2(math.e)

LANES = 128
SUBLANES = 8
VMEM_LIMIT = 56 * 1024 * 1024

HM_HEADS = C_HEADS + 6 * C_KV
HM_KCMP, HM_VCMP, HM_KSEL, HM_VSEL, HM_KWIN, HM_VWIN = (C_HEADS + i * C_KV for i in range(6))


def _params(*sem):
    return pltpu.CompilerParams(dimension_semantics=sem, vmem_limit_bytes=VMEM_LIMIT)


def _rms(x, g):
    return x * lax.rsqrt(jnp.mean(x * x, axis=-1, keepdims=True) + EPS) * g


def _dot(a, b):
    return jnp.dot(a, b, preferred_element_type=F32)


def _dot_nt(a, b):
    return lax.dot_general(a, b, (((1,), (1,)), ((), ())), preferred_element_type=F32)


def _dot_tn(a, b):
    return lax.dot_general(a, b, (((0,), (0,)), ((), ())), preferred_element_type=F32)


def _dot_f32(a, b):
    return jnp.dot(a, b, preferred_element_type=F32, precision=lax.Precision.HIGHEST)


def _ffn_kernel(h_ref, g_ref, wg_ref, wu_ref, wd_ref, fg_ref, o_ref, n_ref, *, final):
    j = pl.program_id(1)

    @pl.when(j == 0)
    def _():
        x = h_ref[...]
        n_ref[...] = _rms(x, g_ref[...]).astype(BF16)
        o_ref[...] = x

    n = n_ref[...]
    a = _dot(n, wg_ref[...])
    b = _dot(n, wu_ref[...])
    t = (a * jax.nn.sigmoid(a) * (0.5 * b)).astype(BF16)
    o_ref[...] += _dot(t, wd_ref[...])

    if final:
        @pl.when(j == pl.num_programs(1) - 1)
        def _():
            o_ref[...] = _rms(o_ref[...], fg_ref[...])


def _ffn(h, g, wg, wu, wd, fg, layer, half, *, final, tm=1024, tf=512):
    T = h.shape[0]
    return pl.pallas_call(
        functools.partial(_ffn_kernel, final=final),
        grid=(T // tm, D_FF // tf),
        in_specs=[
            pl.BlockSpec((tm, D_MODEL), lambda i, j: (i, 0)),
            pl.BlockSpec((1, D_MODEL), lambda i, j: (0, 0)),
            pl.BlockSpec((None, None, D_MODEL, tf), lambda i, j: (layer, half, 0, j)),
            pl.BlockSpec((None, None, D_MODEL, tf), lambda i, j: (layer, half, 0, j)),
            pl.BlockSpec((None, None, tf, D_MODEL), lambda i, j: (layer, half, j, 0)),
            pl.BlockSpec((1, D_MODEL), lambda i, j: (0, 0)),
        ],
        out_specs=pl.BlockSpec((tm, D_MODEL), lambda i, j: (i, 0)),
        out_shape=jax.ShapeDtypeStruct((T, D_MODEL), F32),
        scratch_shapes=[pltpu.VMEM((tm, D_MODEL), BF16)],
        compiler_params=_params("parallel", "arbitrary"),
        name="ffn_final" if final else "ffn",
    )(h, g, wg, wu, wd, fg)


def _ab_proj_kernel(h_ref, g_ref, w_ref, wgate_ref, o_ref, og_ref, n_ref):
    j = pl.program_id(1)

    @pl.when(j == 0)
    def _():
        n = _rms(h_ref[...], g_ref[...]).astype(BF16)
        n_ref[...] = n
        og_ref[...] = _dot(n, wgate_ref[...])

    o_ref[...] = _dot(n_ref[...], w_ref[...])


def _ab_proj(h, g, w, wgate, *, tm=1024, tn=1024):
    T = h.shape[0]
    return pl.pallas_call(
        _ab_proj_kernel,
        grid=(T // tm, AB_MAIN // tn),
        in_specs=[
            pl.BlockSpec((tm, D_MODEL), lambda i, j: (i, 0)),
            pl.BlockSpec((1, D_MODEL), lambda i, j: (0, 0)),
            pl.BlockSpec((D_MODEL, tn), lambda i, j: (0, j)),
            pl.BlockSpec((D_MODEL, LANES), lambda i, j: (0, 0)),
        ],
        out_specs=[
            pl.BlockSpec((tm, tn), lambda i, j: (i, j)),
            pl.BlockSpec((tm, LANES), lambda i, j: (i, 0)),
        ],
        out_shape=[
            jax.ShapeDtypeStruct((T, AB_MAIN), F32),
            jax.ShapeDtypeStruct((T, LANES), F32),
        ],
        scratch_shapes=[pltpu.VMEM((tm, D_MODEL), BF16)],
        compiler_params=_params("parallel", "arbitrary"),
        name="ab_proj",
    )(h, g, w, wgate)


def _gmlp_kernel(u_ref, v_ref, g_ref, ws_ref, bs_ref, o_ref):
    zu = jax.nn.gelu(u_ref[...])
    vn = _rms(jax.nn.gelu(v_ref[...]), g_ref[...]).astype(BF16)
    row = lax.broadcasted_iota(jnp.int32, (A_CHUNK, A_CHUNK), 0)
    col = lax.broadcasted_iota(jnp.int32, (A_CHUNK, A_CHUNK), 1)
    causal = col <= row
    bs = bs_ref[...]
    for gi in range(A_GROUPS):
        sl = slice(gi * A_GROUP_DIM, (gi + 1) * A_GROUP_DIM)
        w = jnp.where(causal, ws_ref[gi], 0.0).astype(BF16)
        sv = _dot(w, vn[:, sl]) + bs[:, gi:gi + 1]
        o_ref[:, sl] = (zu[:, sl] * sv).astype(BF16)


def _gmlp(p, g, ws, bs_t):
    T = p.shape[0]
    return pl.pallas_call(
        _gmlp_kernel,
        grid=(T // A_CHUNK,),
        in_specs=[
            pl.BlockSpec((A_CHUNK, A_WIDTH), lambda c: (c, 0)),
            pl.BlockSpec((A_CHUNK, A_WIDTH), lambda c: (c, 1)),
            pl.BlockSpec((1, A_WIDTH), lambda c: (0, 0)),
            pl.BlockSpec((A_GROUPS, A_CHUNK, A_CHUNK), lambda c: (0, 0, 0)),
            pl.BlockSpec((A_CHUNK, A_GROUPS), lambda c: (0, 0)),
        ],
        out_specs=pl.BlockSpec((A_CHUNK, A_WIDTH), lambda c: (c, 0)),
        out_shape=jax.ShapeDtypeStruct((T, A_WIDTH), BF16),
        compiler_params=_params("parallel"),
        name="gmlp",
    )(p, p, g, ws, bs_t)


def _log_sigmoid(x):
    return jnp.minimum(x, 0.0) - jnp.log1p(jnp.exp(-jnp.abs(x)))


def _mlstm_kernel(qk_ref, v_ref, o_ref, gate_ref, gb_ref, cw_ref, mn_ref, y_ref,
                  xx_ref, c_ref, n_ref, m_ref):
    L, d = B_CHUNK, B_HEAD_DIM
    c = pl.program_id(1)
    tail = 8

    @pl.when(c == 0)
    def _():
        xx_ref[0:tail, :] = jnp.zeros((tail, 2 * B_WIDTH), F32)
        c_ref[...] = jnp.zeros_like(c_ref)
        n_ref[...] = jnp.zeros_like(n_ref)
        m_ref[...] = jnp.zeros_like(m_ref)

    xx_ref[tail:tail + L, :] = qk_ref[...]
    cw = cw_ref[...]
    conv = jnp.zeros((L, 2 * B_WIDTH), F32)
    for kk in range(B_CONV):
        off = tail - (B_CONV - 1) + kk
        conv = conv + cw[kk:kk + 1, :] * xx_ref[off:off + L, :]
    carry_rows = xx_ref[L:L + tail, :]
    xx_ref[0:tail, :] = carry_rows
    qk = conv * jax.nn.sigmoid(conv)

    gcol = gate_ref[...] + gb_ref[...]
    grow = gcol.T
    r_i = lax.broadcasted_iota(jnp.int32, (L, L), 0)
    c_i = lax.broadcasted_iota(jnp.int32, (L, L), 1)
    causal = c_i <= r_i
    tril = causal.astype(F32)
    triu = (r_i <= c_i).astype(F32)
    bcol_all = _dot_f32(tril, _log_sigmoid(gcol))
    brow_all = _dot_f32(_log_sigmoid(grow), triu)

    for hh in range(B_HEADS):
        q = qk[:, hh * d:(hh + 1) * d]
        k = qk[:, B_WIDTH + hh * d:B_WIDTH + (hh + 1) * d] * (d ** -0.5)
        v = v_ref[:, hh * d:(hh + 1) * d]
        qb, kb, vb = q.astype(BF16), k.astype(BF16), v.astype(BF16)
        i_col = gcol[:, hh:hh + 1]
        i_row = grow[hh:hh + 1, :]
        b_col = bcol_all[:, B_HEADS + hh:B_HEADS + hh + 1]
        b_row = brow_all[B_HEADS + hh:B_HEADS + hh + 1, :]
        b_last = b_col[L - 1:L, :]
        m_old = m_ref[hh:hh + 1, 0:1]

        logd = jnp.where(causal, b_col - b_row + i_row, NEG_INF)
        inter = b_col + m_old
        m_t = jnp.maximum(jnp.max(logd, axis=-1, keepdims=True), inter)
        dm = jnp.where(causal, jnp.exp(logd - m_t), 0.0)
        w_inter = jnp.exp(inter - m_t)
        s = _dot_nt(qb, kb) * dm
        ct = c_ref[hh]
        nvec = n_ref[hh:hh + 1, :]
        num = _dot(s.astype(BF16), vb) + w_inter * _dot(qb, ct.astype(BF16))
        den = jnp.sum(s, axis=-1, keepdims=True) + w_inter * jnp.sum(q * nvec, axis=-1, keepdims=True)
        hout = num / jnp.maximum(jnp.abs(den), jnp.exp(-m_t))

        m_new = m_t[L - 1:L, :]
        w_s = jnp.exp(b_last - b_col + i_col - m_new)
        w_prev = jnp.exp(b_last + m_old - m_new)
        c_ref[hh] = w_prev * ct + _dot_tn(kb, (w_s * v).astype(BF16))
        n_ref[hh:hh + 1, :] = w_prev * nvec + jnp.sum(w_s * k, axis=0, keepdims=True)
        m_ref[hh:hh + 1, :] = jnp.broadcast_to(m_new, (1, LANES))

        hn = _rms(hout, mn_ref[:, hh * d:(hh + 1) * d])
        og = jax.nn.sigmoid(o_ref[:, hh * d:(hh + 1) * d])
        y_ref[:, hh * d:(hh + 1) * d] = (og * hn).astype(BF16)


def _mlstm(p, gates, gate_b, conv_w, m_norm, batch):
    T = p.shape[0]
    L = B_CHUNK
    nc = T // batch // L
    qk_blk = (2 * A_WIDTH) // (2 * B_WIDTH)
    v_blk = (2 * A_WIDTH + 2 * B_WIDTH) // B_WIDTH
    row = lambda b, c: b * nc + c
    return pl.pallas_call(
        _mlstm_kernel,
        grid=(batch, nc),
        in_specs=[
            pl.BlockSpec((L, 2 * B_WIDTH), lambda b, c: (row(b, c), qk_blk)),
            pl.BlockSpec((L, B_WIDTH), lambda b, c: (row(b, c), v_blk)),
            pl.BlockSpec((L, B_WIDTH), lambda b, c: (row(b, c), v_blk + 1)),
            pl.BlockSpec((L, LANES), lambda b, c: (row(b, c), 0)),
            pl.BlockSpec((1, LANES), lambda b, c: (0, 0)),
            pl.BlockSpec((B_CONV, 2 * B_WIDTH), lambda b, c: (0, 0)),
            pl.BlockSpec((1, B_WIDTH), lambda b, c: (0, 0)),
        ],
        out_specs=pl.BlockSpec((L, B_WIDTH), lambda b, c: (row(b, c), 0)),
        out_shape=jax.ShapeDtypeStruct((T, B_WIDTH), BF16),
        scratch_shapes=[
            pltpu.VMEM((L + 8, 2 * B_WIDTH), F32),
            pltpu.VMEM((B_HEADS, B_HEAD_DIM, B_HEAD_DIM), F32),
            pltpu.VMEM((8, B_HEAD_DIM), F32),
            pltpu.VMEM((8, LANES), F32),
        ],
        compiler_params=_params("arbitrary", "arbitrary"),
        name="mlstm",
    )(p, p, p, gates, gate_b, conv_w, m_norm)


def _outproj_kernel(*refs, n_lhs):
    h_ref = refs[0]
    o_ref = refs[1 + 2 * n_lhs]
    acc = h_ref[...]
    for i in range(n_lhs):
        acc = acc + _dot(refs[1 + i][...], refs[1 + n_lhs + i][...])
    o_ref[...] = acc


def _outproj(h, lhs, w, *, tm=512):
    T = h.shape[0]
    n = len(lhs)
    width = lhs[0].shape[1]
    assert all(a.shape[1] == width for a in lhs) and w.shape[0] == n * width
    in_specs = [pl.BlockSpec((tm, D_MODEL), lambda i: (i, 0))]
    in_specs += [pl.BlockSpec((tm, width), lambda i: (i, 0)) for _ in lhs]
    in_specs += [pl.BlockSpec((width, D_MODEL), functools.partial(lambda i, k: (k, 0), k=k)) for k in range(n)]
    return pl.pallas_call(
        functools.partial(_outproj_kernel, n_lhs=n),
        grid=(T // tm,),
        in_specs=in_specs,
        out_specs=pl.BlockSpec((tm, D_MODEL), lambda i: (i, 0)),
        out_shape=jax.ShapeDtypeStruct((T, D_MODEL), F32),
        compiler_params=_params("parallel"),
        name="outproj",
    )(h, *lhs, *([w] * n))


def _rope(x, ca, sn):
    half = ROPE_DIMS // 2
    lane = lax.broadcasted_iota(jnp.int32, x.shape, 1)
    partner = jnp.where(lane < half, lane + half, jnp.where(lane < ROPE_DIMS, lane - half, lane))
    return x * ca + jnp.take_along_axis(x, partner, axis=1) * sn


def _nsa_proj_kernel(h_ref, g_ref, w_ref, wgate_ref, ca_ref, sn_ref, o_ref, og_ref, n_ref):
    j = pl.program_id(1)

    @pl.when(j == 0)
    def _():
        n = _rms(h_ref[...], g_ref[...]).astype(BF16)
        n_ref[...] = n
        og_ref[...] = jax.nn.sigmoid(_dot(n, wgate_ref[...]))

    res = _dot(n_ref[...], w_ref[...])
    n_q = C_HEADS // C_KV
    is_value = jnp.logical_and(j >= n_q, (j - n_q) % 2 == 1)

    @pl.when(jnp.logical_not(is_value))
    def _():
        ca, sn = ca_ref[0], sn_ref[0]
        for gi in range(C_KV):
            o_ref[0, gi] = _rope(res[:, gi * LANES:(gi + 1) * LANES], ca, sn).astype(BF16)

    @pl.when(is_value)
    def _():
        for gi in range(C_KV):
            o_ref[0, gi] = res[:, gi * LANES:(gi + 1) * LANES].astype(BF16)


def _nsa_proj(h, g, w, wgate, ca, sn, batch, *, tm=1024):
    T = h.shape[0]
    S = T // batch
    nts = S // tm
    tn = C_KV * C_HEAD_DIM
    n_q = C_HEADS // C_KV
    table = pl.BlockSpec((1, tm, LANES), lambda i, j: (jnp.where(j < n_q, 0, 1), i % nts, 0))
    return pl.pallas_call(
        _nsa_proj_kernel,
        grid=(T // tm, C_MAIN // tn),
        in_specs=[
            pl.BlockSpec((tm, D_MODEL), lambda i, j: (i, 0)),
            pl.BlockSpec((1, D_MODEL), lambda i, j: (0, 0)),
            pl.BlockSpec((D_MODEL, tn), lambda i, j: (0, j)),
            pl.BlockSpec((D_MODEL, LANES), lambda i, j: (0, 0)),
            table, table,
        ],
        out_specs=[
            pl.BlockSpec((1, C_KV, tm, C_HEAD_DIM), lambda i, j: (i // nts, j, i % nts, 0)),
            pl.BlockSpec((tm, LANES), lambda i, j: (i, 0)),
        ],
        out_shape=[
            jax.ShapeDtypeStruct((batch, HM_HEADS, S, C_HEAD_DIM), BF16),
            jax.ShapeDtypeStruct((T, LANES), F32),
        ],
        scratch_shapes=[pltpu.VMEM((tm, D_MODEL), BF16)],
        compiler_params=_params("parallel", "arbitrary"),
        name="nsa_proj",
    )(h, g, w, wgate, ca, sn)


def _compress_kernel(x_ref, pos_ref, w1_ref, w2_ref, o_ref, xs_ref):
    xs_ref[...] = x_ref[0, 0].astype(F32)
    n_rows = xs_ref.shape[0] // CMP_STRIDE
    dh = xs_ref.shape[1]
    ya = jnp.zeros((n_rows, dh), F32)
    yb = jnp.zeros((n_rows, dh), F32)
    for l in range(CMP_STRIDE):
        xl = xs_ref[pl.ds(l, n_rows, stride=CMP_STRIDE), :]
        ya = ya + _dot((xl + pos_ref[0, l:l + 1, :]).astype(BF16), w1_ref[0, l])
        lb = CMP_STRIDE + l
        yb = yb + _dot((xl + pos_ref[0, lb:lb + 1, :]).astype(BF16), w1_ref[0, lb])
    pre = ya + pltpu.roll(yb, n_rows - 1, 0)
    out = _dot(jax.nn.gelu(pre).astype(BF16), w2_ref[0])
    rows = lax.broadcasted_iota(jnp.int32, out.shape, 0)
    o_ref[0, 0, 0] = jnp.where(rows < n_rows - 1, out, 0.0).astype(BF16)


def _compress(qkv, pos, w1, w2):
    B_, _, S, dh = qkv.shape
    rows = S // CMP_STRIDE
    return pl.pallas_call(
        _compress_kernel,
        grid=(B_, 2, C_KV),
        in_specs=[
            pl.BlockSpec((1, 1, S, dh), lambda b, kv, g: (b, HM_KCMP + kv * C_KV + g, 0, 0)),
            pl.BlockSpec((1, CMP_LEN, dh), lambda b, kv, g: (kv, 0, 0)),
            pl.BlockSpec((1, CMP_LEN, dh, dh), lambda b, kv, g: (kv, 0, 0, 0)),
            pl.BlockSpec((1, dh, dh), lambda b, kv, g: (kv, 0, 0)),
        ],
        out_specs=pl.BlockSpec((1, 1, 1, rows, dh), lambda b, kv, g: (b, kv, g, 0, 0)),
        out_shape=jax.ShapeDtypeStruct((B_, 2, C_KV, rows, dh), BF16),
        scratch_shapes=[pltpu.VMEM((S, dh), F32)],
        compiler_params=_params("parallel", "parallel", "parallel"),
        name="nsa_compress",
    )(qkv, pos, w1, w2)


def _softmax_parts_t(s):
    m = jnp.max(s, axis=0, keepdims=True)
    p = jnp.exp2(s - m)
    return p, jnp.sum(p, axis=0, keepdims=True)


def _topk_membership_t(imp_t, n_top):
    n_blk, tq = imp_t.shape
    groups = n_blk // SUBLANES
    va = [imp_t[a * SUBLANES:(a + 1) * SUBLANES] for a in range(groups)]
    cnt = [jnp.zeros((SUBLANES, tq), F32) for _ in range(groups)]
    sub = lax.broadcasted_iota(jnp.int32, (SUBLANES, tq), 0)
    for i in range(n_blk):
        vi = jnp.broadcast_to(imp_t[i:i + 1, :], (SUBLANES, tq))
        for a in range(groups):
            if i < a * SUBLANES:
                ahead = jnp.where(vi >= va[a], 1.0, 0.0)
            elif i >= (a + 1) * SUBLANES:
                ahead = jnp.where(vi > va[a], 1.0, 0.0)
            else:
                ahead = jnp.where(sub > i - a * SUBLANES, jnp.where(vi >= va[a], 1.0, 0.0),
                                  jnp.where(vi > va[a], 1.0, 0.0))
            cnt[a] = cnt[a] + ahead
    return jnp.concatenate([jnp.where(c < n_top, 1.0, 0.0) for c in cnt], axis=0)


def _nsa_attn_kernel(q_ref, kc_ref, vc_ref, ks_ref, vs_ref, kw_ref, vw_ref, gate_ref, ovt_ref, y_ref,
                     qx_ref, m_ref, l_ref, acc_ref, oc_ref, ow_ref, *, tq, tk, n_slc, n_top):
    G, R, dh = C_KV, C_REP, C_HEAD_DIM
    t0 = pl.program_id(1) * tq
    pos = t0 + lax.broadcasted_iota(jnp.int32, (1, tq), 1)

    def q_of(g):
        return q_ref[0, g * R:(g + 1) * R].reshape(R * tq, dh)

    def per_head(x):
        return jnp.concatenate([x] * R, axis=1)

    n_cmp_rows = kc_ref.shape[3]
    cmp_end = lax.broadcasted_iota(jnp.int32, (n_cmp_rows, tq), 0) * CMP_STRIDE + (CMP_LEN - 1)
    ok_c = jnp.logical_and(cmp_end <= pos, cmp_end < n_cmp_rows * CMP_STRIDE)
    bias_c = per_head(jnp.where(ok_c, 0.0, NEG_INF))
    has_c = per_head(pos >= CMP_LEN - 1)
    slab = WINDOW + tq
    w0 = pl.multiple_of(jnp.maximum(t0 - WINDOW, 0), tq)
    kpos_w = w0 + lax.broadcasted_iota(jnp.int32, (slab, tq), 0)
    ok_w = jnp.logical_and(kpos_w <= pos, pos - kpos_w < WINDOW)
    bias_w = per_head(jnp.where(ok_w, 0.0, NEG_INF))
    blk_t = lax.broadcasted_iota(jnp.int32, (n_slc, tq), 0)
    cur_t = jnp.right_shift(pos, SEL_SHIFT)
    forced = jnp.logical_or(blk_t == 0, jnp.logical_or(blk_t == cur_t, blk_t == cur_t - 1))
    future = blk_t > cur_t

    s_cmp = [_dot_nt(kc_ref[0, 0, g], q_of(g)) + bias_c for g in range(G)]
    s_win = [_dot_nt(kw_ref[0, g, pl.ds(w0, slab), :], q_of(g)) + bias_w for g in range(G)]
    for g in range(G):
        p_c, l_c = _softmax_parts_t(s_cmp[g])
        p_c = p_c * jnp.where(has_c, 1.0 / l_c, 0.0)
        oc_ref[g] = _dot_tn(vc_ref[0, 0, g], p_c.astype(BF16))

        p_w, l_w = _softmax_parts_t(s_win[g])
        ow_ref[g] = _dot_tn(vw_ref[0, g, pl.ds(w0, slab), :], p_w.astype(BF16)) * (1.0 / l_w)

        p_sum = p_c[:, 0:tq]
        for r in range(1, R):
            p_sum = p_sum + p_c[:, r * tq:(r + 1) * tq]
        imp_t = jnp.where(forced, BIG, jnp.where(future, -BIG, _dot_f32(ovt_ref[...], p_sum)))
        sel_t = _topk_membership_t(imp_t, n_top)
        sel = jnp.concatenate([sel_t, jnp.zeros((LANES - n_slc, tq), F32)], axis=0).T
        penalty = jnp.where(sel > 0.5, 0.0, NEG_INF).astype(BF16)
        qx_ref[g] = jnp.concatenate([q_of(g), jnp.concatenate([penalty] * R, axis=0)], axis=1)

    m_ref[...] = jnp.full(m_ref.shape, NEG_INF, F32)
    l_ref[...] = jnp.zeros(l_ref.shape, F32)
    acc_ref[...] = jnp.zeros(acc_ref.shape, F32)
    blocks_per_tile = tk // SEL_LEN

    def sweep_tile(kt, diagonal):
        k0 = pl.multiple_of(kt * tk, tk)
        key_blk = kt * blocks_per_tile + jnp.right_shift(lax.broadcasted_iota(jnp.int32, (tk, LANES), 0), SEL_SHIFT)
        onehot = jnp.where(lax.broadcasted_iota(jnp.int32, (tk, LANES), 1) == key_blk, 1.0, 0.0).astype(BF16)
        s_all = []
        for g in range(G):
            k_ext = jnp.concatenate([ks_ref[0, g, pl.ds(k0, tk), :], onehot], axis=1)
            s = _dot_nt(k_ext, qx_ref[g])
            if diagonal:
                causal = k0 + lax.broadcasted_iota(jnp.int32, (tk, tq), 0) <= pos
                s = s + per_head(jnp.where(causal, 0.0, NEG_INF))
            s_all.append(s)
        for g, s in enumerate(s_all):
            m_old = m_ref[g]
            m_new = jnp.maximum(m_old, jnp.max(s, axis=0, keepdims=True))
            p = jnp.exp2(s - m_new)
            alpha = jnp.exp2(m_old - m_new)
            l_ref[g] = alpha * l_ref[g] + jnp.sum(p, axis=0, keepdims=True)
            acc_ref[g] = alpha * acc_ref[g] + _dot_tn(vs_ref[0, g, pl.ds(k0, tk), :], p.astype(BF16))
            m_ref[g] = m_new

    def sweep(kt, carry):
        sweep_tile(kt, diagonal=False)
        return carry

    last_tile = t0 // tk
    lax.fori_loop(0, last_tile, sweep, 0)
    sweep_tile(last_tile, diagonal=True)

    gates_t = gate_ref[...].T
    for g in range(G):
        os_t = acc_ref[g] * (1.0 / l_ref[g])
        oc_t, ow_t = oc_ref[g], ow_ref[g]
        for r in range(R):
            cols = slice(r * tq, (r + 1) * tq)
            hd = g * R + r
            merged_t = (gates_t[hd:hd + 1, :] * oc_t[:, cols]
                        + gates_t[C_HEADS + hd:C_HEADS + hd + 1, :] * os_t[:, cols]
                        + gates_t[2 * C_HEADS + hd:2 * C_HEADS + hd + 1, :] * ow_t[:, cols])
            y_ref[:, (g * R + r) * dh:(g * R + r + 1) * dh] = merged_t.T.astype(BF16)


def _nsa_attn(qkv, cmp_kv, gates, overlap_t, *, tq=128, tk=512):
    B_, _, S, dh = qkv.shape
    nq = S // tq
    n_slc = S // SEL_LEN
    n_top = min(SEL_TOPK, n_slc)
    assert n_top >= 3 and n_slc % SUBLANES == 0
    tk = min(tk, S)
    assert tq % LANES == 0 and tk % tq == 0 and n_slc <= LANES
    n_cmp_rows = cmp_kv.shape[3]
    G, R = C_KV, C_REP
    seq = lambda base: pl.BlockSpec((1, G, S, dh), lambda b, i: (b, base // G, 0, 0), pipeline_mode=pl.Buffered(1))
    return pl.pallas_call(
        functools.partial(_nsa_attn_kernel, tq=tq, tk=tk, n_slc=n_slc, n_top=n_top),
        grid=(B_, nq),
        in_specs=[
            pl.BlockSpec((1, C_HEADS, tq, dh), lambda b, i: (b, 0, i, 0)),
            pl.BlockSpec((1, 1, G, n_cmp_rows, dh), lambda b, i: (b, 0, 0, 0, 0)),
            pl.BlockSpec((1, 1, G, n_cmp_rows, dh), lambda b, i: (b, 1, 0, 0, 0)),
            seq(HM_KSEL), seq(HM_VSEL), seq(HM_KWIN), seq(HM_VWIN),
            pl.BlockSpec((tq, LANES), lambda b, i: (b * nq + i, 0)),
            pl.BlockSpec((n_slc, n_cmp_rows), lambda b, i: (0, 0)),
        ],
        out_specs=pl.BlockSpec((tq, C_HEADS * dh), lambda b, i: (b * nq + i, 0)),
        out_shape=jax.ShapeDtypeStruct((B_ * S, C_HEADS * dh), BF16),
        scratch_shapes=[
            pltpu.VMEM((G, R * tq, dh + LANES), BF16),
            pltpu.VMEM((G, 1, R * tq), F32),
            pltpu.VMEM((G, 1, R * tq), F32),
            pltpu.VMEM((G, dh, R * tq), F32),
            pltpu.VMEM((G, dh, R * tq), F32),
            pltpu.VMEM((G, dh, R * tq), F32),
        ],
        compiler_params=_params("parallel", "arbitrary"),
        name="nsa_attn",
    )(qkv, cmp_kv, cmp_kv, qkv, qkv, qkv, qkv, gates, overlap_t)


def _rope_tables(S):
    half = ROPE_DIMS // 2
    inv = 1.0 / (ROPE_THETA ** (jnp.arange(half, dtype=F32) / half))
    ang = jnp.arange(S, dtype=F32)[:, None] * inv[None, :]
    cos, sin = jnp.cos(ang), jnp.sin(ang)
    rest = LANES - ROPE_DIMS
    ca = jnp.concatenate([cos, cos, jnp.ones((S, rest), F32)], axis=-1)
    sn = jnp.concatenate([-sin, sin, jnp.zeros((S, rest), F32)], axis=-1)
    q_scale = C_HEAD_DIM ** -0.5 * LOG2E
    return jnp.stack([ca * q_scale, ca]), jnp.stack([sn * q_scale, sn])


def _overlap_matrix(n_cmp_rows, n_slc):
    ci = jnp.arange(n_cmp_rows)[:, None] * CMP_STRIDE
    sj = jnp.arange(n_slc)[None, :] * SEL_LEN
    ov = jnp.clip(jnp.minimum(ci + CMP_LEN, sj + SEL_LEN) - jnp.maximum(ci, sj), 0).astype(F32) / CMP_LEN
    return ov


def _gmlp_mlstm_layer(h, mix_g, w_in, gate_b, g_norm, g_ws, g_bs, conv_w, m_norm, w_out, batch):
    w_main = w_in.astype(BF16)
    w_gate = jnp.pad(w_in[:, AB_MAIN:], ((0, 0), (0, LANES - 2 * B_HEADS))).astype(BF16)
    p, gates = _ab_proj(h, mix_g.reshape(1, D_MODEL), w_main, w_gate)
    y_a = _gmlp(p, g_norm.reshape(1, A_WIDTH), g_ws, g_bs.T)
    gb = jnp.pad(gate_b.reshape(1, 2 * B_HEADS), ((0, 0), (0, LANES - 2 * B_HEADS)))
    y_b = _mlstm(p, gates, gb, conv_w, m_norm.reshape(1, B_WIDTH), batch)
    return _outproj(h, [y_a, y_b], w_out.astype(BF16))


def _nsa_layer(h, mix_g, w_in, cmp_pos, cmp_w1, cmp_w2, w_out, batch):
    T = h.shape[0]
    S = T // batch
    w_main = w_in.astype(BF16)
    wg = jnp.pad(w_in[:, C_MAIN:], ((0, 0), (0, LANES - 3 * C_HEADS))).astype(BF16)
    ca, sn = _rope_tables(S)
    qkv, gates = _nsa_proj(h, mix_g.reshape(1, D_MODEL), w_main, wg, ca, sn, batch)
    cmp_kv = _compress(qkv, cmp_pos, cmp_w1.reshape(2, CMP_LEN, C_HEAD_DIM, C_HEAD_DIM).astype(BF16),
                       cmp_w2.astype(BF16))
    overlap_t = _overlap_matrix(S // CMP_STRIDE, S // SEL_LEN).T
    y = _nsa_attn(qkv, cmp_kv, gates, overlap_t)
    return _outproj(h, [y], w_out.astype(BF16))


def kernel(x, ffn_norm, ffn_w_gate, ffn_w_up, ffn_w_down, mix_norm, ab_w_in, mlstm_gate_bias, gmlp_norm, gmlp_w_s,
           gmlp_b_s, mlstm_conv, mlstm_norm, ab_w_out, nsa_w_in, nsa_cmp_pos, nsa_cmp_w1, nsa_cmp_w2, nsa_w_out,
           final_norm):
    batch, S, _ = x.shape
    depth = ffn_norm.shape[0]
    h = x.reshape(batch * S, D_MODEL)
    fg = final_norm.reshape(1, D_MODEL)

    wg, wu, wd = ffn_w_gate.astype(BF16), ffn_w_up.astype(BF16), ffn_w_down.astype(BF16)

    def ffn(h, layer, half, final=False):
        return _ffn(h, ffn_norm[layer, half].reshape(1, D_MODEL), wg, wu, wd, fg, layer, half, final=final)

    for layer in range(depth):
        j = layer // 2
        h = ffn(h, layer, 0)
        if layer % 2 == 0:
            h = _gmlp_mlstm_layer(h, mix_norm[layer], ab_w_in[j], mlstm_gate_bias[j], gmlp_norm[j], gmlp_w_s[j],
                                  gmlp_b_s[j], mlstm_conv[j], mlstm_norm[j], ab_w_out[j], batch)
        else:
            h = _nsa_layer(h, mix_norm[layer], nsa_w_in[j], nsa_cmp_pos[j], nsa_cmp_w1[j], nsa_cmp_w2[j],
                           nsa_w_out[j], batch)
        h = ffn(h, layer, 1, final=(layer == depth - 1))
    return h.reshape(batch, S, D_MODEL)
```

```python
import functools
import math

import jax
import jax.numpy as jnp
from jax import lax
from jax.experimental import pallas as pl
from jax.experimental.pallas import tpu as pltpu

F32 = jnp.float32
BF16 = jnp.bfloat16

D_MODEL = 2048
D_FF = 5632
EPS = 1e-6

A_WIDTH = D_MODEL // 2
A_CHUNK = 128
A_GROUPS = 8
A_GROUP_DIM = A_WIDTH // A_GROUPS

B_HEADS = 4
B_WIDTH = D_MODEL // 2
B_HEAD_DIM = B_WIDTH // B_HEADS
B_CHUNK = 128
B_CONV = 4
AB_MAIN = 2 * A_WIDTH + 4 * B_WIDTH

C_HEADS = 16
C_KV = 4
C_REP = C_HEADS // C_KV
C_HEAD_DIM = D_MODEL // C_HEADS
CMP_LEN = 32
CMP_STRIDE = 16
SEL_LEN = 64
SEL_SHIFT = SEL_LEN.bit_length() - 1
SEL_TOPK = 16
WINDOW = 512
C_MAIN = C_HEADS * C_HEAD_DIM + 6 * C_KV * C_HEAD_DIM
ROPE_THETA = 500000.0
ROPE_DIMS = C_HEAD_DIM // 4
NEG_INF = -1e30
BIG = 1e9
LOG2E = math.log2(math.e)

LANES = 128
SUBLANES = 8
VMEM_LIMIT = 56 * 1024 * 1024
FFN_TF = 512

HM_HEADS = C_HEADS + 6 * C_KV
HM_KCMP, HM_VCMP, HM_KSEL, HM_VSEL, HM_KWIN, HM_VWIN = (C_HEADS + i * C_KV for i in range(6))


def _params(*sem):
    return pltpu.CompilerParams(dimension_semantics=sem, vmem_limit_bytes=VMEM_LIMIT)


def _rms(x, g):
    return x * lax.rsqrt(jnp.mean(x * x, axis=-1, keepdims=True) + EPS) * g


def _dot(a, b):
    return jnp.dot(a, b, preferred_element_type=F32)


def _dot_nt(a, b):
    return lax.dot_general(a, b, (((1,), (1,)), ((), ())), preferred_element_type=F32)


def _dot_tn(a, b):
    return lax.dot_general(a, b, (((0,), (0,)), ((), ())), preferred_element_type=F32)


def _dot_f32(a, b):
    return jnp.dot(a, b, preferred_element_type=F32, precision=lax.Precision.HIGHEST)


def _ffn_kernel(h_ref, g_ref, wgu_ref, wd_ref, fg_ref, o_ref, n_ref, *, final):
    j = pl.program_id(1)

    @pl.when(j == 0)
    def _():
        x = h_ref[...]
        n_ref[...] = _rms(x, g_ref[...]).astype(BF16)
        o_ref[...] = x

    tf = wd_ref.shape[0]
    ab = _dot(n_ref[...], wgu_ref[...])
    a, b = ab[:, :tf], ab[:, tf:]
    t = (a * jax.nn.sigmoid(a) * (0.5 * b)).astype(BF16)
    o_ref[...] += _dot(t, wd_ref[...])

    if final:
        @pl.when(j == pl.num_programs(1) - 1)
        def _():
            o_ref[...] = _rms(o_ref[...], fg_ref[...])


def _ffn_weight_tiles(w_gate, w_up, tf=FFN_TF):
    lead = w_gate.shape[:2]
    tiles = lambda w: w.reshape(*lead, D_MODEL, D_FF // tf, tf)
    return jnp.concatenate([tiles(w_gate), tiles(w_up)], axis=-1).transpose(0, 1, 3, 2, 4).astype(BF16)


def _ffn(h, g, wgu, wd, fg, layer, half, *, final, tm=1024):
    T = h.shape[0]
    n_tiles, tf = wgu.shape[2], wgu.shape[4] // 2
    return pl.pallas_call(
        functools.partial(_ffn_kernel, final=final),
        grid=(T // tm, n_tiles),
        in_specs=[
            pl.BlockSpec((tm, D_MODEL), lambda i, j: (i, 0)),
            pl.BlockSpec((1, D_MODEL), lambda i, j: (0, 0)),
            pl.BlockSpec((None, None, None, D_MODEL, 2 * tf), lambda i, j: (layer, half, j, 0, 0)),
            pl.BlockSpec((None, None, tf, D_MODEL), lambda i, j: (layer, half, j, 0)),
            pl.BlockSpec((1, D_MODEL), lambda i, j: (0, 0)),
        ],
        out_specs=pl.BlockSpec((tm, D_MODEL), lambda i, j: (i, 0)),
        out_shape=jax.ShapeDtypeStruct((T, D_MODEL), F32),
        scratch_shapes=[pltpu.VMEM((tm, D_MODEL), BF16)],
        compiler_params=_params("parallel", "arbitrary"),
        name="ffn_final" if final else "ffn",
    )(h, g, wgu, wd, fg)


def _ab_proj_kernel(h_ref, g_ref, w_ref, wgate_ref, o_ref, og_ref, n_ref):
    j = pl.program_id(1)

    @pl.when(j == 0)
    def _():
        n = _rms(h_ref[...], g_ref[...]).astype(BF16)
        n_ref[...] = n
        og_ref[...] = _dot(n, wgate_ref[...])

    o_ref[...] = _dot(n_ref[...], w_ref[...].astype(BF16))


def _ab_proj(h, g, w, wgate, *, tm=1024, tn=1024):
    T = h.shape[0]
    return pl.pallas_call(
        _ab_proj_kernel,
        grid=(T // tm, AB_MAIN // tn),
        in_specs=[
            pl.BlockSpec((tm, D_MODEL), lambda i, j: (i, 0)),
            pl.BlockSpec((1, D_MODEL), lambda i, j: (0, 0)),
            pl.BlockSpec((D_MODEL, tn), lambda i, j: (0, j)),
            pl.BlockSpec((D_MODEL, LANES), lambda i, j: (0, 0)),
        ],
        out_specs=[
            pl.BlockSpec((tm, tn), lambda i, j: (i, j)),
            pl.BlockSpec((tm, LANES), lambda i, j: (i, 0)),
        ],
        out_shape=[
            jax.ShapeDtypeStruct((T, AB_MAIN), F32),
            jax.ShapeDtypeStruct((T, LANES), F32),
        ],
        scratch_shapes=[pltpu.VMEM((tm, D_MODEL), BF16)],
        compiler_params=_params("parallel", "arbitrary"),
        name="ab_proj",
    )(h, g, w, wgate)


def _gmlp_kernel(u_ref, v_ref, g_ref, ws_ref, bs_ref, o_ref):
    zu = jax.nn.gelu(u_ref[...])
    vn = _rms(jax.nn.gelu(v_ref[...]), g_ref[...]).astype(BF16)
    row = lax.broadcasted_iota(jnp.int32, (A_CHUNK, A_CHUNK), 0)
    col = lax.broadcasted_iota(jnp.int32, (A_CHUNK, A_CHUNK), 1)
    causal = col <= row
    bs = bs_ref[...]
    for gi in range(A_GROUPS):
        sl = slice(gi * A_GROUP_DIM, (gi + 1) * A_GROUP_DIM)
        w = jnp.where(causal, ws_ref[gi], 0.0).astype(BF16)
        for c in range(u_ref.shape[0] // A_CHUNK):
            rows = slice(c * A_CHUNK, (c + 1) * A_CHUNK)
            sv = _dot(w, vn[rows, sl]) + bs[:, gi:gi + 1]
            o_ref[rows, sl] = (zu[rows, sl] * sv).astype(BF16)


def _gmlp(p, g, ws, bs_t, *, chunks_per_step=4):
    T = p.shape[0]
    tm = chunks_per_step * A_CHUNK
    return pl.pallas_call(
        _gmlp_kernel,
        grid=(T // tm,),
        in_specs=[
            pl.BlockSpec((tm, A_WIDTH), lambda c: (c, 0)),
            pl.BlockSpec((tm, A_WIDTH), lambda c: (c, 1)),
            pl.BlockSpec((1, A_WIDTH), lambda c: (0, 0)),
            pl.BlockSpec((A_GROUPS, A_CHUNK, A_CHUNK), lambda c: (0, 0, 0)),
            pl.BlockSpec((A_CHUNK, A_GROUPS), lambda c: (0, 0)),
        ],
        out_specs=pl.BlockSpec((tm, A_WIDTH), lambda c: (c, 0)),
        out_shape=jax.ShapeDtypeStruct((T, A_WIDTH), BF16),
        compiler_params=_params("parallel"),
        name="gmlp",
    )(p, p, g, ws, bs_t)


def _log_sigmoid(x):
    return jnp.minimum(x, 0.0) - jnp.log1p(jnp.exp(-jnp.abs(x)))


def _mlstm_kernel(qk_ref, v_ref, o_ref, gate_ref, gb_ref, cw_ref, mn_ref, y_ref,
                  xx_ref, c_ref, n_ref, m_ref):
    L, d = B_CHUNK, B_HEAD_DIM
    c = pl.program_id(1)
    tail = 8

    @pl.when(c == 0)
    def _():
        xx_ref[0:tail, :] = jnp.zeros((tail, 2 * B_WIDTH), F32)
        c_ref[...] = jnp.zeros_like(c_ref)
        n_ref[...] = jnp.zeros_like(n_ref)
        m_ref[...] = jnp.zeros_like(m_ref)

    xx_ref[tail:tail + L, :] = qk_ref[...]
    cw = cw_ref[...]
    conv = jnp.zeros((L, 2 * B_WIDTH), F32)
    for kk in range(B_CONV):
        off = tail - (B_CONV - 1) + kk
        conv = conv + cw[kk:kk + 1, :] * xx_ref[off:off + L, :]
    carry_rows = xx_ref[L:L + tail, :]
    xx_ref[0:tail, :] = carry_rows
    qk = conv * jax.nn.sigmoid(conv)

    gcol = gate_ref[...] + gb_ref[...]
    grow = gcol.T
    r_i = lax.broadcasted_iota(jnp.int32, (L, L), 0)
    c_i = lax.broadcasted_iota(jnp.int32, (L, L), 1)
    causal = c_i <= r_i
    tril = causal.astype(F32)
    triu = (r_i <= c_i).astype(F32)
    bcol_all = _dot_f32(tril, _log_sigmoid(gcol))
    brow_all = _dot_f32(_log_sigmoid(grow), triu)

    for hh in range(B_HEADS):
        q = qk[:, hh * d:(hh + 1) * d]
        k = qk[:, B_WIDTH + hh * d:B_WIDTH + (hh + 1) * d] * (d ** -0.5)
        v = v_ref[:, hh * d:(hh + 1) * d]
        qb, kb, vb = q.astype(BF16), k.astype(BF16), v.astype(BF16)
        i_col = gcol[:, hh:hh + 1]
        i_row = grow[hh:hh + 1, :]
        b_col = bcol_all[:, B_HEADS + hh:B_HEADS + hh + 1]
        b_row = brow_all[B_HEADS + hh:B_HEADS + hh + 1, :]
        b_last = b_col[L - 1:L, :]
        m_old = m_ref[hh:hh + 1, 0:1]

        logd = jnp.where(causal, b_col - b_row + i_row, NEG_INF)
        inter = b_col + m_old
        m_t = jnp.maximum(jnp.max(logd, axis=-1, keepdims=True), inter)
        dm = jnp.where(causal, jnp.exp(logd - m_t), 0.0)
        w_inter = jnp.exp(inter - m_t)
        s = _dot_nt(qb, kb) * dm
        ct = c_ref[hh]
        nvec = n_ref[hh:hh + 1, :]
        num = _dot(s.astype(BF16), vb) + w_inter * _dot(qb, ct.astype(BF16))
        den = jnp.sum(s, axis=-1, keepdims=True) + w_inter * jnp.sum(q * nvec, axis=-1, keepdims=True)
        hout = num / jnp.maximum(jnp.abs(den), jnp.exp(-m_t))

        m_new = m_t[L - 1:L, :]
        w_s = jnp.exp(b_last - b_col + i_col - m_new)
        w_prev = jnp.exp(b_last + m_old - m_new)
        c_ref[hh] = w_prev * ct + _dot_tn(kb, (w_s * v).astype(BF16))
        n_ref[hh:hh + 1, :] = w_prev * nvec + jnp.sum(w_s * k, axis=0, keepdims=True)
        m_ref[hh:hh + 1, :] = jnp.broadcast_to(m_new, (1, LANES))

        hn = _rms(hout, mn_ref[:, hh * d:(hh + 1) * d])
        og = jax.nn.sigmoid(o_ref[:, hh * d:(hh + 1) * d])
        y_ref[:, hh * d:(hh + 1) * d] = (og * hn).astype(BF16)


def _mlstm(p, gates, gate_b, conv_w, m_norm, batch):
    T = p.shape[0]
    L = B_CHUNK
    nc = T // batch // L
    qk_blk = (2 * A_WIDTH) // (2 * B_WIDTH)
    v_blk = (2 * A_WIDTH + 2 * B_WIDTH) // B_WIDTH
    row = lambda b, c: b * nc + c
    return pl.pallas_call(
        _mlstm_kernel,
        grid=(batch, nc),
        in_specs=[
            pl.BlockSpec((L, 2 * B_WIDTH), lambda b, c: (row(b, c), qk_blk)),
            pl.BlockSpec((L, B_WIDTH), lambda b, c: (row(b, c), v_blk)),
            pl.BlockSpec((L, B_WIDTH), lambda b, c: (row(b, c), v_blk + 1)),
            pl.BlockSpec((L, LANES), lambda b, c: (row(b, c), 0)),
            pl.BlockSpec((1, LANES), lambda b, c: (0, 0)),
            pl.BlockSpec((B_CONV, 2 * B_WIDTH), lambda b, c: (0, 0)),
            pl.BlockSpec((1, B_WIDTH), lambda b, c: (0, 0)),
        ],
        out_specs=pl.BlockSpec((L, B_WIDTH), lambda b, c: (row(b, c), 0)),
        out_shape=jax.ShapeDtypeStruct((T, B_WIDTH), BF16),
        scratch_shapes=[
            pltpu.VMEM((L + 8, 2 * B_WIDTH), F32),
            pltpu.VMEM((B_HEADS, B_HEAD_DIM, B_HEAD_DIM), F32),
            pltpu.VMEM((8, B_HEAD_DIM), F32),
            pltpu.VMEM((8, LANES), F32),
        ],
        compiler_params=_params("arbitrary", "arbitrary"),
        name="mlstm",
    )(p, p, p, gates, gate_b, conv_w, m_norm)


def _outproj_kernel(*refs, n_lhs):
    h_ref = refs[0]
    o_ref = refs[1 + 2 * n_lhs]
    acc = h_ref[...]
    for i in range(n_lhs):
        acc = acc + _dot(refs[1 + i][...], refs[1 + n_lhs + i][...])
    o_ref[...] = acc


def _outproj(h, lhs, w, *, tm=512):
    T = h.shape[0]
    n = len(lhs)
    width = lhs[0].shape[1]
    assert all(a.shape[1] == width for a in lhs) and w.shape[0] == n * width
    in_specs = [pl.BlockSpec((tm, D_MODEL), lambda i: (i, 0))]
    in_specs += [pl.BlockSpec((tm, width), lambda i: (i, 0)) for _ in lhs]
    in_specs += [pl.BlockSpec((width, D_MODEL), functools.partial(lambda i, k: (k, 0), k=k)) for k in range(n)]
    return pl.pallas_call(
        functools.partial(_outproj_kernel, n_lhs=n),
        grid=(T // tm,),
        in_specs=in_specs,
        out_specs=pl.BlockSpec((tm, D_MODEL), lambda i: (i, 0)),
        out_shape=jax.ShapeDtypeStruct((T, D_MODEL), F32),
        compiler_params=_params("parallel"),
        name="outproj",
    )(h, *lhs, *([w] * n))


def _rope(x, ca, sn):
    half = ROPE_DIMS // 2
    lane = lax.broadcasted_iota(jnp.int32, x.shape, 1)
    partner = jnp.where(lane < half, lane + half, jnp.where(lane < ROPE_DIMS, lane - half, lane))
    return x * ca + jnp.take_along_axis(x, partner, axis=1) * sn


def _nsa_proj_kernel(h_ref, g_ref, w_ref, wgate_ref, ca_ref, sn_ref, o_ref, og_ref, n_ref):
    j = pl.program_id(1)

    @pl.when(j == 0)
    def _():
        n = _rms(h_ref[...], g_ref[...]).astype(BF16)
        n_ref[...] = n
        og_ref[...] = jax.nn.sigmoid(_dot(n, wgate_ref[...]))

    res = _dot(n_ref[...], w_ref[...].astype(BF16))
    n_q = C_HEADS // C_KV
    is_value = jnp.logical_and(j >= n_q, (j - n_q) % 2 == 1)

    @pl.when(jnp.logical_not(is_value))
    def _():
        ca, sn = ca_ref[0], sn_ref[0]
        for gi in range(C_KV):
            o_ref[0, gi] = _rope(res[:, gi * LANES:(gi + 1) * LANES], ca, sn).astype(BF16)

    @pl.when(is_value)
    def _():
        for gi in range(C_KV):
            o_ref[0, gi] = res[:, gi * LANES:(gi + 1) * LANES].astype(BF16)


def _nsa_proj(h, g, w, wgate, ca, sn, batch, *, tm=1024):
    T = h.shape[0]
    S = T // batch
    nts = S // tm
    tn = C_KV * C_HEAD_DIM
    n_q = C_HEADS // C_KV
    table = pl.BlockSpec((1, tm, LANES), lambda i, j: (jnp.where(j < n_q, 0, 1), i % nts, 0))
    return pl.pallas_call(
        _nsa_proj_kernel,
        grid=(T // tm, C_MAIN // tn),
        in_specs=[
            pl.BlockSpec((tm, D_MODEL), lambda i, j: (i, 0)),
            pl.BlockSpec((1, D_MODEL), lambda i, j: (0, 0)),
            pl.BlockSpec((D_MODEL, tn), lambda i, j: (0, j)),
            pl.BlockSpec((D_MODEL, LANES), lambda i, j: (0, 0)),
            table, table,
        ],
        out_specs=[
            pl.BlockSpec((1, C_KV, tm, C_HEAD_DIM), lambda i, j: (i // nts, j, i % nts, 0)),
            pl.BlockSpec((tm, LANES), lambda i, j: (i, 0)),
        ],
        out_shape=[
            jax.ShapeDtypeStruct((batch, HM_HEADS, S, C_HEAD_DIM), BF16),
            jax.ShapeDtypeStruct((T, LANES), F32),
        ],
        scratch_shapes=[pltpu.VMEM((tm, D_MODEL), BF16)],
        compiler_params=_params("parallel", "arbitrary"),
        name="nsa_proj",
    )(h, g, w, wgate, ca, sn)


def _compress_kernel(x_ref, pos_ref, w1_ref, w2_ref, o_ref, xs_ref):
    xs_ref[...] = x_ref[0, 0].astype(F32)
    n_rows = xs_ref.shape[0] // CMP_STRIDE
    dh = xs_ref.shape[1]
    ya = jnp.zeros((n_rows, dh), F32)
    yb = jnp.zeros((n_rows, dh), F32)
    for l in range(CMP_STRIDE):
        xl = xs_ref[pl.ds(l, n_rows, stride=CMP_STRIDE), :]
        ya = ya + _dot((xl + pos_ref[0, l:l + 1, :]).astype(BF16), w1_ref[0, l])
        lb = CMP_STRIDE + l
        yb = yb + _dot((xl + pos_ref[0, lb:lb + 1, :]).astype(BF16), w1_ref[0, lb])
    pre = ya + pltpu.roll(yb, n_rows - 1, 0)
    out = _dot(jax.nn.gelu(pre).astype(BF16), w2_ref[0])
    rows = lax.broadcasted_iota(jnp.int32, out.shape, 0)
    o_ref[0, 0, 0] = jnp.where(rows < n_rows - 1, out, 0.0).astype(BF16)


def _compress(qkv, pos, w1, w2):
    B_, _, S, dh = qkv.shape
    rows = S // CMP_STRIDE
    return pl.pallas_call(
        _compress_kernel,
        grid=(B_, 2, C_KV),
        in_specs=[
            pl.BlockSpec((1, 1, S, dh), lambda b, kv, g: (b, HM_KCMP + kv * C_KV + g, 0, 0)),
            pl.BlockSpec((1, CMP_LEN, dh), lambda b, kv, g: (kv, 0, 0)),
            pl.BlockSpec((1, CMP_LEN, dh, dh), lambda b, kv, g: (kv, 0, 0, 0)),
            pl.BlockSpec((1, dh, dh), lambda b, kv, g: (kv, 0, 0)),
        ],
        out_specs=pl.BlockSpec((1, 1, 1, rows, dh), lambda b, kv, g: (b, kv, g, 0, 0)),
        out_shape=jax.ShapeDtypeStruct((B_, 2, C_KV, rows, dh), BF16),
        scratch_shapes=[pltpu.VMEM((S, dh), F32)],
        compiler_params=_params("parallel", "parallel", "parallel"),
        name="nsa_compress",
    )(qkv, pos, w1, w2)


def _softmax_parts_t(s):
    m = jnp.max(s, axis=0, keepdims=True)
    p = jnp.exp2(s - m)
    return p, jnp.sum(p, axis=0, keepdims=True)


def _topk_membership_t(imp_t, n_top):
    n_blk, tq = imp_t.shape
    groups = n_blk // SUBLANES
    va = [imp_t[a * SUBLANES:(a + 1) * SUBLANES] for a in range(groups)]
    cnt = [jnp.zeros((SUBLANES, tq), F32) for _ in range(groups)]
    sub = lax.broadcasted_iota(jnp.int32, (SUBLANES, tq), 0)
    for i in range(n_blk):
        vi = jnp.broadcast_to(imp_t[i:i + 1, :], (SUBLANES, tq))
        for a in range(groups):
            if i < a * SUBLANES:
                ahead = jnp.where(vi >= va[a], 1.0, 0.0)
            elif i >= (a + 1) * SUBLANES:
                ahead = jnp.where(vi > va[a], 1.0, 0.0)
            else:
                ahead = jnp.where(sub > i - a * SUBLANES, jnp.where(vi >= va[a], 1.0, 0.0),
                                  jnp.where(vi > va[a], 1.0, 0.0))
            cnt[a] = cnt[a] + ahead
    return jnp.concatenate([jnp.where(c < n_top, 1.0, 0.0) for c in cnt], axis=0)


def _nsa_attn_kernel(q_ref, kc_ref, vc_ref, ks_ref, vs_ref, kw_ref, vw_ref, gate_ref, ovt_ref, y_ref,
                     qx_ref, m_ref, l_ref, acc_ref, oc_ref, ow_ref, *, tq, tk, n_slc, n_top):
    G, R, dh = C_KV, C_REP, C_HEAD_DIM
    t0 = pl.program_id(1) * tq
    pos = t0 + lax.broadcasted_iota(jnp.int32, (1, tq), 1)

    def q_of(g):
        return q_ref[0, g * R:(g + 1) * R].reshape(R * tq, dh)

    def per_head(x):
        return jnp.concatenate([x] * R, axis=1)

    n_cmp_rows = kc_ref.shape[3]
    cmp_end = lax.broadcasted_iota(jnp.int32, (n_cmp_rows, tq), 0) * CMP_STRIDE + (CMP_LEN - 1)
    ok_c = jnp.logical_and(cmp_end <= pos, cmp_end < n_cmp_rows * CMP_STRIDE)
    bias_c = per_head(jnp.where(ok_c, 0.0, NEG_INF))
    has_c = per_head(pos >= CMP_LEN - 1)
    slab = WINDOW + tq
    w0 = pl.multiple_of(jnp.maximum(t0 - WINDOW, 0), tq)
    kpos_w = w0 + lax.broadcasted_iota(jnp.int32, (slab, tq), 0)
    ok_w = jnp.logical_and(kpos_w <= pos, pos - kpos_w < WINDOW)
    bias_w = per_head(jnp.where(ok_w, 0.0, NEG_INF))
    blk_t = lax.broadcasted_iota(jnp.int32, (n_slc, tq), 0)
    cur_t = jnp.right_shift(pos, SEL_SHIFT)
    forced = jnp.logical_or(blk_t == 0, jnp.logical_or(blk_t == cur_t, blk_t == cur_t - 1))
    future = blk_t > cur_t

    s_cmp = [_dot_nt(kc_ref[0, 0, g], q_of(g)) + bias_c for g in range(G)]
    s_win = [_dot_nt(kw_ref[0, g, pl.ds(w0, slab), :], q_of(g)) + bias_w for g in range(G)]
    for g in range(G):
        p_c, l_c = _softmax_parts_t(s_cmp[g])
        p_c = p_c * jnp.where(has_c, 1.0 / l_c, 0.0)
        oc_ref[g] = _dot_tn(vc_ref[0, 0, g], p_c.astype(BF16))

        p_w, l_w = _softmax_parts_t(s_win[g])
        ow_ref[g] = _dot_tn(vw_ref[0, g, pl.ds(w0, slab), :], p_w.astype(BF16)) * (1.0 / l_w)

        p_sum = p_c[:, 0:tq]
        for r in range(1, R):
            p_sum = p_sum + p_c[:, r * tq:(r + 1) * tq]
        imp_t = jnp.where(forced, BIG, jnp.where(future, -BIG, _dot_f32(ovt_ref[...], p_sum)))
        sel_t = _topk_membership_t(imp_t, n_top)
        sel = jnp.concatenate([sel_t, jnp.zeros((LANES - n_slc, tq), F32)], axis=0).T
        penalty = jnp.where(sel > 0.5, 0.0, NEG_INF).astype(BF16)
        qx_ref[g] = jnp.concatenate([q_of(g), jnp.concatenate([penalty] * R, axis=0)], axis=1)

    m_ref[...] = jnp.full(m_ref.shape, NEG_INF, F32)
    l_ref[...] = jnp.zeros(l_ref.shape, F32)
    acc_ref[...] = jnp.zeros(acc_ref.shape, F32)
    blocks_per_tile = tk // SEL_LEN

    def sweep_tile(kt, diagonal):
        k0 = pl.multiple_of(kt * tk, tk)
        key_blk = kt * blocks_per_tile + jnp.right_shift(lax.broadcasted_iota(jnp.int32, (tk, LANES), 0), SEL_SHIFT)
        onehot = jnp.where(lax.broadcasted_iota(jnp.int32, (tk, LANES), 1) == key_blk, 1.0, 0.0).astype(BF16)
        s_all = []
        for g in range(G):
            k_ext = jnp.concatenate([ks_ref[0, g, pl.ds(k0, tk), :], onehot], axis=1)
            s = _dot_nt(k_ext, qx_ref[g])
            if diagonal:
                causal = k0 + lax.broadcasted_iota(jnp.int32, (tk, tq), 0) <= pos
                s = s + per_head(jnp.where(causal, 0.0, NEG_INF))
            s_all.append(s)
        for g, s in enumerate(s_all):
            m_old = m_ref[g]
            m_new = jnp.maximum(m_old, jnp.max(s, axis=0, keepdims=True))
            p = jnp.exp2(s - m_new)
            alpha = jnp.exp2(m_old - m_new)
            l_ref[g] = alpha * l_ref[g] + jnp.sum(p, axis=0, keepdims=True)
            acc_ref[g] = alpha * acc_ref[g] + _dot_tn(vs_ref[0, g, pl.ds(k0, tk), :], p.astype(BF16))
            m_ref[g] = m_new

    def sweep(kt, carry):
        sweep_tile(kt, diagonal=False)
        return carry

    last_tile = t0 // tk
    lax.fori_loop(0, last_tile, sweep, 0)
    sweep_tile(last_tile, diagonal=True)

    gates_t = gate_ref[...].T
    for g in range(G):
        os_t = acc_ref[g] * (1.0 / l_ref[g])
        oc_t, ow_t = oc_ref[g], ow_ref[g]
        for r in range(R):
            cols = slice(r * tq, (r + 1) * tq)
            hd = g * R + r
            merged_t = (gates_t[hd:hd + 1, :] * oc_t[:, cols]
                        + gates_t[C_HEADS + hd:C_HEADS + hd + 1, :] * os_t[:, cols]
                        + gates_t[2 * C_HEADS + hd:2 * C_HEADS + hd + 1, :] * ow_t[:, cols])
            y_ref[:, (g * R + r) * dh:(g * R + r + 1) * dh] = merged_t.T.astype(BF16)


def _nsa_attn(qkv, cmp_kv, gates, overlap_t, *, tq=128, tk=512):
    B_, _, S, dh = qkv.shape
    nq = S // tq
    n_slc = S // SEL_LEN
    n_top = min(SEL_TOPK, n_slc)
    assert n_top >= 3 and n_slc % SUBLANES == 0
    tk = min(tk, S)
    assert tq % LANES == 0 and tk % tq == 0 and n_slc <= LANES
    n_cmp_rows = cmp_kv.shape[3]
    G, R = C_KV, C_REP
    seq = lambda base: pl.BlockSpec((1, G, S, dh), lambda b, i: (b, base // G, 0, 0), pipeline_mode=pl.Buffered(1))
    return pl.pallas_call(
        functools.partial(_nsa_attn_kernel, tq=tq, tk=tk, n_slc=n_slc, n_top=n_top),
        grid=(B_, nq),
        in_specs=[
            pl.BlockSpec((1, C_HEADS, tq, dh), lambda b, i: (b, 0, i, 0)),
            pl.BlockSpec((1, 1, G, n_cmp_rows, dh), lambda b, i: (b, 0, 0, 0, 0)),
            pl.BlockSpec((1, 1, G, n_cmp_rows, dh), lambda b, i: (b, 1, 0, 0, 0)),
            seq(HM_KSEL), seq(HM_VSEL), seq(HM_KWIN), seq(HM_VWIN),
            pl.BlockSpec((tq, LANES), lambda b, i: (b * nq + i, 0)),
            pl.BlockSpec((n_slc, n_cmp_rows), lambda b, i: (0, 0)),
        ],
        out_specs=pl.BlockSpec((tq, C_HEADS * dh), lambda b, i: (b * nq + i, 0)),
        out_shape=jax.ShapeDtypeStruct((B_ * S, C_HEADS * dh), BF16),
        scratch_shapes=[
            pltpu.VMEM((G, R * tq, dh + LANES), BF16),
            pltpu.VMEM((G, 1, R * tq), F32),
            pltpu.VMEM((G, 1, R * tq), F32),
            pltpu.VMEM((G, dh, R * tq), F32),
            pltpu.VMEM((G, dh, R * tq), F32),
            pltpu.VMEM((G, dh, R * tq), F32),
        ],
        compiler_params=_params("parallel", "arbitrary"),
        name="nsa_attn",
    )(qkv, cmp_kv, cmp_kv, qkv, qkv, qkv, qkv, gates, overlap_t)


def _rope_tables(S):
    half = ROPE_DIMS // 2
    inv = 1.0 / (ROPE_THETA ** (jnp.arange(half, dtype=F32) / half))
    ang = jnp.arange(S, dtype=F32)[:, None] * inv[None, :]
    cos, sin = jnp.cos(ang), jnp.sin(ang)
    rest = LANES - ROPE_DIMS
    ca = jnp.concatenate([cos, cos, jnp.ones((S, rest), F32)], axis=-1)
    sn = jnp.concatenate([-sin, sin, jnp.zeros((S, rest), F32)], axis=-1)
    q_scale = C_HEAD_DIM ** -0.5 * LOG2E
    return jnp.stack([ca * q_scale, ca]), jnp.stack([sn * q_scale, sn])


def _overlap_matrix(n_cmp_rows, n_slc):
    ci = jnp.arange(n_cmp_rows)[:, None] * CMP_STRIDE
    sj = jnp.arange(n_slc)[None, :] * SEL_LEN
    ov = jnp.clip(jnp.minimum(ci + CMP_LEN, sj + SEL_LEN) - jnp.maximum(ci, sj), 0).astype(F32) / CMP_LEN
    return ov


def _gmlp_mlstm_layer(h, mix_g, w_in, gate_b, g_norm, g_ws, g_bs, conv_w, m_norm, w_out, batch):
    w_main = w_in
    w_gate = jnp.pad(w_in[:, AB_MAIN:], ((0, 0), (0, LANES - 2 * B_HEADS))).astype(BF16)
    p, gates = _ab_proj(h, mix_g.reshape(1, D_MODEL), w_main, w_gate)
    y_a = _gmlp(p, g_norm.reshape(1, A_WIDTH), g_ws, g_bs.T)
    gb = jnp.pad(gate_b.reshape(1, 2 * B_HEADS), ((0, 0), (0, LANES - 2 * B_HEADS)))
    y_b = _mlstm(p, gates, gb, conv_w, m_norm.reshape(1, B_WIDTH), batch)
    return _outproj(h, [y_a, y_b], w_out.astype(BF16))


def _nsa_layer(h, mix_g, w_in, cmp_pos, cmp_w1, cmp_w2, w_out, batch):
    T = h.shape[0]
    S = T // batch
    w_main = w_in
    wg = jnp.pad(w_in[:, C_MAIN:], ((0, 0), (0, LANES - 3 * C_HEADS))).astype(BF16)
    ca, sn = _rope_tables(S)
    qkv, gates = _nsa_proj(h, mix_g.reshape(1, D_MODEL), w_main, wg, ca, sn, batch)
    cmp_kv = _compress(qkv, cmp_pos, cmp_w1.reshape(2, CMP_LEN, C_HEAD_DIM, C_HEAD_DIM).astype(BF16),
                       cmp_w2.astype(BF16))
    overlap_t = _overlap_matrix(S // CMP_STRIDE, S // SEL_LEN).T
    y = _nsa_attn(qkv, cmp_kv, gates, overlap_t)
    return _outproj(h, [y], w_out.astype(BF16))


def kernel(x, ffn_norm, ffn_w_gate, ffn_w_up, ffn_w_down, mix_norm, ab_w_in, mlstm_gate_bias, gmlp_norm, gmlp_w_s,
           gmlp_b_s, mlstm_conv, mlstm_norm, ab_w_out, nsa_w_in, nsa_cmp_pos, nsa_cmp_w1, nsa_cmp_w2, nsa_w_out,
           final_norm):
    batch, S, _ = x.shape
    depth = ffn_norm.shape[0]
    h = x.reshape(batch * S, D_MODEL)
    fg = final_norm.reshape(1, D_MODEL)

    wgu, wd = _ffn_weight_tiles(ffn_w_gate, ffn_w_up), ffn_w_down.astype(BF16)

    def ffn(h, layer, half, final=False):
        return _ffn(h, ffn_norm[layer, half].reshape(1, D_MODEL), wgu, wd, fg, layer, half, final=final)

    for layer in range(depth):
        j = layer // 2
        h = ffn(h, layer, 0)
        if layer % 2 == 0:
            h = _gmlp_mlstm_layer(h, mix_norm[layer], ab_w_in[j], mlstm_gate_bias[j], gmlp_norm[j], gmlp_w_s[j],
                                  gmlp_b_s[j], mlstm_conv[j], mlstm_norm[j], ab_w_out[j], batch)
        else:
            h = _nsa_layer(h, mix_norm[layer], nsa_w_in[j], nsa_cmp_pos[j], nsa_cmp_w1[j], nsa_cmp_w2[j],
                           nsa_w_out[j], batch)
        h = ffn(h, layer, 1, final=(layer == depth - 1))
    return h.reshape(batch, S, D_MODEL)
```

```python
import functools
import math

import jax
import jax.numpy as jnp
from jax import lax
from jax.experimental import pallas as pl
from jax.experimental.pallas import tpu as pltpu

F32 = jnp.float32
BF16 = jnp.bfloat16

D_MODEL = 2048
D_FF = 5632
EPS = 1e-6

A_WIDTH = D_MODEL // 2
A_CHUNK = 128
A_GROUPS = 8
A_GROUP_DIM = A_WIDTH // A_GROUPS

B_HEADS = 4
B_WIDTH = D_MODEL // 2
B_HEAD_DIM = B_WIDTH // B_HEADS
B_CHUNK = 128
B_CONV = 4
AB_MAIN = 2 * A_WIDTH + 4 * B_WIDTH

C_HEADS = 16
C_KV = 4
C_REP = C_HEADS // C_KV
C_HEAD_DIM = D_MODEL // C_HEADS
CMP_LEN = 32
CMP_STRIDE = 16
SEL_LEN = 64
SEL_SHIFT = SEL_LEN.bit_length() - 1
SEL_TOPK = 16
WINDOW = 512
C_MAIN = C_HEADS * C_HEAD_DIM + 6 * C_KV * C_HEAD_DIM
ROPE_THETA = 500000.0
ROPE_DIMS = C_HEAD_DIM // 4
NEG_INF = -1e30
BIG = 1e9
LOG2E = math.log2(math.e)

LANES = 128
SUBLANES = 8
VMEM_LIMIT = 56 * 1024 * 1024

HM_HEADS = C_HEADS + 6 * C_KV
HM_KCMP, HM_VCMP, HM_KSEL, HM_VSEL, HM_KWIN, HM_VWIN = (C_HEADS + i * C_KV for i in range(6))


def _params(*sem):
    return pltpu.CompilerParams(dimension_semantics=sem, vmem_limit_bytes=VMEM_LIMIT)


def _rms(x, g):
    return x * lax.rsqrt(jnp.mean(x * x, axis=-1, keepdims=True) + EPS) * g


def _dot(a, b):
    return jnp.dot(a, b, preferred_element_type=F32)


def _dot_nt(a, b):
    return lax.dot_general(a, b, (((1,), (1,)), ((), ())), preferred_element_type=F32)


def _dot_tn(a, b):
    return lax.dot_general(a, b, (((0,), (0,)), ((), ())), preferred_element_type=F32)


def _dot_f32(a, b):
    return jnp.dot(a, b, preferred_element_type=F32, precision=lax.Precision.HIGHEST)


def _ffn_kernel(h_ref, g_ref, wg_ref, wu_ref, wd_ref, fg_ref, o_ref, n_ref, *, final):
    j = pl.program_id(1)

    @pl.when(j == 0)
    def _():
        x = h_ref[...]
        n_ref[...] = _rms(x, g_ref[...]).astype(BF16)
        o_ref[...] = x

    n = n_ref[...]
    a = _dot(n, wg_ref[...])
    b = _dot(n, wu_ref[...])
    t = (a * jax.nn.sigmoid(a) * (0.5 * b)).astype(BF16)
    o_ref[...] += _dot(t, wd_ref[...])

    if final:
        @pl.when(j == pl.num_programs(1) - 1)
        def _():
            o_ref[...] = _rms(o_ref[...], fg_ref[...])


def _ffn(h, g, wg, wu, wd, fg, layer, half, *, final, tm=1024, tf=512):
    T = h.shape[0]
    return pl.pallas_call(
        functools.partial(_ffn_kernel, final=final),
        grid=(T // tm, D_FF // tf),
        in_specs=[
            pl.BlockSpec((tm, D_MODEL), lambda i, j: (i, 0)),
            pl.BlockSpec((1, D_MODEL), lambda i, j: (0, 0)),
            pl.BlockSpec((None, None, D_MODEL, tf), lambda i, j: (layer, half, 0, j)),
            pl.BlockSpec((None, None, D_MODEL, tf), lambda i, j: (layer, half, 0, j)),
            pl.BlockSpec((None, None, tf, D_MODEL), lambda i, j: (layer, half, j, 0)),
            pl.BlockSpec((1, D_MODEL), lambda i, j: (0, 0)),
        ],
        out_specs=pl.BlockSpec((tm, D_MODEL), lambda i, j: (i, 0)),
        out_shape=jax.ShapeDtypeStruct((T, D_MODEL), F32),
        scratch_shapes=[pltpu.VMEM((tm, D_MODEL), BF16)],
        compiler_params=_params("parallel", "arbitrary"),
        name="ffn_final" if final else "ffn",
    )(h, g, wg, wu, wd, fg)


def _ab_proj_kernel(h_ref, g_ref, w_ref, wgate_ref, o_ref, og_ref, n_ref):
    j = pl.program_id(1)

    @pl.when(j == 0)
    def _():
        n = _rms(h_ref[...], g_ref[...]).astype(BF16)
        n_ref[...] = n
        og_ref[...] = _dot(n, wgate_ref[...])

    o_ref[...] = _dot(n_ref[...], w_ref[...])


def _ab_proj(h, g, w, wgate, *, tm=1024, tn=1536):
    T = h.shape[0]
    return pl.pallas_call(
        _ab_proj_kernel,
        grid=(T // tm, AB_MAIN // tn),
        in_specs=[
            pl.BlockSpec((tm, D_MODEL), lambda i, j: (i, 0)),
            pl.BlockSpec((1, D_MODEL), lambda i, j: (0, 0)),
            pl.BlockSpec((D_MODEL, tn), lambda i, j: (0, j)),
            pl.BlockSpec((D_MODEL, LANES), lambda i, j: (0, 0)),
        ],
        out_specs=[
            pl.BlockSpec((tm, tn), lambda i, j: (i, j)),
            pl.BlockSpec((tm, LANES), lambda i, j: (i, 0)),
        ],
        out_shape=[
            jax.ShapeDtypeStruct((T, AB_MAIN), F32),
            jax.ShapeDtypeStruct((T, LANES), F32),
        ],
        scratch_shapes=[pltpu.VMEM((tm, D_MODEL), BF16)],
        compiler_params=_params("parallel", "arbitrary"),
        name="ab_proj",
    )(h, g, w, wgate)


def _gmlp_kernel(u_ref, v_ref, g_ref, ws_ref, bs_ref, o_ref):
    zu = jax.nn.gelu(u_ref[...])
    vn = _rms(jax.nn.gelu(v_ref[...]), g_ref[...]).astype(BF16)
    row = lax.broadcasted_iota(jnp.int32, (A_CHUNK, A_CHUNK), 0)
    col = lax.broadcasted_iota(jnp.int32, (A_CHUNK, A_CHUNK), 1)
    causal = col <= row
    bs = bs_ref[...]
    for gi in range(A_GROUPS):
        sl = slice(gi * A_GROUP_DIM, (gi + 1) * A_GROUP_DIM)
        w = jnp.where(causal, ws_ref[gi], 0.0).astype(BF16)
        for c in range(u_ref.shape[0] // A_CHUNK):
            rows = slice(c * A_CHUNK, (c + 1) * A_CHUNK)
            sv = _dot(w, vn[rows, sl]) + bs[:, gi:gi + 1]
            o_ref[rows, sl] = (zu[rows, sl] * sv).astype(BF16)


def _gmlp(p, g, ws, bs_t, *, chunks_per_step=8):
    T = p.shape[0]
    tm = chunks_per_step * A_CHUNK
    return pl.pallas_call(
        _gmlp_kernel,
        grid=(T // tm,),
        in_specs=[
            pl.BlockSpec((tm, A_WIDTH), lambda c: (c, 0)),
            pl.BlockSpec((tm, A_WIDTH), lambda c: (c, 1)),
            pl.BlockSpec((1, A_WIDTH), lambda c: (0, 0)),
            pl.BlockSpec((A_GROUPS, A_CHUNK, A_CHUNK), lambda c: (0, 0, 0)),
            pl.BlockSpec((A_CHUNK, A_GROUPS), lambda c: (0, 0)),
        ],
        out_specs=pl.BlockSpec((tm, A_WIDTH), lambda c: (c, 0)),
        out_shape=jax.ShapeDtypeStruct((T, A_WIDTH), BF16),
        compiler_params=_params("parallel"),
        name="gmlp",
    )(p, p, g, ws, bs_t)


def _log_sigmoid(x):
    return jnp.minimum(x, 0.0) - jnp.log1p(jnp.exp(-jnp.abs(x)))


def _mlstm_kernel(qk_ref, v_ref, o_ref, gate_ref, gb_ref, cw_ref, mn_ref, y_ref,
                  xx_ref, c_ref, n_ref, m_ref):
    L, d = B_CHUNK, B_HEAD_DIM
    c = pl.program_id(1)
    tail = 8

    @pl.when(c == 0)
    def _():
        xx_ref[0:tail, :] = jnp.zeros((tail, 2 * B_WIDTH), F32)
        c_ref[...] = jnp.zeros_like(c_ref)
        n_ref[...] = jnp.zeros_like(n_ref)
        m_ref[...] = jnp.zeros_like(m_ref)

    xx_ref[tail:tail + L, :] = qk_ref[...]
    cw = cw_ref[...]
    conv = jnp.zeros((L, 2 * B_WIDTH), F32)
    for kk in range(B_CONV):
        off = tail - (B_CONV - 1) + kk
        conv = conv + cw[kk:kk + 1, :] * xx_ref[off:off + L, :]
    carry_rows = xx_ref[L:L + tail, :]
    xx_ref[0:tail, :] = carry_rows
    qk = conv * jax.nn.sigmoid(conv)

    gcol = gate_ref[...] + gb_ref[...]
    grow = gcol.T
    r_i = lax.broadcasted_iota(jnp.int32, (L, L), 0)
    c_i = lax.broadcasted_iota(jnp.int32, (L, L), 1)
    causal = c_i <= r_i
    tril = causal.astype(F32)
    triu = (r_i <= c_i).astype(F32)
    bcol_all = _dot_f32(tril, _log_sigmoid(gcol))
    brow_all = _dot_f32(_log_sigmoid(grow), triu)

    for hh in range(B_HEADS):
        q = qk[:, hh * d:(hh + 1) * d]
        k = qk[:, B_WIDTH + hh * d:B_WIDTH + (hh + 1) * d] * (d ** -0.5)
        v = v_ref[:, hh * d:(hh + 1) * d]
        qb, kb, vb = q.astype(BF16), k.astype(BF16), v.astype(BF16)
        i_col = gcol[:, hh:hh + 1]
        i_row = grow[hh:hh + 1, :]
        b_col = bcol_all[:, B_HEADS + hh:B_HEADS + hh + 1]
        b_row = brow_all[B_HEADS + hh:B_HEADS + hh + 1, :]
        b_last = b_col[L - 1:L, :]
        m_old = m_ref[hh:hh + 1, 0:1]

        logd = jnp.where(causal, b_col - b_row + i_row, NEG_INF)
        inter = b_col + m_old
        m_t = jnp.maximum(jnp.max(logd, axis=-1, keepdims=True), inter)
        dm = jnp.where(causal, jnp.exp(logd - m_t), 0.0)
        w_inter = jnp.exp(inter - m_t)
        s = _dot_nt(qb, kb) * dm
        ct = c_ref[hh]
        nvec = n_ref[hh:hh + 1, :]
        num = _dot(s.astype(BF16), vb) + w_inter * _dot(qb, ct.astype(BF16))
        den = jnp.sum(s, axis=-1, keepdims=True) + w_inter * jnp.sum(q * nvec, axis=-1, keepdims=True)
        hout = num / jnp.maximum(jnp.abs(den), jnp.exp(-m_t))

        m_new = m_t[L - 1:L, :]
        w_s = jnp.exp(b_last - b_col + i_col - m_new)
        w_prev = jnp.exp(b_last + m_old - m_new)
        c_ref[hh] = w_prev * ct + _dot_tn(kb, (w_s * v).astype(BF16))
        n_ref[hh:hh + 1, :] = w_prev * nvec + jnp.sum(w_s * k, axis=0, keepdims=True)
        m_ref[hh:hh + 1, :] = jnp.broadcast_to(m_new, (1, LANES))

        hn = _rms(hout, mn_ref[:, hh * d:(hh + 1) * d])
        og = jax.nn.sigmoid(o_ref[:, hh * d:(hh + 1) * d])
        y_ref[:, hh * d:(hh + 1) * d] = (og * hn).astype(BF16)


def _mlstm(p, gates, gate_b, conv_w, m_norm, batch):
    T = p.shape[0]
    L = B_CHUNK
    nc = T // batch // L
    qk_blk = (2 * A_WIDTH) // (2 * B_WIDTH)
    v_blk = (2 * A_WIDTH + 2 * B_WIDTH) // B_WIDTH
    row = lambda b, c: b * nc + c
    return pl.pallas_call(
        _mlstm_kernel,
        grid=(batch, nc),
        in_specs=[
            pl.BlockSpec((L, 2 * B_WIDTH), lambda b, c: (row(b, c), qk_blk)),
            pl.BlockSpec((L, B_WIDTH), lambda b, c: (row(b, c), v_blk)),
            pl.BlockSpec((L, B_WIDTH), lambda b, c: (row(b, c), v_blk + 1)),
            pl.BlockSpec((L, LANES), lambda b, c: (row(b, c), 0)),
            pl.BlockSpec((1, LANES), lambda b, c: (0, 0)),
            pl.BlockSpec((B_CONV, 2 * B_WIDTH), lambda b, c: (0, 0)),
            pl.BlockSpec((1, B_WIDTH), lambda b, c: (0, 0)),
        ],
        out_specs=pl.BlockSpec((L, B_WIDTH), lambda b, c: (row(b, c), 0)),
        out_shape=jax.ShapeDtypeStruct((T, B_WIDTH), BF16),
        scratch_shapes=[
            pltpu.VMEM((L + 8, 2 * B_WIDTH), F32),
            pltpu.VMEM((B_HEADS, B_HEAD_DIM, B_HEAD_DIM), F32),
            pltpu.VMEM((8, B_HEAD_DIM), F32),
            pltpu.VMEM((8, LANES), F32),
        ],
        compiler_params=_params("arbitrary", "arbitrary"),
        name="mlstm",
    )(p, p, p, gates, gate_b, conv_w, m_norm)


def _outproj_kernel(*refs, n_lhs):
    h_ref = refs[0]
    o_ref = refs[1 + 2 * n_lhs]
    acc = h_ref[...]
    for i in range(n_lhs):
        acc = acc + _dot(refs[1 + i][...], refs[1 + n_lhs + i][...])
    o_ref[...] = acc


def _outproj(h, lhs, w, *, tm=512):
    T = h.shape[0]
    n = len(lhs)
    width = lhs[0].shape[1]
    assert all(a.shape[1] == width for a in lhs) and w.shape[0] == n * width
    in_specs = [pl.BlockSpec((tm, D_MODEL), lambda i: (i, 0))]
    in_specs += [pl.BlockSpec((tm, width), lambda i: (i, 0)) for _ in lhs]
    in_specs += [pl.BlockSpec((width, D_MODEL), functools.partial(lambda i, k: (k, 0), k=k)) for k in range(n)]
    return pl.pallas_call(
        functools.partial(_outproj_kernel, n_lhs=n),
        grid=(T // tm,),
        in_specs=in_specs,
        out_specs=pl.BlockSpec((tm, D_MODEL), lambda i: (i, 0)),
        out_shape=jax.ShapeDtypeStruct((T, D_MODEL), F32),
        compiler_params=_params("parallel"),
        name="outproj",
    )(h, *lhs, *([w] * n))


def _rope(x, ca, sn):
    half = ROPE_DIMS // 2
    lane = lax.broadcasted_iota(jnp.int32, x.shape, 1)
    partner = jnp.where(lane < half, lane + half, jnp.where(lane < ROPE_DIMS, lane - half, lane))
    return x * ca + jnp.take_along_axis(x, partner, axis=1) * sn


def _nsa_proj_kernel(h_ref, g_ref, w_ref, wgate_ref, ca_ref, sn_ref, o_ref, og_ref, n_ref):
    j = pl.program_id(1)

    @pl.when(j == 0)
    def _():
        n = _rms(h_ref[...], g_ref[...]).astype(BF16)
        n_ref[...] = n
        og_ref[...] = jax.nn.sigmoid(_dot(n, wgate_ref[...]))

    res = _dot(n_ref[...], w_ref[...])
    heads = o_ref.shape[1]
    is_q = j < C_HEADS // heads

    def head(k):
        return res[:, k * LANES:(k + 1) * LANES]

    @pl.when(is_q)
    def _():
        ca, sn = ca_ref[0], sn_ref[0]
        for k in range(heads):
            o_ref[0, k] = _rope(head(k), ca, sn).astype(BF16)

    @pl.when(jnp.logical_not(is_q))
    def _():
        ca, sn = ca_ref[0], sn_ref[0]
        for k in range(C_KV):
            o_ref[0, k] = _rope(head(k), ca, sn).astype(BF16)
            o_ref[0, C_KV + k] = head(C_KV + k).astype(BF16)


def _nsa_proj(h, g, w, wgate, ca, sn, batch, *, tm=1024):
    T = h.shape[0]
    S = T // batch
    nts = S // tm
    heads = 2 * C_KV
    tn = heads * C_HEAD_DIM
    n_q = C_HEADS // heads
    table = pl.BlockSpec((1, tm, LANES), lambda i, j: (jnp.where(j < n_q, 0, 1), i % nts, 0))
    return pl.pallas_call(
        _nsa_proj_kernel,
        grid=(T // tm, C_MAIN // tn),
        in_specs=[
            pl.BlockSpec((tm, D_MODEL), lambda i, j: (i, 0)),
            pl.BlockSpec((1, D_MODEL), lambda i, j: (0, 0)),
            pl.BlockSpec((D_MODEL, tn), lambda i, j: (0, j)),
            pl.BlockSpec((D_MODEL, LANES), lambda i, j: (0, 0)),
            table, table,
        ],
        out_specs=[
            pl.BlockSpec((1, heads, tm, C_HEAD_DIM), lambda i, j: (i // nts, j, i % nts, 0)),
            pl.BlockSpec((tm, LANES), lambda i, j: (i, 0)),
        ],
        out_shape=[
            jax.ShapeDtypeStruct((batch, HM_HEADS, S, C_HEAD_DIM), BF16),
            jax.ShapeDtypeStruct((T, LANES), F32),
        ],
        scratch_shapes=[pltpu.VMEM((tm, D_MODEL), BF16)],
        compiler_params=_params("parallel", "arbitrary"),
        name="nsa_proj",
    )(h, g, w, wgate, ca, sn)


def _compress_kernel(x_ref, pos_ref, w1_ref, w2_ref, o_ref, xs_ref):
    xs_ref[...] = x_ref[0, 0].astype(F32)
    n_rows = xs_ref.shape[0] // CMP_STRIDE
    dh = xs_ref.shape[1]
    ya = jnp.zeros((n_rows, dh), F32)
    yb = jnp.zeros((n_rows, dh), F32)
    for l in range(CMP_STRIDE):
        xl = xs_ref[pl.ds(l, n_rows, stride=CMP_STRIDE), :]
        ya = ya + _dot((xl + pos_ref[0, l:l + 1, :]).astype(BF16), w1_ref[0, l])
        lb = CMP_STRIDE + l
        yb = yb + _dot((xl + pos_ref[0, lb:lb + 1, :]).astype(BF16), w1_ref[0, lb])
    pre = ya + pltpu.roll(yb, n_rows - 1, 0)
    out = _dot(jax.nn.gelu(pre).astype(BF16), w2_ref[0])
    rows = lax.broadcasted_iota(jnp.int32, out.shape, 0)
    o_ref[0, 0, 0] = jnp.where(rows < n_rows - 1, out, 0.0).astype(BF16)


def _compress(qkv, pos, w1, w2):
    B_, _, S, dh = qkv.shape
    rows = S // CMP_STRIDE
    return pl.pallas_call(
        _compress_kernel,
        grid=(B_, 2, C_KV),
        in_specs=[
            pl.BlockSpec((1, 1, S, dh), lambda b, kv, g: (b, HM_KCMP + kv * C_KV + g, 0, 0)),
            pl.BlockSpec((1, CMP_LEN, dh), lambda b, kv, g: (kv, 0, 0)),
            pl.BlockSpec((1, CMP_LEN, dh, dh), lambda b, kv, g: (kv, 0, 0, 0)),
            pl.BlockSpec((1, dh, dh), lambda b, kv, g: (kv, 0, 0)),
        ],
        out_specs=pl.BlockSpec((1, 1, 1, rows, dh), lambda b, kv, g: (b, kv, g, 0, 0)),
        out_shape=jax.ShapeDtypeStruct((B_, 2, C_KV, rows, dh), BF16),
        scratch_shapes=[pltpu.VMEM((S, dh), F32)],
        compiler_params=_params("parallel", "parallel", "parallel"),
        name="nsa_compress",
    )(qkv, pos, w1, w2)


def _softmax_parts_t(s):
    m = jnp.max(s, axis=0, keepdims=True)
    p = jnp.exp2(s - m)
    return p, jnp.sum(p, axis=0, keepdims=True)


def _topk_membership_t(imp_t, n_top):
    n_blk, tq = imp_t.shape
    groups = n_blk // SUBLANES
    va = [imp_t[a * SUBLANES:(a + 1) * SUBLANES] for a in range(groups)]
    cnt = [jnp.zeros((SUBLANES, tq), F32) for _ in range(groups)]
    sub = lax.broadcasted_iota(jnp.int32, (SUBLANES, tq), 0)
    for i in range(n_blk):
        vi = jnp.broadcast_to(imp_t[i:i + 1, :], (SUBLANES, tq))
        for a in range(groups):
            if i < a * SUBLANES:
                ahead = jnp.where(vi >= va[a], 1.0, 0.0)
            elif i >= (a + 1) * SUBLANES:
                ahead = jnp.where(vi > va[a], 1.0, 0.0)
            else:
                ahead = jnp.where(sub > i - a * SUBLANES, jnp.where(vi >= va[a], 1.0, 0.0),
                                  jnp.where(vi > va[a], 1.0, 0.0))
            cnt[a] = cnt[a] + ahead
    return jnp.concatenate([jnp.where(c < n_top, 1.0, 0.0) for c in cnt], axis=0)


def _nsa_attn_kernel(q_ref, kc_ref, vc_ref, ks_ref, vs_ref, kw_ref, vw_ref, gate_ref, ovt_ref, y_ref,
                     qx_ref, m_ref, l_ref, acc_ref, oc_ref, ow_ref, *, tq, tk, n_slc, n_top):
    G, R, dh = C_KV, C_REP, C_HEAD_DIM
    t0 = pl.program_id(1) * tq
    pos = t0 + lax.broadcasted_iota(jnp.int32, (1, tq), 1)

    def q_of(g):
        return q_ref[0, g * R:(g + 1) * R].reshape(R * tq, dh)

    def per_head(x):
        return jnp.concatenate([x] * R, axis=1)

    n_cmp_rows = kc_ref.shape[3]
    cmp_end = lax.broadcasted_iota(jnp.int32, (n_cmp_rows, tq), 0) * CMP_STRIDE + (CMP_LEN - 1)
    ok_c = jnp.logical_and(cmp_end <= pos, cmp_end < n_cmp_rows * CMP_STRIDE)
    bias_c = per_head(jnp.where(ok_c, 0.0, NEG_INF))
    has_c = per_head(pos >= CMP_LEN - 1)
    slab = WINDOW + tq
    w0 = pl.multiple_of(jnp.maximum(t0 - WINDOW, 0), tq)
    kpos_w = w0 + lax.broadcasted_iota(jnp.int32, (slab, tq), 0)
    ok_w = jnp.logical_and(kpos_w <= pos, pos - kpos_w < WINDOW)
    bias_w = per_head(jnp.where(ok_w, 0.0, NEG_INF))
    blk_t = lax.broadcasted_iota(jnp.int32, (n_slc, tq), 0)
    cur_t = jnp.right_shift(pos, SEL_SHIFT)
    forced = jnp.logical_or(blk_t == 0, jnp.logical_or(blk_t == cur_t, blk_t == cur_t - 1))
    future = blk_t > cur_t

    s_cmp = [_dot_nt(kc_ref[0, 0, g], q_of(g)) + bias_c for g in range(G)]
    s_win = [_dot_nt(kw_ref[0, g, pl.ds(w0, slab), :], q_of(g)) + bias_w for g in range(G)]
    for g in range(G):
        p_c, l_c = _softmax_parts_t(s_cmp[g])
        p_c = p_c * jnp.where(has_c, 1.0 / l_c, 0.0)
        oc_ref[g] = _dot_tn(vc_ref[0, 0, g], p_c.astype(BF16))

        p_w, l_w = _softmax_parts_t(s_win[g])
        ow_ref[g] = _dot_tn(vw_ref[0, g, pl.ds(w0, slab), :], p_w.astype(BF16)) * (1.0 / l_w)

        p_sum = p_c[:, 0:tq]
        for r in range(1, R):
            p_sum = p_sum + p_c[:, r * tq:(r + 1) * tq]
        imp_t = jnp.where(forced, BIG, jnp.where(future, -BIG, _dot_f32(ovt_ref[...], p_sum)))
        sel_t = _topk_membership_t(imp_t, n_top)
        sel = jnp.concatenate([sel_t, jnp.zeros((LANES - n_slc, tq), F32)], axis=0).T
        penalty = jnp.where(sel > 0.5, 0.0, NEG_INF).astype(BF16)
        qx_ref[g] = jnp.concatenate([q_of(g), jnp.concatenate([penalty] * R, axis=0)], axis=1)

    m_ref[...] = jnp.full(m_ref.shape, NEG_INF, F32)
    l_ref[...] = jnp.zeros(l_ref.shape, F32)
    acc_ref[...] = jnp.zeros(acc_ref.shape, F32)
    blocks_per_tile = tk // SEL_LEN

    def sweep_tile(kt, diagonal):
        k0 = pl.multiple_of(kt * tk, tk)
        key_blk = kt * blocks_per_tile + jnp.right_shift(lax.broadcasted_iota(jnp.int32, (tk, LANES), 0), SEL_SHIFT)
        onehot = jnp.where(lax.broadcasted_iota(jnp.int32, (tk, LANES), 1) == key_blk, 1.0, 0.0).astype(BF16)
        s_all = []
        for g in range(G):
            k_ext = jnp.concatenate([ks_ref[0, g, pl.ds(k0, tk), :], onehot], axis=1)
            s = _dot_nt(k_ext, qx_ref[g])
            if diagonal:
                causal = k0 + lax.broadcasted_iota(jnp.int32, (tk, tq), 0) <= pos
                s = s + per_head(jnp.where(causal, 0.0, NEG_INF))
            s_all.append(s)
        for g, s in enumerate(s_all):
            m_old = m_ref[g]
            m_new = jnp.maximum(m_old, jnp.max(s, axis=0, keepdims=True))
            p = jnp.exp2(s - m_new)
            alpha = jnp.exp2(m_old - m_new)
            l_ref[g] = alpha * l_ref[g] + jnp.sum(p, axis=0, keepdims=True)
            acc_ref[g] = alpha * acc_ref[g] + _dot_tn(vs_ref[0, g, pl.ds(k0, tk), :], p.astype(BF16))
            m_ref[g] = m_new

    def sweep(kt, carry):
        sweep_tile(kt, diagonal=False)
        return carry

    last_tile = t0 // tk
    lax.fori_loop(0, last_tile, sweep, 0)
    sweep_tile(last_tile, diagonal=True)

    gates_t = gate_ref[...].T
    for g in range(G):
        os_t = acc_ref[g] * (1.0 / l_ref[g])
        oc_t, ow_t = oc_ref[g], ow_ref[g]
        for r in range(R):
            cols = slice(r * tq, (r + 1) * tq)
            hd = g * R + r
            merged_t = (gates_t[hd:hd + 1, :] * oc_t[:, cols]
                        + gates_t[C_HEADS + hd:C_HEADS + hd + 1, :] * os_t[:, cols]
                        + gates_t[2 * C_HEADS + hd:2 * C_HEADS + hd + 1, :] * ow_t[:, cols])
            y_ref[:, (g * R + r) * dh:(g * R + r + 1) * dh] = merged_t.T.astype(BF16)


def _nsa_attn(qkv, cmp_kv, gates, overlap_t, *, tq=128, tk=512):
    B_, _, S, dh = qkv.shape
    nq = S // tq
    n_slc = S // SEL_LEN
    n_top = min(SEL_TOPK, n_slc)
    assert n_top >= 3 and n_slc % SUBLANES == 0
    tk = min(tk, S)
    assert tq % LANES == 0 and tk % tq == 0 and n_slc <= LANES
    n_cmp_rows = cmp_kv.shape[3]
    G, R = C_KV, C_REP
    seq = lambda base: pl.BlockSpec((1, G, S, dh), lambda b, i: (b, base // G, 0, 0), pipeline_mode=pl.Buffered(1))
    return pl.pallas_call(
        functools.partial(_nsa_attn_kernel, tq=tq, tk=tk, n_slc=n_slc, n_top=n_top),
        grid=(B_, nq),
        in_specs=[
            pl.BlockSpec((1, C_HEADS, tq, dh), lambda b, i: (b, 0, i, 0)),
            pl.BlockSpec((1, 1, G, n_cmp_rows, dh), lambda b, i: (b, 0, 0, 0, 0)),
            pl.BlockSpec((1, 1, G, n_cmp_rows, dh), lambda b, i: (b, 1, 0, 0, 0)),
            seq(HM_KSEL), seq(HM_VSEL), seq(HM_KWIN), seq(HM_VWIN),
            pl.BlockSpec((tq, LANES), lambda b, i: (b * nq + i, 0)),
            pl.BlockSpec((n_slc, n_cmp_rows), lambda b, i: (0, 0)),
        ],
        out_specs=pl.BlockSpec((tq, C_HEADS * dh), lambda b, i: (b * nq + i, 0)),
        out_shape=jax.ShapeDtypeStruct((B_ * S, C_HEADS * dh), BF16),
        scratch_shapes=[
            pltpu.VMEM((G, R * tq, dh + LANES), BF16),
            pltpu.VMEM((G, 1, R * tq), F32),
            pltpu.VMEM((G, 1, R * tq), F32),
            pltpu.VMEM((G, dh, R * tq), F32),
            pltpu.VMEM((G, dh, R * tq), F32),
            pltpu.VMEM((G, dh, R * tq), F32),
        ],
        compiler_params=_params("parallel", "arbitrary"),
        name="nsa_attn",
    )(qkv, cmp_kv, cmp_kv, qkv, qkv, qkv, qkv, gates, overlap_t)


def _rope_tables(S):
    half = ROPE_DIMS // 2
    inv = 1.0 / (ROPE_THETA ** (jnp.arange(half, dtype=F32) / half))
    ang = jnp.arange(S, dtype=F32)[:, None] * inv[None, :]
    cos, sin = jnp.cos(ang), jnp.sin(ang)
    rest = LANES - ROPE_DIMS
    ca = jnp.concatenate([cos, cos, jnp.ones((S, rest), F32)], axis=-1)
    sn = jnp.concatenate([-sin, sin, jnp.zeros((S, rest), F32)], axis=-1)
    q_scale = C_HEAD_DIM ** -0.5 * LOG2E
    return jnp.stack([ca * q_scale, ca]), jnp.stack([sn * q_scale, sn])


def _overlap_matrix(n_cmp_rows, n_slc):
    ci = jnp.arange(n_cmp_rows)[:, None] * CMP_STRIDE
    sj = jnp.arange(n_slc)[None, :] * SEL_LEN
    ov = jnp.clip(jnp.minimum(ci + CMP_LEN, sj + SEL_LEN) - jnp.maximum(ci, sj), 0).astype(F32) / CMP_LEN
    return ov


def _gmlp_mlstm_layer(h, mix_g, w_in, gate_b, g_norm, g_ws, g_bs, conv_w, m_norm, w_out, batch):
    w_main = w_in.astype(BF16)
    w_gate = jnp.pad(w_in[:, AB_MAIN:], ((0, 0), (0, LANES - 2 * B_HEADS))).astype(BF16)
    p, gates = _ab_proj(h, mix_g.reshape(1, D_MODEL), w_main, w_gate)
    y_a = _gmlp(p, g_norm.reshape(1, A_WIDTH), g_ws, g_bs.T)
    gb = jnp.pad(gate_b.reshape(1, 2 * B_HEADS), ((0, 0), (0, LANES - 2 * B_HEADS)))
    y_b = _mlstm(p, gates, gb, conv_w, m_norm.reshape(1, B_WIDTH), batch)
    return _outproj(h, [y_a, y_b], w_out.astype(BF16))


def _nsa_layer(h, mix_g, w_in, cmp_pos, cmp_w1, cmp_w2, w_out, batch):
    T = h.shape[0]
    S = T // batch
    w_main = w_in.astype(BF16)
    wg = jnp.pad(w_in[:, C_MAIN:], ((0, 0), (0, LANES - 3 * C_HEADS))).astype(BF16)
    ca, sn = _rope_tables(S)
    qkv, gates = _nsa_proj(h, mix_g.reshape(1, D_MODEL), w_main, wg, ca, sn, batch)
    cmp_kv = _compress(qkv, cmp_pos, cmp_w1.reshape(2, CMP_LEN, C_HEAD_DIM, C_HEAD_DIM).astype(BF16),
                       cmp_w2.astype(BF16))
    overlap_t = _overlap_matrix(S // CMP_STRIDE, S // SEL_LEN).T
    y = _nsa_attn(qkv, cmp_kv, gates, overlap_t)
    return _outproj(h, [y], w_out.astype(BF16))


def kernel(x, ffn_norm, ffn_w_gate, ffn_w_up, ffn_w_down, mix_norm, ab_w_in, mlstm_gate_bias, gmlp_norm, gmlp_w_s,
           gmlp_b_s, mlstm_conv, mlstm_norm, ab_w_out, nsa_w_in, nsa_cmp_pos, nsa_cmp_w1, nsa_cmp_w2, nsa_w_out,
           final_norm):
    batch, S, _ = x.shape
    depth = ffn_norm.shape[0]
    h = x.reshape(batch * S, D_MODEL)
    fg = final_norm.reshape(1, D_MODEL)

    wg, wu, wd = ffn_w_gate.astype(BF16), ffn_w_up.astype(BF16), ffn_w_down.astype(BF16)

    def ffn(h, layer, half, final=False):
        return _ffn(h, ffn_norm[layer, half].reshape(1, D_MODEL), wg, wu, wd, fg, layer, half, final=final)

    for layer in range(depth):
        j = layer // 2
        h = ffn(h, layer, 0)
        if layer % 2 == 0:
            h = _gmlp_mlstm_layer(h, mix_norm[layer], ab_w_in[j], mlstm_gate_bias[j], gmlp_norm[j], gmlp_w_s[j],
                                  gmlp_b_s[j], mlstm_conv[j], mlstm_norm[j], ab_w_out[j], batch)
        else:
            h = _nsa_layer(h, mix_norm[layer], nsa_w_in[j], nsa_cmp_pos[j], nsa_cmp_w1[j], nsa_cmp_w2[j],
                           nsa_w_out[j], batch)
        h = ffn(h, layer, 1, final=(layer == depth - 1))
    return h.reshape(batch, S, D_MODEL)
```

```python
import functools
import math

import jax
import jax.numpy as jnp
from jax import lax
from jax.experimental import pallas as pl
from jax.experimental.pallas import tpu as pltpu

F32 = jnp.float32
BF16 = jnp.bfloat16

D_MODEL = 2048
D_FF = 5632
EPS = 1e-6

A_WIDTH = D_MODEL // 2
A_CHUNK = 128
A_GROUPS = 8
A_GROUP_DIM = A_WIDTH // A_GROUPS

B_HEADS = 4
B_WIDTH = D_MODEL // 2
B_HEAD_DIM = B_WIDTH // B_HEADS
B_CHUNK = 128
B_CONV = 4
AB_MAIN = 2 * A_WIDTH + 4 * B_WIDTH

C_HEADS = 16
C_KV = 4
C_REP = C_HEADS // C_KV
C_HEAD_DIM = D_MODEL // C_HEADS
CMP_LEN = 32
CMP_STRIDE = 16
SEL_LEN = 64
SEL_SHIFT = SEL_LEN.bit_length() - 1
SEL_TOPK = 16
WINDOW = 512
C_MAIN = C_HEADS * C_HEAD_DIM + 6 * C_KV * C_HEAD_DIM
ROPE_THETA = 500000.0
ROPE_DIMS = C_HEAD_DIM // 4
NEG_INF = -1e30
BIG = 1e9
LOG2E = math.log2(math.e)

LANES = 128
SUBLANES = 8
VMEM_LIMIT = 60 * 1024 * 1024

HM_HEADS = C_HEADS + 6 * C_KV
HM_KCMP, HM_VCMP, HM_KSEL, HM_VSEL, HM_KWIN, HM_VWIN = (C_HEADS + i * C_KV for i in range(6))


def _params(*sem):
    return pltpu.CompilerParams(dimension_semantics=sem, vmem_limit_bytes=VMEM_LIMIT)


def _rms(x, g):
    return x * lax.rsqrt(jnp.mean(x * x, axis=-1, keepdims=True) + EPS) * g


def _dot(a, b):
    return jnp.dot(a, b, preferred_element_type=F32)


def _dot_nt(a, b):
    return lax.dot_general(a, b, (((1,), (1,)), ((), ())), preferred_element_type=F32)


def _dot_tn(a, b):
    return lax.dot_general(a, b, (((0,), (0,)), ((), ())), preferred_element_type=F32)


def _dot_f32(a, b):
    return jnp.dot(a, b, preferred_element_type=F32, precision=lax.Precision.HIGHEST)


def _ffn_kernel(h_ref, g_ref, wg_ref, wu_ref, wd_ref, fg_ref, o_ref, n_ref, *, final):
    j = pl.program_id(1)

    @pl.when(j == 0)
    def _():
        x = h_ref[...]
        n_ref[...] = _rms(x, g_ref[...]).astype(BF16)
        o_ref[...] = x

    n = n_ref[...]
    a = _dot(n, wg_ref[...])
    b = _dot(n, wu_ref[...])
    t = (a * jax.nn.sigmoid(a) * (0.5 * b)).astype(BF16)
    o_ref[...] += _dot(t, wd_ref[...])

    if final:
        @pl.when(j == pl.num_programs(1) - 1)
        def _():
            o_ref[...] = _rms(o_ref[...], fg_ref[...])


def _ffn(h, g, wg, wu, wd, fg, layer, half, *, final, tm=1024, tf=512):
    T = h.shape[0]
    return pl.pallas_call(
        functools.partial(_ffn_kernel, final=final),
        grid=(T // tm, D_FF // tf),
        in_specs=[
            pl.BlockSpec((tm, D_MODEL), lambda i, j: (i, 0)),
            pl.BlockSpec((1, D_MODEL), lambda i, j: (0, 0)),
            pl.BlockSpec((None, None, D_MODEL, tf), lambda i, j: (layer, half, 0, j)),
            pl.BlockSpec((None, None, D_MODEL, tf), lambda i, j: (layer, half, 0, j)),
            pl.BlockSpec((None, None, tf, D_MODEL), lambda i, j: (layer, half, j, 0)),
            pl.BlockSpec((1, D_MODEL), lambda i, j: (0, 0)),
        ],
        out_specs=pl.BlockSpec((tm, D_MODEL), lambda i, j: (i, 0)),
        out_shape=jax.ShapeDtypeStruct((T, D_MODEL), F32),
        scratch_shapes=[pltpu.VMEM((tm, D_MODEL), BF16)],
        compiler_params=_params("parallel", "arbitrary"),
        name="ffn_final" if final else "ffn",
    )(h, g, wg, wu, wd, fg)


def _ab_proj_kernel(h_ref, g_ref, w_ref, wgate_ref, o_ref, og_ref, n_ref):
    j = pl.program_id(1)

    @pl.when(j == 0)
    def _():
        n = _rms(h_ref[...], g_ref[...]).astype(BF16)
        n_ref[...] = n
        og_ref[...] = _dot(n, wgate_ref[...])

    o_ref[...] = _dot(n_ref[...], w_ref[...])


def _ab_proj(h, g, w, wgate, *, tm=1024, tn=1536):
    T = h.shape[0]
    return pl.pallas_call(
        _ab_proj_kernel,
        grid=(T // tm, AB_MAIN // tn),
        in_specs=[
            pl.BlockSpec((tm, D_MODEL), lambda i, j: (i, 0)),
            pl.BlockSpec((1, D_MODEL), lambda i, j: (0, 0)),
            pl.BlockSpec((D_MODEL, tn), lambda i, j: (0, j)),
            pl.BlockSpec((D_MODEL, LANES), lambda i, j: (0, 0)),
        ],
        out_specs=[
            pl.BlockSpec((tm, tn), lambda i, j: (i, j)),
            pl.BlockSpec((tm, LANES), lambda i, j: (i, 0)),
        ],
        out_shape=[
            jax.ShapeDtypeStruct((T, AB_MAIN), F32),
            jax.ShapeDtypeStruct((T, LANES), F32),
        ],
        scratch_shapes=[pltpu.VMEM((tm, D_MODEL), BF16)],
        compiler_params=_params("parallel", "arbitrary"),
        name="ab_proj",
    )(h, g, w, wgate)


def _gmlp_kernel(u_ref, v_ref, g_ref, ws_ref, bs_ref, o_ref):
    zu = jax.nn.gelu(u_ref[...])
    vn = _rms(jax.nn.gelu(v_ref[...]), g_ref[...]).astype(BF16)
    row = lax.broadcasted_iota(jnp.int32, (A_CHUNK, A_CHUNK), 0)
    col = lax.broadcasted_iota(jnp.int32, (A_CHUNK, A_CHUNK), 1)
    causal = col <= row
    bs = bs_ref[...]
    for gi in range(A_GROUPS):
        sl = slice(gi * A_GROUP_DIM, (gi + 1) * A_GROUP_DIM)
        w = jnp.where(causal, ws_ref[gi], 0.0).astype(BF16)
        for c in range(u_ref.shape[0] // A_CHUNK):
            rows = slice(c * A_CHUNK, (c + 1) * A_CHUNK)
            sv = _dot(w, vn[rows, sl]) + bs[:, gi:gi + 1]
            o_ref[rows, sl] = (zu[rows, sl] * sv).astype(BF16)


def _gmlp(p, g, ws, bs_t, *, chunks_per_step=8):
    T = p.shape[0]
    tm = chunks_per_step * A_CHUNK
    return pl.pallas_call(
        _gmlp_kernel,
        grid=(T // tm,),
        in_specs=[
            pl.BlockSpec((tm, A_WIDTH), lambda c: (c, 0)),
            pl.BlockSpec((tm, A_WIDTH), lambda c: (c, 1)),
            pl.BlockSpec((1, A_WIDTH), lambda c: (0, 0)),
            pl.BlockSpec((A_GROUPS, A_CHUNK, A_CHUNK), lambda c: (0, 0, 0)),
            pl.BlockSpec((A_CHUNK, A_GROUPS), lambda c: (0, 0)),
        ],
        out_specs=pl.BlockSpec((tm, A_WIDTH), lambda c: (c, 0)),
        out_shape=jax.ShapeDtypeStruct((T, A_WIDTH), BF16),
        compiler_params=_params("parallel"),
        name="gmlp",
    )(p, p, g, ws, bs_t)


def _log_sigmoid(x):
    return jnp.minimum(x, 0.0) - jnp.log1p(jnp.exp(-jnp.abs(x)))


def _mlstm_kernel(qk_ref, v_ref, o_ref, gate_ref, gb_ref, cw_ref, mn_ref, y_ref,
                  xx_ref, c_ref, n_ref, m_ref):
    @pl.when(pl.program_id(1) == 0)
    def _():
        xx_ref[0:SUBLANES, :] = jnp.zeros((SUBLANES, 2 * B_WIDTH), F32)
        c_ref[...] = jnp.zeros_like(c_ref)
        n_ref[...] = jnp.zeros_like(n_ref)
        m_ref[...] = jnp.zeros_like(m_ref)

    for sub in range(qk_ref.shape[0] // B_CHUNK):
        rows = pl.ds(sub * B_CHUNK, B_CHUNK)
        _mlstm_chunk(qk_ref.at[rows], v_ref.at[rows], o_ref.at[rows], gate_ref.at[rows], gb_ref, cw_ref, mn_ref,
                     y_ref.at[rows], xx_ref, c_ref, n_ref, m_ref)


def _mlstm_chunk(qk_ref, v_ref, o_ref, gate_ref, gb_ref, cw_ref, mn_ref, y_ref, xx_ref, c_ref, n_ref, m_ref):
    L, d = B_CHUNK, B_HEAD_DIM
    tail = SUBLANES

    xx_ref[tail:tail + L, :] = qk_ref[...]
    cw = cw_ref[...]
    conv = jnp.zeros((L, 2 * B_WIDTH), F32)
    for kk in range(B_CONV):
        off = tail - (B_CONV - 1) + kk
        conv = conv + cw[kk:kk + 1, :] * xx_ref[off:off + L, :]
    carry_rows = xx_ref[L:L + tail, :]
    xx_ref[0:tail, :] = carry_rows
    qk = conv * jax.nn.sigmoid(conv)

    gcol = gate_ref[...] + gb_ref[...]
    grow = gcol.T
    r_i = lax.broadcasted_iota(jnp.int32, (L, L), 0)
    c_i = lax.broadcasted_iota(jnp.int32, (L, L), 1)
    causal = c_i <= r_i
    tril = causal.astype(F32)
    triu = (r_i <= c_i).astype(F32)
    bcol_all = _dot_f32(tril, _log_sigmoid(gcol))
    brow_all = _dot_f32(_log_sigmoid(grow), triu)

    for hh in range(B_HEADS):
        q = qk[:, hh * d:(hh + 1) * d]
        k = qk[:, B_WIDTH + hh * d:B_WIDTH + (hh + 1) * d] * (d ** -0.5)
        v = v_ref[:, hh * d:(hh + 1) * d]
        qb, kb, vb = q.astype(BF16), k.astype(BF16), v.astype(BF16)
        i_col = gcol[:, hh:hh + 1]
        i_row = grow[hh:hh + 1, :]
        b_col = bcol_all[:, B_HEADS + hh:B_HEADS + hh + 1]
        b_row = brow_all[B_HEADS + hh:B_HEADS + hh + 1, :]
        b_last = b_col[L - 1:L, :]
        m_old = m_ref[hh:hh + 1, 0:1]

        logd = jnp.where(causal, b_col - b_row + i_row, NEG_INF)
        inter = b_col + m_old
        m_t = jnp.maximum(jnp.max(logd, axis=-1, keepdims=True), inter)
        dm = jnp.where(causal, jnp.exp(logd - m_t), 0.0)
        w_inter = jnp.exp(inter - m_t)
        s = _dot_nt(qb, kb) * dm
        ct = c_ref[hh]
        nvec = n_ref[hh:hh + 1, :]
        num = _dot(s.astype(BF16), vb) + w_inter * _dot(qb, ct.astype(BF16))
        den = jnp.sum(s, axis=-1, keepdims=True) + w_inter * jnp.sum(q * nvec, axis=-1, keepdims=True)
        hout = num / jnp.maximum(jnp.abs(den), jnp.exp(-m_t))

        m_new = m_t[L - 1:L, :]
        w_s = jnp.exp(b_last - b_col + i_col - m_new)
        w_prev = jnp.exp(b_last + m_old - m_new)
        c_ref[hh] = w_prev * ct + _dot_tn(kb, (w_s * v).astype(BF16))
        n_ref[hh:hh + 1, :] = w_prev * nvec + jnp.sum(w_s * k, axis=0, keepdims=True)
        m_ref[hh:hh + 1, :] = jnp.broadcast_to(m_new, (1, LANES))

        hn = _rms(hout, mn_ref[:, hh * d:(hh + 1) * d])
        og = jax.nn.sigmoid(o_ref[:, hh * d:(hh + 1) * d])
        y_ref[:, hh * d:(hh + 1) * d] = (og * hn).astype(BF16)


def _mlstm(p, gates, gate_b, conv_w, m_norm, batch, *, chunks_per_step=4):
    T = p.shape[0]
    L = chunks_per_step * B_CHUNK
    nc = T // batch // L
    qk_blk = (2 * A_WIDTH) // (2 * B_WIDTH)
    v_blk = (2 * A_WIDTH + 2 * B_WIDTH) // B_WIDTH
    row = lambda b, c: b * nc + c
    return pl.pallas_call(
        _mlstm_kernel,
        grid=(batch, nc),
        in_specs=[
            pl.BlockSpec((L, 2 * B_WIDTH), lambda b, c: (row(b, c), qk_blk)),
            pl.BlockSpec((L, B_WIDTH), lambda b, c: (row(b, c), v_blk)),
            pl.BlockSpec((L, B_WIDTH), lambda b, c: (row(b, c), v_blk + 1)),
            pl.BlockSpec((L, LANES), lambda b, c: (row(b, c), 0)),
            pl.BlockSpec((1, LANES), lambda b, c: (0, 0)),
            pl.BlockSpec((B_CONV, 2 * B_WIDTH), lambda b, c: (0, 0)),
            pl.BlockSpec((1, B_WIDTH), lambda b, c: (0, 0)),
        ],
        out_specs=pl.BlockSpec((L, B_WIDTH), lambda b, c: (row(b, c), 0)),
        out_shape=jax.ShapeDtypeStruct((T, B_WIDTH), BF16),
        scratch_shapes=[
            pltpu.VMEM((B_CHUNK + SUBLANES, 2 * B_WIDTH), F32),
            pltpu.VMEM((B_HEADS, B_HEAD_DIM, B_HEAD_DIM), F32),
            pltpu.VMEM((8, B_HEAD_DIM), F32),
            pltpu.VMEM((8, LANES), F32),
        ],
        compiler_params=_params("arbitrary", "arbitrary"),
        name="mlstm",
    )(p, p, p, gates, gate_b, conv_w, m_norm)


def _outproj_kernel(*refs, n_lhs):
    h_ref = refs[0]
    o_ref = refs[1 + 2 * n_lhs]
    acc = h_ref[...]
    for i in range(n_lhs):
        acc = acc + _dot(refs[1 + i][...], refs[1 + n_lhs + i][...])
    o_ref[...] = acc


def _outproj(h, lhs, w, *, tm=512):
    T = h.shape[0]
    n = len(lhs)
    width = lhs[0].shape[1]
    assert all(a.shape[1] == width for a in lhs) and w.shape[0] == n * width
    in_specs = [pl.BlockSpec((tm, D_MODEL), lambda i: (i, 0))]
    in_specs += [pl.BlockSpec((tm, width), lambda i: (i, 0)) for _ in lhs]
    in_specs += [pl.BlockSpec((width, D_MODEL), functools.partial(lambda i, k: (k, 0), k=k)) for k in range(n)]
    return pl.pallas_call(
        functools.partial(_outproj_kernel, n_lhs=n),
        grid=(T // tm,),
        in_specs=in_specs,
        out_specs=pl.BlockSpec((tm, D_MODEL), lambda i: (i, 0)),
        out_shape=jax.ShapeDtypeStruct((T, D_MODEL), F32),
        compiler_params=_params("parallel"),
        name="outproj",
    )(h, *lhs, *([w] * n))


def _rope(x, ca, sn):
    half = ROPE_DIMS // 2
    lane = lax.broadcasted_iota(jnp.int32, x.shape, 1)
    partner = jnp.where(lane < half, lane + half, jnp.where(lane < ROPE_DIMS, lane - half, lane))
    return x * ca + jnp.take_along_axis(x, partner, axis=1) * sn


def _nsa_proj_kernel(h_ref, g_ref, w_ref, wgate_ref, ca_ref, sn_ref, o_ref, og_ref, n_ref):
    j = pl.program_id(1)

    @pl.when(j == 0)
    def _():
        n = _rms(h_ref[...], g_ref[...]).astype(BF16)
        n_ref[...] = n
        og_ref[...] = jax.nn.sigmoid(_dot(n, wgate_ref[...]))

    res = _dot(n_ref[...], w_ref[...])
    heads = o_ref.shape[1]
    is_q = j < C_HEADS // heads

    def head(k):
        return res[:, k * LANES:(k + 1) * LANES]

    @pl.when(is_q)
    def _():
        ca, sn = ca_ref[0], sn_ref[0]
        for k in range(heads):
            o_ref[0, k] = _rope(head(k), ca, sn).astype(BF16)

    @pl.when(jnp.logical_not(is_q))
    def _():
        ca, sn = ca_ref[0], sn_ref[0]
        for k in range(C_KV):
            o_ref[0, k] = _rope(head(k), ca, sn).astype(BF16)
            o_ref[0, C_KV + k] = head(C_KV + k).astype(BF16)


def _nsa_proj(h, g, w, wgate, ca, sn, batch, *, tm=1024):
    T = h.shape[0]
    S = T // batch
    nts = S // tm
    heads = 2 * C_KV
    tn = heads * C_HEAD_DIM
    n_q = C_HEADS // heads
    table = pl.BlockSpec((1, tm, LANES), lambda i, j: (jnp.where(j < n_q, 0, 1), i % nts, 0))
    return pl.pallas_call(
        _nsa_proj_kernel,
        grid=(T // tm, C_MAIN // tn),
        in_specs=[
            pl.BlockSpec((tm, D_MODEL), lambda i, j: (i, 0)),
            pl.BlockSpec((1, D_MODEL), lambda i, j: (0, 0)),
            pl.BlockSpec((D_MODEL, tn), lambda i, j: (0, j)),
            pl.BlockSpec((D_MODEL, LANES), lambda i, j: (0, 0)),
            table, table,
        ],
        out_specs=[
            pl.BlockSpec((1, heads, tm, C_HEAD_DIM), lambda i, j: (i // nts, j, i % nts, 0)),
            pl.BlockSpec((tm, LANES), lambda i, j: (i, 0)),
        ],
        out_shape=[
            jax.ShapeDtypeStruct((batch, HM_HEADS, S, C_HEAD_DIM), BF16),
            jax.ShapeDtypeStruct((T, LANES), F32),
        ],
        scratch_shapes=[pltpu.VMEM((tm, D_MODEL), BF16)],
        compiler_params=_params("parallel", "arbitrary"),
        name="nsa_proj",
    )(h, g, w, wgate, ca, sn)


def _compress_kernel(x_ref, pos_ref, w1_ref, w2_ref, o_ref, xs_ref):
    xs_ref[...] = x_ref[0, 0].astype(F32)
    n_rows = xs_ref.shape[0] // CMP_STRIDE
    dh = xs_ref.shape[1]
    ya = jnp.zeros((n_rows, dh), F32)
    yb = jnp.zeros((n_rows, dh), F32)
    for l in range(CMP_STRIDE):
        xl = xs_ref[pl.ds(l, n_rows, stride=CMP_STRIDE), :]
        ya = ya + _dot((xl + pos_ref[0, l:l + 1, :]).astype(BF16), w1_ref[0, l])
        lb = CMP_STRIDE + l
        yb = yb + _dot((xl + pos_ref[0, lb:lb + 1, :]).astype(BF16), w1_ref[0, lb])
    pre = ya + pltpu.roll(yb, n_rows - 1, 0)
    out = _dot(jax.nn.gelu(pre).astype(BF16), w2_ref[0])
    rows = lax.broadcasted_iota(jnp.int32, out.shape, 0)
    o_ref[0, 0, 0] = jnp.where(rows < n_rows - 1, out, 0.0).astype(BF16)


def _compress(qkv, pos, w1, w2):
    B_, _, S, dh = qkv.shape
    rows = S // CMP_STRIDE
    return pl.pallas_call(
        _compress_kernel,
        grid=(B_, 2, C_KV),
        in_specs=[
            pl.BlockSpec((1, 1, S, dh), lambda b, kv, g: (b, HM_KCMP + kv * C_KV + g, 0, 0)),
            pl.BlockSpec((1, CMP_LEN, dh), lambda b, kv, g: (kv, 0, 0)),
            pl.BlockSpec((1, CMP_LEN, dh, dh), lambda b, kv, g: (kv, 0, 0, 0)),
            pl.BlockSpec((1, dh, dh), lambda b, kv, g: (kv, 0, 0)),
        ],
        out_specs=pl.BlockSpec((1, 1, 1, rows, dh), lambda b, kv, g: (b, kv, g, 0, 0)),
        out_shape=jax.ShapeDtypeStruct((B_, 2, C_KV, rows, dh), BF16),
        scratch_shapes=[pltpu.VMEM((S, dh), F32)],
        compiler_params=_params("parallel", "parallel", "parallel"),
        name="nsa_compress",
    )(qkv, pos, w1, w2)


def _softmax_parts_t(s):
    m = jnp.max(s, axis=0, keepdims=True)
    p = jnp.exp2(s - m)
    return p, jnp.sum(p, axis=0, keepdims=True)


def _topk_membership_t(imp_t, n_top):
    n_blk, tq = imp_t.shape
    groups = n_blk // SUBLANES
    va = [imp_t[a * SUBLANES:(a + 1) * SUBLANES] for a in range(groups)]
    cnt = [jnp.zeros((SUBLANES, tq), F32) for _ in range(groups)]
    sub = lax.broadcasted_iota(jnp.int32, (SUBLANES, tq), 0)
    for i in range(n_blk):
        vi = jnp.broadcast_to(imp_t[i:i + 1, :], (SUBLANES, tq))
        for a in range(groups):
            if i < a * SUBLANES:
                ahead = jnp.where(vi >= va[a], 1.0, 0.0)
            elif i >= (a + 1) * SUBLANES:
                ahead = jnp.where(vi > va[a], 1.0, 0.0)
            else:
                ahead = jnp.where(sub > i - a * SUBLANES, jnp.where(vi >= va[a], 1.0, 0.0),
                                  jnp.where(vi > va[a], 1.0, 0.0))
            cnt[a] = cnt[a] + ahead
    return jnp.concatenate([jnp.where(c < n_top, 1.0, 0.0) for c in cnt], axis=0)


def _nsa_attn_kernel(q_ref, kc_ref, vc_ref, ks_ref, vs_ref, kw_ref, vw_ref, gate_ref, ovt_ref, y_ref,
                     qx_ref, m_ref, l_ref, acc_ref, oc_ref, ow_ref, *, tq, tk, n_slc, n_top):
    G, R, dh = C_KV, C_REP, C_HEAD_DIM
    t0 = pl.program_id(1) * tq
    pos = t0 + lax.broadcasted_iota(jnp.int32, (1, tq), 1)

    def q_of(g):
        return q_ref[0, g * R:(g + 1) * R].reshape(R * tq, dh)

    def per_head(x):
        return jnp.concatenate([x] * R, axis=1)

    n_cmp_rows = kc_ref.shape[3]
    cmp_end = lax.broadcasted_iota(jnp.int32, (n_cmp_rows, tq), 0) * CMP_STRIDE + (CMP_LEN - 1)
    ok_c = jnp.logical_and(cmp_end <= pos, cmp_end < n_cmp_rows * CMP_STRIDE)
    bias_c = per_head(jnp.where(ok_c, 0.0, NEG_INF))
    has_c = per_head(pos >= CMP_LEN - 1)
    slab = WINDOW + tq
    w0 = pl.multiple_of(jnp.maximum(t0 - WINDOW, 0), tq)
    kpos_w = w0 + lax.broadcasted_iota(jnp.int32, (slab, tq), 0)
    ok_w = jnp.logical_and(kpos_w <= pos, pos - kpos_w < WINDOW)
    bias_w = per_head(jnp.where(ok_w, 0.0, NEG_INF))
    blk_t = lax.broadcasted_iota(jnp.int32, (n_slc, tq), 0)
    cur_t = jnp.right_shift(pos, SEL_SHIFT)
    forced = jnp.logical_or(blk_t == 0, jnp.logical_or(blk_t == cur_t, blk_t == cur_t - 1))
    future = blk_t > cur_t

    s_cmp = [_dot_nt(kc_ref[0, 0, g], q_of(g)) + bias_c for g in range(G)]
    s_win = [_dot_nt(kw_ref[0, g, pl.ds(w0, slab), :], q_of(g)) + bias_w for g in range(G)]
    for g in range(G):
        p_c, l_c = _softmax_parts_t(s_cmp[g])
        p_c = p_c * jnp.where(has_c, 1.0 / l_c, 0.0)
        oc_ref[g] = _dot_tn(vc_ref[0, 0, g], p_c.astype(BF16))

        p_w, l_w = _softmax_parts_t(s_win[g])
        ow_ref[g] = _dot_tn(vw_ref[0, g, pl.ds(w0, slab), :], p_w.astype(BF16)) * (1.0 / l_w)

        p_sum = p_c[:, 0:tq]
        for r in range(1, R):
            p_sum = p_sum + p_c[:, r * tq:(r + 1) * tq]
        imp_t = jnp.where(forced, BIG, jnp.where(future, -BIG, _dot_f32(ovt_ref[...], p_sum)))
        sel_t = _topk_membership_t(imp_t, n_top)
        sel = jnp.concatenate([sel_t, jnp.zeros((LANES - n_slc, tq), F32)], axis=0).T
        penalty = jnp.where(sel > 0.5, 0.0, NEG_INF).astype(BF16)
        qx_ref[g] = jnp.concatenate([q_of(g), jnp.concatenate([penalty] * R, axis=0)], axis=1)

    m_ref[...] = jnp.full(m_ref.shape, NEG_INF, F32)
    l_ref[...] = jnp.zeros(l_ref.shape, F32)
    acc_ref[...] = jnp.zeros(acc_ref.shape, F32)
    blocks_per_tile = tk // SEL_LEN

    def sweep_tile(kt, diagonal):
        k0 = pl.multiple_of(kt * tk, tk)
        key_blk = kt * blocks_per_tile + jnp.right_shift(lax.broadcasted_iota(jnp.int32, (tk, LANES), 0), SEL_SHIFT)
        onehot = jnp.where(lax.broadcasted_iota(jnp.int32, (tk, LANES), 1) == key_blk, 1.0, 0.0).astype(BF16)
        s_all = []
        for g in range(G):
            k_ext = jnp.concatenate([ks_ref[0, g, pl.ds(k0, tk), :], onehot], axis=1)
            s = _dot_nt(k_ext, qx_ref[g])
            if diagonal:
                causal = k0 + lax.broadcasted_iota(jnp.int32, (tk, tq), 0) <= pos
                s = s + per_head(jnp.where(causal, 0.0, NEG_INF))
            s_all.append(s)
        for g, s in enumerate(s_all):
            m_old = m_ref[g]
            m_new = jnp.maximum(m_old, jnp.max(s, axis=0, keepdims=True))
            p = jnp.exp2(s - m_new)
            alpha = jnp.exp2(m_old - m_new)
            l_ref[g] = alpha * l_ref[g] + jnp.sum(p, axis=0, keepdims=True)
            acc_ref[g] = alpha * acc_ref[g] + _dot_tn(vs_ref[0, g, pl.ds(k0, tk), :], p.astype(BF16))
            m_ref[g] = m_new

    def sweep(kt, carry):
        sweep_tile(kt, diagonal=False)
        return carry

    last_tile = t0 // tk
    lax.fori_loop(0, last_tile, sweep, 0)
    sweep_tile(last_tile, diagonal=True)

    gates_t = gate_ref[...].T
    for g in range(G):
        os_t = acc_ref[g] * (1.0 / l_ref[g])
        oc_t, ow_t = oc_ref[g], ow_ref[g]
        for r in range(R):
            cols = slice(r * tq, (r + 1) * tq)
            hd = g * R + r
            merged_t = (gates_t[hd:hd + 1, :] * oc_t[:, cols]
                        + gates_t[C_HEADS + hd:C_HEADS + hd + 1, :] * os_t[:, cols]
                        + gates_t[2 * C_HEADS + hd:2 * C_HEADS + hd + 1, :] * ow_t[:, cols])
            y_ref[:, (g * R + r) * dh:(g * R + r + 1) * dh] = merged_t.T.astype(BF16)


def _nsa_attn(qkv, cmp_kv, gates, overlap_t, *, tq=256, tk=512):
    B_, _, S, dh = qkv.shape
    nq = S // tq
    n_slc = S // SEL_LEN
    n_top = min(SEL_TOPK, n_slc)
    assert n_top >= 3 and n_slc % SUBLANES == 0
    tk = min(tk, S)
    assert tq % LANES == 0 and tk % tq == 0 and n_slc <= LANES
    n_cmp_rows = cmp_kv.shape[3]
    G, R = C_KV, C_REP
    seq = lambda base: pl.BlockSpec((1, G, S, dh), lambda b, i: (b, base // G, 0, 0), pipeline_mode=pl.Buffered(1))
    return pl.pallas_call(
        functools.partial(_nsa_attn_kernel, tq=tq, tk=tk, n_slc=n_slc, n_top=n_top),
        grid=(B_, nq),
        in_specs=[
            pl.BlockSpec((1, C_HEADS, tq, dh), lambda b, i: (b, 0, i, 0)),
            pl.BlockSpec((1, 1, G, n_cmp_rows, dh), lambda b, i: (b, 0, 0, 0, 0)),
            pl.BlockSpec((1, 1, G, n_cmp_rows, dh), lambda b, i: (b, 1, 0, 0, 0)),
            seq(HM_KSEL), seq(HM_VSEL), seq(HM_KWIN), seq(HM_VWIN),
            pl.BlockSpec((tq, LANES), lambda b, i: (b * nq + i, 0)),
            pl.BlockSpec((n_slc, n_cmp_rows), lambda b, i: (0, 0)),
        ],
        out_specs=pl.BlockSpec((tq, C_HEADS * dh), lambda b, i: (b * nq + i, 0)),
        out_shape=jax.ShapeDtypeStruct((B_ * S, C_HEADS * dh), BF16),
        scratch_shapes=[
            pltpu.VMEM((G, R * tq, dh + LANES), BF16),
            pltpu.VMEM((G, 1, R * tq), F32),
            pltpu.VMEM((G, 1, R * tq), F32),
            pltpu.VMEM((G, dh, R * tq), F32),
            pltpu.VMEM((G, dh, R * tq), F32),
            pltpu.VMEM((G, dh, R * tq), F32),
        ],
        compiler_params=_params("parallel", "arbitrary"),
        name="nsa_attn",
    )(qkv, cmp_kv, cmp_kv, qkv, qkv, qkv, qkv, gates, overlap_t)


def _rope_tables(S):
    half = ROPE_DIMS // 2
    inv = 1.0 / (ROPE_THETA ** (jnp.arange(half, dtype=F32) / half))
    ang = jnp.arange(S, dtype=F32)[:, None] * inv[None, :]
    cos, sin = jnp.cos(ang), jnp.sin(ang)
    rest = LANES - ROPE_DIMS
    ca = jnp.concatenate([cos, cos, jnp.ones((S, rest), F32)], axis=-1)
    sn = jnp.concatenate([-sin, sin, jnp.zeros((S, rest), F32)], axis=-1)
    q_scale = C_HEAD_DIM ** -0.5 * LOG2E
    return jnp.stack([ca * q_scale, ca]), jnp.stack([sn * q_scale, sn])


def _overlap_matrix(n_cmp_rows, n_slc):
    ci = jnp.arange(n_cmp_rows)[:, None] * CMP_STRIDE
    sj = jnp.arange(n_slc)[None, :] * SEL_LEN
    ov = jnp.clip(jnp.minimum(ci + CMP_LEN, sj + SEL_LEN) - jnp.maximum(ci, sj), 0).astype(F32) / CMP_LEN
    return ov


def _gmlp_mlstm_layer(h, mix_g, w_in, gate_b, g_norm, g_ws, g_bs, conv_w, m_norm, w_out, batch):
    w_main = w_in.astype(BF16)
    w_gate = jnp.pad(w_in[:, AB_MAIN:], ((0, 0), (0, LANES - 2 * B_HEADS))).astype(BF16)
    p, gates = _ab_proj(h, mix_g.reshape(1, D_MODEL), w_main, w_gate)
    y_a = _gmlp(p, g_norm.reshape(1, A_WIDTH), g_ws, g_bs.T)
    gb = jnp.pad(gate_b.reshape(1, 2 * B_HEADS), ((0, 0), (0, LANES - 2 * B_HEADS)))
    y_b = _mlstm(p, gates, gb, conv_w, m_norm.reshape(1, B_WIDTH), batch)
    return _outproj(h, [y_a, y_b], w_out.astype(BF16))


def _nsa_layer(h, mix_g, w_in, cmp_pos, cmp_w1, cmp_w2, w_out, batch):
    T = h.shape[0]
    S = T // batch
    w_main = w_in.astype(BF16)
    wg = jnp.pad(w_in[:, C_MAIN:], ((0, 0), (0, LANES - 3 * C_HEADS))).astype(BF16)
    ca, sn = _rope_tables(S)
    qkv, gates = _nsa_proj(h, mix_g.reshape(1, D_MODEL), w_main, wg, ca, sn, batch)
    cmp_kv = _compress(qkv, cmp_pos, cmp_w1.reshape(2, CMP_LEN, C_HEAD_DIM, C_HEAD_DIM).astype(BF16),
                       cmp_w2.astype(BF16))
    overlap_t = _overlap_matrix(S // CMP_STRIDE, S // SEL_LEN).T
    y = _nsa_attn(qkv, cmp_kv, gates, overlap_t)
    return _outproj(h, [y], w_out.astype(BF16))


def kernel(x, ffn_norm, ffn_w_gate, ffn_w_up, ffn_w_down, mix_norm, ab_w_in, mlstm_gate_bias, gmlp_norm, gmlp_w_s,
           gmlp_b_s, mlstm_conv, mlstm_norm, ab_w_out, nsa_w_in, nsa_cmp_pos, nsa_cmp_w1, nsa_cmp_w2, nsa_w_out,
           final_norm):
    batch, S, _ = x.shape
    depth = ffn_norm.shape[0]
    h = x.reshape(batch * S, D_MODEL)
    fg = final_norm.reshape(1, D_MODEL)

    wg, wu, wd = ffn_w_gate.astype(BF16), ffn_w_up.astype(BF16), ffn_w_down.astype(BF16)

    def ffn(h, layer, half, final=False):
        return _ffn(h, ffn_norm[layer, half].reshape(1, D_MODEL), wg, wu, wd, fg, layer, half, final=final)

    for layer in range(depth):
        j = layer // 2
        h = ffn(h, layer, 0)
        if layer % 2 == 0:
            h = _gmlp_mlstm_layer(h, mix_norm[layer], ab_w_in[j], mlstm_gate_bias[j], gmlp_norm[j], gmlp_w_s[j],
                                  gmlp_b_s[j], mlstm_conv[j], mlstm_norm[j], ab_w_out[j], batch)
        else:
            h = _nsa_layer(h, mix_norm[layer], nsa_w_in[j], nsa_cmp_pos[j], nsa_cmp_w1[j], nsa_cmp_w2[j],
                           nsa_w_out[j], batch)
        h = ffn(h, layer, 1, final=(layer == depth - 1))
    return h.reshape(batch, S, D_MODEL)
```

```python
import functools
import math

import jax
import jax.numpy as jnp
from jax import lax
from jax.experimental import pallas as pl
from jax.experimental.pallas import tpu as pltpu

F32 = jnp.float32
BF16 = jnp.bfloat16

D_MODEL = 2048
D_FF = 5632
EPS = 1e-6

A_WIDTH = D_MODEL // 2
A_CHUNK = 128
A_GROUPS = 8
A_GROUP_DIM = A_WIDTH // A_GROUPS

B_HEADS = 4
B_WIDTH = D_MODEL // 2
B_HEAD_DIM = B_WIDTH // B_HEADS
B_CHUNK = 128
B_CONV = 4
AB_MAIN = 2 * A_WIDTH + 4 * B_WIDTH

C_HEADS = 16
C_KV = 4
C_REP = C_HEADS // C_KV
C_HEAD_DIM = D_MODEL // C_HEADS
CMP_LEN = 32
CMP_STRIDE = 16
SEL_LEN = 64
SEL_SHIFT = SEL_LEN.bit_length() - 1
SEL_TOPK = 16
WINDOW = 512
C_MAIN = C_HEADS * C_HEAD_DIM + 6 * C_KV * C_HEAD_DIM
ROPE_THETA = 500000.0
ROPE_DIMS = C_HEAD_DIM // 4
NEG_INF = -1e30
BIG = 1e9
LOG2E = math.log2(math.e)

LANES = 128
SUBLANES = 8
VMEM_LIMIT = 60 * 1024 * 1024

HM_HEADS = C_HEADS + 6 * C_KV
HM_KCMP, HM_VCMP, HM_KSEL, HM_VSEL, HM_KWIN, HM_VWIN = (C_HEADS + i * C_KV for i in range(6))


def _params(*sem):
    return pltpu.CompilerParams(dimension_semantics=sem, vmem_limit_bytes=VMEM_LIMIT)


def _rms(x, g):
    return x * lax.rsqrt(jnp.mean(x * x, axis=-1, keepdims=True) + EPS) * g


def _dot(a, b):
    return jnp.dot(a, b, preferred_element_type=F32)


def _dot_nt(a, b):
    return lax.dot_general(a, b, (((1,), (1,)), ((), ())), preferred_element_type=F32)


def _dot_tn(a, b):
    return lax.dot_general(a, b, (((0,), (0,)), ((), ())), preferred_element_type=F32)


def _dot_f32(a, b):
    return jnp.dot(a, b, preferred_element_type=F32, precision=lax.Precision.HIGHEST)


def _ffn_kernel(h_ref, g_ref, wg_ref, wu_ref, wd_ref, fg_ref, o_ref, n_ref, *, final):
    j = pl.program_id(1)

    @pl.when(j == 0)
    def _():
        x = h_ref[...]
        n_ref[...] = _rms(x, g_ref[...]).astype(BF16)
        o_ref[...] = x

    n = n_ref[...]
    a = _dot(n, wg_ref[...])
    b = _dot(n, wu_ref[...])
    t = (a * jax.nn.sigmoid(a) * (0.5 * b)).astype(BF16)
    o_ref[...] += _dot(t, wd_ref[...])

    if final:
        @pl.when(j == pl.num_programs(1) - 1)
        def _():
            o_ref[...] = _rms(o_ref[...], fg_ref[...])


def _ffn(h, g, wg, wu, wd, fg, layer, half, *, final, tm=1024, tf=512):
    T = h.shape[0]
    return pl.pallas_call(
        functools.partial(_ffn_kernel, final=final),
        grid=(T // tm, D_FF // tf),
        in_specs=[
            pl.BlockSpec((tm, D_MODEL), lambda i, j: (i, 0)),
            pl.BlockSpec((1, D_MODEL), lambda i, j: (0, 0)),
            pl.BlockSpec((None, None, D_MODEL, tf), lambda i, j: (layer, half, 0, j)),
            pl.BlockSpec((None, None, D_MODEL, tf), lambda i, j: (layer, half, 0, j)),
            pl.BlockSpec((None, None, tf, D_MODEL), lambda i, j: (layer, half, j, 0)),
            pl.BlockSpec((1, D_MODEL), lambda i, j: (0, 0)),
        ],
        out_specs=pl.BlockSpec((tm, D_MODEL), lambda i, j: (i, 0)),
        out_shape=jax.ShapeDtypeStruct((T, D_MODEL), F32),
        scratch_shapes=[pltpu.VMEM((tm, D_MODEL), BF16)],
        compiler_params=_params("parallel", "arbitrary"),
        name="ffn_final" if final else "ffn",
    )(h, g, wg, wu, wd, fg)


def _ab_proj_kernel(h_ref, g_ref, w_ref, wgate_ref, o_ref, og_ref, n_ref):
    j = pl.program_id(1)

    @pl.when(j == 0)
    def _():
        n = _rms(h_ref[...], g_ref[...]).astype(BF16)
        n_ref[...] = n
        og_ref[...] = _dot(n, wgate_ref[...])

    o_ref[...] = _dot(n_ref[...], w_ref[...])


def _ab_proj(h, g, w, wgate, *, tm=1024, tn=2048):
    T = h.shape[0]
    return pl.pallas_call(
        _ab_proj_kernel,
        grid=(T // tm, AB_MAIN // tn),
        in_specs=[
            pl.BlockSpec((tm, D_MODEL), lambda i, j: (i, 0)),
            pl.BlockSpec((1, D_MODEL), lambda i, j: (0, 0)),
            pl.BlockSpec((D_MODEL, tn), lambda i, j: (0, j)),
            pl.BlockSpec((D_MODEL, LANES), lambda i, j: (0, 0)),
        ],
        out_specs=[
            pl.BlockSpec((tm, tn), lambda i, j: (i, j)),
            pl.BlockSpec((tm, LANES), lambda i, j: (i, 0)),
        ],
        out_shape=[
            jax.ShapeDtypeStruct((T, AB_MAIN), F32),
            jax.ShapeDtypeStruct((T, LANES), F32),
        ],
        scratch_shapes=[pltpu.VMEM((tm, D_MODEL), BF16)],
        compiler_params=_params("parallel", "arbitrary"),
        name="ab_proj",
    )(h, g, w, wgate)


def _gmlp_kernel(u_ref, v_ref, g_ref, ws_ref, bs_ref, o_ref):
    zu = jax.nn.gelu(u_ref[...])
    vn = _rms(jax.nn.gelu(v_ref[...]), g_ref[...]).astype(BF16)
    row = lax.broadcasted_iota(jnp.int32, (A_CHUNK, A_CHUNK), 0)
    col = lax.broadcasted_iota(jnp.int32, (A_CHUNK, A_CHUNK), 1)
    causal = col <= row
    bs = bs_ref[...]
    for gi in range(A_GROUPS):
        sl = slice(gi * A_GROUP_DIM, (gi + 1) * A_GROUP_DIM)
        w = jnp.where(causal, ws_ref[gi], 0.0).astype(BF16)
        for c in range(u_ref.shape[0] // A_CHUNK):
            rows = slice(c * A_CHUNK, (c + 1) * A_CHUNK)
            sv = _dot(w, vn[rows, sl]) + bs[:, gi:gi + 1]
            o_ref[rows, sl] = (zu[rows, sl] * sv).astype(BF16)


def _gmlp(p, g, ws, bs_t, *, chunks_per_step=8):
    T = p.shape[0]
    tm = chunks_per_step * A_CHUNK
    return pl.pallas_call(
        _gmlp_kernel,
        grid=(T // tm,),
        in_specs=[
            pl.BlockSpec((tm, A_WIDTH), lambda c: (c, 0)),
            pl.BlockSpec((tm, A_WIDTH), lambda c: (c, 1)),
            pl.BlockSpec((1, A_WIDTH), lambda c: (0, 0)),
            pl.BlockSpec((A_GROUPS, A_CHUNK, A_CHUNK), lambda c: (0, 0, 0)),
            pl.BlockSpec((A_CHUNK, A_GROUPS), lambda c: (0, 0)),
        ],
        out_specs=pl.BlockSpec((tm, A_WIDTH), lambda c: (c, 0)),
        out_shape=jax.ShapeDtypeStruct((T, A_WIDTH), BF16),
        compiler_params=_params("parallel"),
        name="gmlp",
    )(p, p, g, ws, bs_t)


def _log_sigmoid(x):
    return jnp.minimum(x, 0.0) - jnp.log1p(jnp.exp(-jnp.abs(x)))


def _mlstm_kernel(qk_ref, v_ref, o_ref, gate_ref, gb_ref, cw_ref, mn_ref, y_ref,
                  xx_ref, c_ref, n_ref, m_ref):
    @pl.when(pl.program_id(1) == 0)
    def _():
        xx_ref[0:SUBLANES, :] = jnp.zeros((SUBLANES, 2 * B_WIDTH), F32)
        c_ref[...] = jnp.zeros_like(c_ref)
        n_ref[...] = jnp.zeros_like(n_ref)
        m_ref[...] = jnp.zeros_like(m_ref)

    for sub in range(qk_ref.shape[0] // B_CHUNK):
        rows = pl.ds(sub * B_CHUNK, B_CHUNK)
        _mlstm_chunk(qk_ref.at[rows], v_ref.at[rows], o_ref.at[rows], gate_ref.at[rows], gb_ref, cw_ref, mn_ref,
                     y_ref.at[rows], xx_ref, c_ref, n_ref, m_ref)


def _mlstm_chunk(qk_ref, v_ref, o_ref, gate_ref, gb_ref, cw_ref, mn_ref, y_ref, xx_ref, c_ref, n_ref, m_ref):
    L, d = B_CHUNK, B_HEAD_DIM
    tail = SUBLANES

    xx_ref[tail:tail + L, :] = qk_ref[...]
    cw = cw_ref[...]
    conv = jnp.zeros((L, 2 * B_WIDTH), F32)
    for kk in range(B_CONV):
        off = tail - (B_CONV - 1) + kk
        conv = conv + cw[kk:kk + 1, :] * xx_ref[off:off + L, :]
    carry_rows = xx_ref[L:L + tail, :]
    xx_ref[0:tail, :] = carry_rows
    qk = conv * jax.nn.sigmoid(conv)

    gcol = gate_ref[...] + gb_ref[...]
    grow = gcol.T
    r_i = lax.broadcasted_iota(jnp.int32, (L, L), 0)
    c_i = lax.broadcasted_iota(jnp.int32, (L, L), 1)
    causal = c_i <= r_i
    tril = causal.astype(F32)
    triu = (r_i <= c_i).astype(F32)
    bcol_all = _dot_f32(tril, _log_sigmoid(gcol))
    brow_all = _dot_f32(_log_sigmoid(grow), triu)

    for hh in range(B_HEADS):
        q = qk[:, hh * d:(hh + 1) * d]
        k = qk[:, B_WIDTH + hh * d:B_WIDTH + (hh + 1) * d] * (d ** -0.5)
        v = v_ref[:, hh * d:(hh + 1) * d]
        qb, kb, vb = q.astype(BF16), k.astype(BF16), v.astype(BF16)
        i_col = gcol[:, hh:hh + 1]
        i_row = grow[hh:hh + 1, :]
        b_col = bcol_all[:, B_HEADS + hh:B_HEADS + hh + 1]
        b_row = brow_all[B_HEADS + hh:B_HEADS + hh + 1, :]
        b_last = b_col[L - 1:L, :]
        m_old = m_ref[hh:hh + 1, 0:1]

        logd = jnp.where(causal, b_col - b_row + i_row, NEG_INF)
        inter = b_col + m_old
        m_t = jnp.maximum(jnp.max(logd, axis=-1, keepdims=True), inter)
        dm = jnp.where(causal, jnp.exp(logd - m_t), 0.0)
        w_inter = jnp.exp(inter - m_t)
        s = _dot_nt(qb, kb) * dm
        ct = c_ref[hh]
        nvec = n_ref[hh:hh + 1, :]
        num = _dot(s.astype(BF16), vb) + w_inter * _dot(qb, ct.astype(BF16))
        den = jnp.sum(s, axis=-1, keepdims=True) + w_inter * jnp.sum(q * nvec, axis=-1, keepdims=True)
        hout = num / jnp.maximum(jnp.abs(den), jnp.exp(-m_t))

        m_new = m_t[L - 1:L, :]
        w_s = jnp.exp(b_last - b_col + i_col - m_new)
        w_prev = jnp.exp(b_last + m_old - m_new)
        c_ref[hh] = w_prev * ct + _dot_tn(kb, (w_s * v).astype(BF16))
        n_ref[hh:hh + 1, :] = w_prev * nvec + jnp.sum(w_s * k, axis=0, keepdims=True)
        m_ref[hh:hh + 1, :] = jnp.broadcast_to(m_new, (1, LANES))

        hn = _rms(hout, mn_ref[:, hh * d:(hh + 1) * d])
        og = jax.nn.sigmoid(o_ref[:, hh * d:(hh + 1) * d])
        y_ref[:, hh * d:(hh + 1) * d] = (og * hn).astype(BF16)


def _mlstm(p, gates, gate_b, conv_w, m_norm, batch, *, chunks_per_step=1):
    T = p.shape[0]
    L = chunks_per_step * B_CHUNK
    nc = T // batch // L
    qk_blk = (2 * A_WIDTH) // (2 * B_WIDTH)
    v_blk = (2 * A_WIDTH + 2 * B_WIDTH) // B_WIDTH
    row = lambda b, c: b * nc + c
    return pl.pallas_call(
        _mlstm_kernel,
        grid=(batch, nc),
        in_specs=[
            pl.BlockSpec((L, 2 * B_WIDTH), lambda b, c: (row(b, c), qk_blk)),
            pl.BlockSpec((L, B_WIDTH), lambda b, c: (row(b, c), v_blk)),
            pl.BlockSpec((L, B_WIDTH), lambda b, c: (row(b, c), v_blk + 1)),
            pl.BlockSpec((L, LANES), lambda b, c: (row(b, c), 0)),
            pl.BlockSpec((1, LANES), lambda b, c: (0, 0)),
            pl.BlockSpec((B_CONV, 2 * B_WIDTH), lambda b, c: (0, 0)),
            pl.BlockSpec((1, B_WIDTH), lambda b, c: (0, 0)),
        ],
        out_specs=pl.BlockSpec((L, B_WIDTH), lambda b, c: (row(b, c), 0)),
        out_shape=jax.ShapeDtypeStruct((T, B_WIDTH), BF16),
        scratch_shapes=[
            pltpu.VMEM((B_CHUNK + SUBLANES, 2 * B_WIDTH), F32),
            pltpu.VMEM((B_HEADS, B_HEAD_DIM, B_HEAD_DIM), F32),
            pltpu.VMEM((8, B_HEAD_DIM), F32),
            pltpu.VMEM((8, LANES), F32),
        ],
        compiler_params=_params("arbitrary", "arbitrary"),
        name="mlstm",
    )(p, p, p, gates, gate_b, conv_w, m_norm)


def _outproj_kernel(*refs, n_lhs):
    h_ref = refs[0]
    o_ref = refs[1 + 2 * n_lhs]
    acc = h_ref[...]
    for i in range(n_lhs):
        acc = acc + _dot(refs[1 + i][...], refs[1 + n_lhs + i][...])
    o_ref[...] = acc


def _outproj(h, lhs, w, *, tm=1024):
    T = h.shape[0]
    n = len(lhs)
    width = lhs[0].shape[1]
    assert all(a.shape[1] == width for a in lhs) and w.shape[0] == n * width
    in_specs = [pl.BlockSpec((tm, D_MODEL), lambda i: (i, 0))]
    in_specs += [pl.BlockSpec((tm, width), lambda i: (i, 0)) for _ in lhs]
    in_specs += [pl.BlockSpec((width, D_MODEL), functools.partial(lambda i, k: (k, 0), k=k)) for k in range(n)]
    return pl.pallas_call(
        functools.partial(_outproj_kernel, n_lhs=n),
        grid=(T // tm,),
        in_specs=in_specs,
        out_specs=pl.BlockSpec((tm, D_MODEL), lambda i: (i, 0)),
        out_shape=jax.ShapeDtypeStruct((T, D_MODEL), F32),
        compiler_params=_params("parallel"),
        name="outproj",
    )(h, *lhs, *([w] * n))


def _rope(x, ca, sn):
    half = ROPE_DIMS // 2
    lane = lax.broadcasted_iota(jnp.int32, x.shape, 1)
    partner = jnp.where(lane < half, lane + half, jnp.where(lane < ROPE_DIMS, lane - half, lane))
    return x * ca + jnp.take_along_axis(x, partner, axis=1) * sn


def _nsa_proj_kernel(h_ref, g_ref, w_ref, wgate_ref, ca_ref, sn_ref, o_ref, og_ref, n_ref):
    j = pl.program_id(1)

    @pl.when(j == 0)
    def _():
        n = _rms(h_ref[...], g_ref[...]).astype(BF16)
        n_ref[...] = n
        og_ref[...] = jax.nn.sigmoid(_dot(n, wgate_ref[...]))

    res = _dot(n_ref[...], w_ref[...])
    heads = o_ref.shape[1]
    is_q = j < C_HEADS // heads

    def head(k):
        return res[:, k * LANES:(k + 1) * LANES]

    @pl.when(is_q)
    def _():
        ca, sn = ca_ref[0], sn_ref[0]
        for k in range(heads):
            o_ref[0, k] = _rope(head(k), ca, sn).astype(BF16)

    @pl.when(jnp.logical_not(is_q))
    def _():
        ca, sn = ca_ref[0], sn_ref[0]
        for k in range(C_KV):
            o_ref[0, k] = _rope(head(k), ca, sn).astype(BF16)
            o_ref[0, C_KV + k] = head(C_KV + k).astype(BF16)


def _nsa_proj(h, g, w, wgate, ca, sn, batch, *, tm=1024):
    T = h.shape[0]
    S = T // batch
    nts = S // tm
    heads = 2 * C_KV
    tn = heads * C_HEAD_DIM
    n_q = C_HEADS // heads
    table = pl.BlockSpec((1, tm, LANES), lambda i, j: (jnp.where(j < n_q, 0, 1), i % nts, 0))
    return pl.pallas_call(
        _nsa_proj_kernel,
        grid=(T // tm, C_MAIN // tn),
        in_specs=[
            pl.BlockSpec((tm, D_MODEL), lambda i, j: (i, 0)),
            pl.BlockSpec((1, D_MODEL), lambda i, j: (0, 0)),
            pl.BlockSpec((D_MODEL, tn), lambda i, j: (0, j)),
            pl.BlockSpec((D_MODEL, LANES), lambda i, j: (0, 0)),
            table, table,
        ],
        out_specs=[
            pl.BlockSpec((1, heads, tm, C_HEAD_DIM), lambda i, j: (i // nts, j, i % nts, 0)),
            pl.BlockSpec((tm, LANES), lambda i, j: (i, 0)),
        ],
        out_shape=[
            jax.ShapeDtypeStruct((batch, HM_HEADS, S, C_HEAD_DIM), BF16),
            jax.ShapeDtypeStruct((T, LANES), F32),
        ],
        scratch_shapes=[pltpu.VMEM((tm, D_MODEL), BF16)],
        compiler_params=_params("parallel", "arbitrary"),
        name="nsa_proj",
    )(h, g, w, wgate, ca, sn)


def _compress_kernel(x_ref, pos_ref, w1_ref, w2_ref, o_ref, xs_ref):
    xs_ref[...] = x_ref[0, 0].astype(F32)
    n_rows = xs_ref.shape[0] // CMP_STRIDE
    dh = xs_ref.shape[1]
    ya = jnp.zeros((n_rows, dh), F32)
    yb = jnp.zeros((n_rows, dh), F32)
    for l in range(CMP_STRIDE):
        xl = xs_ref[pl.ds(l, n_rows, stride=CMP_STRIDE), :]
        ya = ya + _dot((xl + pos_ref[0, l:l + 1, :]).astype(BF16), w1_ref[0, l])
        lb = CMP_STRIDE + l
        yb = yb + _dot((xl + pos_ref[0, lb:lb + 1, :]).astype(BF16), w1_ref[0, lb])
    pre = ya + pltpu.roll(yb, n_rows - 1, 0)
    out = _dot(jax.nn.gelu(pre).astype(BF16), w2_ref[0])
    rows = lax.broadcasted_iota(jnp.int32, out.shape, 0)
    o_ref[0, 0, 0] = jnp.where(rows < n_rows - 1, out, 0.0).astype(BF16)


def _compress(qkv, pos, w1, w2):
    B_, _, S, dh = qkv.shape
    rows = S // CMP_STRIDE
    return pl.pallas_call(
        _compress_kernel,
        grid=(B_, 2, C_KV),
        in_specs=[
            pl.BlockSpec((1, 1, S, dh), lambda b, kv, g: (b, HM_KCMP + kv * C_KV + g, 0, 0)),
            pl.BlockSpec((1, CMP_LEN, dh), lambda b, kv, g: (kv, 0, 0)),
            pl.BlockSpec((1, CMP_LEN, dh, dh), lambda b, kv, g: (kv, 0, 0, 0)),
            pl.BlockSpec((1, dh, dh), lambda b, kv, g: (kv, 0, 0)),
        ],
        out_specs=pl.BlockSpec((1, 1, 1, rows, dh), lambda b, kv, g: (b, kv, g, 0, 0)),
        out_shape=jax.ShapeDtypeStruct((B_, 2, C_KV, rows, dh), BF16),
        scratch_shapes=[pltpu.VMEM((S, dh), F32)],
        compiler_params=_params("parallel", "parallel", "parallel"),
        name="nsa_compress",
    )(qkv, pos, w1, w2)


def _softmax_parts_t(s):
    m = jnp.max(s, axis=0, keepdims=True)
    p = jnp.exp2(s - m)
    return p, jnp.sum(p, axis=0, keepdims=True)


def _topk_membership_t(imp_t, n_top):
    n_blk, tq = imp_t.shape
    groups = n_blk // SUBLANES
    va = [imp_t[a * SUBLANES:(a + 1) * SUBLANES] for a in range(groups)]
    cnt = [jnp.zeros((SUBLANES, tq), F32) for _ in range(groups)]
    sub = lax.broadcasted_iota(jnp.int32, (SUBLANES, tq), 0)
    for i in range(n_blk):
        vi = jnp.broadcast_to(imp_t[i:i + 1, :], (SUBLANES, tq))
        for a in range(groups):
            if i < a * SUBLANES:
                ahead = jnp.where(vi >= va[a], 1.0, 0.0)
            elif i >= (a + 1) * SUBLANES:
                ahead = jnp.where(vi > va[a], 1.0, 0.0)
            else:
                ahead = jnp.where(sub > i - a * SUBLANES, jnp.where(vi >= va[a], 1.0, 0.0),
                                  jnp.where(vi > va[a], 1.0, 0.0))
            cnt[a] = cnt[a] + ahead
    return jnp.concatenate([jnp.where(c < n_top, 1.0, 0.0) for c in cnt], axis=0)


def _nsa_attn_kernel(q_ref, kc_ref, vc_ref, ks_ref, vs_ref, kw_ref, vw_ref, gate_ref, ovt_ref, y_ref,
                     qx_ref, m_ref, l_ref, acc_ref, oc_ref, ow_ref, *, tq, tk, n_slc, n_top):
    G, R, dh = C_KV, C_REP, C_HEAD_DIM
    t0 = pl.program_id(1) * tq
    pos = t0 + lax.broadcasted_iota(jnp.int32, (1, tq), 1)

    def q_of(g):
        return q_ref[0, g * R:(g + 1) * R].reshape(R * tq, dh)

    def per_head(x):
        return jnp.concatenate([x] * R, axis=1)

    n_cmp_rows = kc_ref.shape[3]
    cmp_end = lax.broadcasted_iota(jnp.int32, (n_cmp_rows, tq), 0) * CMP_STRIDE + (CMP_LEN - 1)
    ok_c = jnp.logical_and(cmp_end <= pos, cmp_end < n_cmp_rows * CMP_STRIDE)
    bias_c = per_head(jnp.where(ok_c, 0.0, NEG_INF))
    has_c = per_head(pos >= CMP_LEN - 1)
    slab = WINDOW + tq
    w0 = pl.multiple_of(jnp.maximum(t0 - WINDOW, 0), tq)
    kpos_w = w0 + lax.broadcasted_iota(jnp.int32, (slab, tq), 0)
    ok_w = jnp.logical_and(kpos_w <= pos, pos - kpos_w < WINDOW)
    bias_w = per_head(jnp.where(ok_w, 0.0, NEG_INF))
    blk_t = lax.broadcasted_iota(jnp.int32, (n_slc, tq), 0)
    cur_t = jnp.right_shift(pos, SEL_SHIFT)
    forced = jnp.logical_or(blk_t == 0, jnp.logical_or(blk_t == cur_t, blk_t == cur_t - 1))
    future = blk_t > cur_t

    s_cmp = [_dot_nt(kc_ref[0, 0, g], q_of(g)) + bias_c for g in range(G)]
    s_win = [_dot_nt(kw_ref[0, g, pl.ds(w0, slab), :], q_of(g)) + bias_w for g in range(G)]
    for g in range(G):
        p_c, l_c = _softmax_parts_t(s_cmp[g])
        p_c = p_c * jnp.where(has_c, 1.0 / l_c, 0.0)
        oc_ref[g] = _dot_tn(vc_ref[0, 0, g], p_c.astype(BF16))

        p_w, l_w = _softmax_parts_t(s_win[g])
        ow_ref[g] = _dot_tn(vw_ref[0, g, pl.ds(w0, slab), :], p_w.astype(BF16)) * (1.0 / l_w)

        p_sum = p_c[:, 0:tq]
        for r in range(1, R):
            p_sum = p_sum + p_c[:, r * tq:(r + 1) * tq]
        imp_t = jnp.where(forced, BIG, jnp.where(future, -BIG, _dot_f32(ovt_ref[...], p_sum)))
        sel_t = _topk_membership_t(imp_t, n_top)
        sel = jnp.concatenate([sel_t, jnp.zeros((LANES - n_slc, tq), F32)], axis=0).T
        penalty = jnp.where(sel > 0.5, 0.0, NEG_INF).astype(BF16)
        qx_ref[g] = jnp.concatenate([q_of(g), jnp.concatenate([penalty] * R, axis=0)], axis=1)

    m_ref[...] = jnp.full(m_ref.shape, NEG_INF, F32)
    l_ref[...] = jnp.zeros(l_ref.shape, F32)
    acc_ref[...] = jnp.zeros(acc_ref.shape, F32)
    blocks_per_tile = tk // SEL_LEN

    def sweep_tile(kt, diagonal):
        k0 = pl.multiple_of(kt * tk, tk)
        key_blk = kt * blocks_per_tile + jnp.right_shift(lax.broadcasted_iota(jnp.int32, (tk, LANES), 0), SEL_SHIFT)
        onehot = jnp.where(lax.broadcasted_iota(jnp.int32, (tk, LANES), 1) == key_blk, 1.0, 0.0).astype(BF16)
        s_all = []
        for g in range(G):
            k_ext = jnp.concatenate([ks_ref[0, g, pl.ds(k0, tk), :], onehot], axis=1)
            s = _dot_nt(k_ext, qx_ref[g])
            if diagonal:
                causal = k0 + lax.broadcasted_iota(jnp.int32, (tk, tq), 0) <= pos
                s = s + per_head(jnp.where(causal, 0.0, NEG_INF))
            s_all.append(s)
        for g, s in enumerate(s_all):
            m_old = m_ref[g]
            m_new = jnp.maximum(m_old, jnp.max(s, axis=0, keepdims=True))
            p = jnp.exp2(s - m_new)
            alpha = jnp.exp2(m_old - m_new)
            l_ref[g] = alpha * l_ref[g] + jnp.sum(p, axis=0, keepdims=True)
            acc_ref[g] = alpha * acc_ref[g] + _dot_tn(vs_ref[0, g, pl.ds(k0, tk), :], p.astype(BF16))
            m_ref[g] = m_new

    def sweep(kt, carry):
        sweep_tile(kt, diagonal=False)
        return carry

    last_tile = t0 // tk
    lax.fori_loop(0, last_tile, sweep, 0)
    sweep_tile(last_tile, diagonal=True)

    gates_t = gate_ref[...].T
    for g in range(G):
        os_t = acc_ref[g] * (1.0 / l_ref[g])
        oc_t, ow_t = oc_ref[g], ow_ref[g]
        for r in range(R):
            cols = slice(r * tq, (r + 1) * tq)
            hd = g * R + r
            merged_t = (gates_t[hd:hd + 1, :] * oc_t[:, cols]
                        + gates_t[C_HEADS + hd:C_HEADS + hd + 1, :] * os_t[:, cols]
                        + gates_t[2 * C_HEADS + hd:2 * C_HEADS + hd + 1, :] * ow_t[:, cols])
            y_ref[:, (g * R + r) * dh:(g * R + r + 1) * dh] = merged_t.T.astype(BF16)


def _nsa_attn(qkv, cmp_kv, gates, overlap_t, *, tq=256, tk=512):
    B_, _, S, dh = qkv.shape
    nq = S // tq
    n_slc = S // SEL_LEN
    n_top = min(SEL_TOPK, n_slc)
    assert n_top >= 3 and n_slc % SUBLANES == 0
    tk = min(tk, S)
    assert tq % LANES == 0 and tk % tq == 0 and n_slc <= LANES
    n_cmp_rows = cmp_kv.shape[3]
    G, R = C_KV, C_REP
    seq = lambda base: pl.BlockSpec((1, G, S, dh), lambda b, i: (b, base // G, 0, 0), pipeline_mode=pl.Buffered(1))
    return pl.pallas_call(
        functools.partial(_nsa_attn_kernel, tq=tq, tk=tk, n_slc=n_slc, n_top=n_top),
        grid=(B_, nq),
        in_specs=[
            pl.BlockSpec((1, C_HEADS, tq, dh), lambda b, i: (b, 0, i, 0)),
            pl.BlockSpec((1, 1, G, n_cmp_rows, dh), lambda b, i: (b, 0, 0, 0, 0)),
            pl.BlockSpec((1, 1, G, n_cmp_rows, dh), lambda b, i: (b, 1, 0, 0, 0)),
            seq(HM_KSEL), seq(HM_VSEL), seq(HM_KWIN), seq(HM_VWIN),
            pl.BlockSpec((tq, LANES), lambda b, i: (b * nq + i, 0)),
            pl.BlockSpec((n_slc, n_cmp_rows), lambda b, i: (0, 0)),
        ],
        out_specs=pl.BlockSpec((tq, C_HEADS * dh), lambda b, i: (b * nq + i, 0)),
        out_shape=jax.ShapeDtypeStruct((B_ * S, C_HEADS * dh), BF16),
        scratch_shapes=[
            pltpu.VMEM((G, R * tq, dh + LANES), BF16),
            pltpu.VMEM((G, 1, R * tq), F32),
            pltpu.VMEM((G, 1, R * tq), F32),
            pltpu.VMEM((G, dh, R * tq), F32),
            pltpu.VMEM((G, dh, R * tq), F32),
            pltpu.VMEM((G, dh, R * tq), F32),
        ],
        compiler_params=_params("parallel", "arbitrary"),
        name="nsa_attn",
    )(qkv, cmp_kv, cmp_kv, qkv, qkv, qkv, qkv, gates, overlap_t)


def _rope_tables(S):
    half = ROPE_DIMS // 2
    inv = 1.0 / (ROPE_THETA ** (jnp.arange(half, dtype=F32) / half))
    ang = jnp.arange(S, dtype=F32)[:, None] * inv[None, :]
    cos, sin = jnp.cos(ang), jnp.sin(ang)
    rest = LANES - ROPE_DIMS
    ca = jnp.concatenate([cos, cos, jnp.ones((S, rest), F32)], axis=-1)
    sn = jnp.concatenate([-sin, sin, jnp.zeros((S, rest), F32)], axis=-1)
    q_scale = C_HEAD_DIM ** -0.5 * LOG2E
    return jnp.stack([ca * q_scale, ca]), jnp.stack([sn * q_scale, sn])


def _overlap_matrix(n_cmp_rows, n_slc):
    ci = jnp.arange(n_cmp_rows)[:, None] * CMP_STRIDE
    sj = jnp.arange(n_slc)[None, :] * SEL_LEN
    ov = jnp.clip(jnp.minimum(ci + CMP_LEN, sj + SEL_LEN) - jnp.maximum(ci, sj), 0).astype(F32) / CMP_LEN
    return ov


def _gmlp_mlstm_layer(h, mix_g, w_in, gate_b, g_norm, g_ws, g_bs, conv_w, m_norm, w_out, batch):
    w_main = w_in.astype(BF16)
    w_gate = jnp.pad(w_in[:, AB_MAIN:], ((0, 0), (0, LANES - 2 * B_HEADS))).astype(BF16)
    p, gates = _ab_proj(h, mix_g.reshape(1, D_MODEL), w_main, w_gate)
    y_a = _gmlp(p, g_norm.reshape(1, A_WIDTH), g_ws, g_bs.T)
    gb = jnp.pad(gate_b.reshape(1, 2 * B_HEADS), ((0, 0), (0, LANES - 2 * B_HEADS)))
    y_b = _mlstm(p, gates, gb, conv_w, m_norm.reshape(1, B_WIDTH), batch)
    return _outproj(h, [y_a, y_b], w_out.astype(BF16))


def _nsa_layer(h, mix_g, w_in, cmp_pos, cmp_w1, cmp_w2, w_out, batch):
    T = h.shape[0]
    S = T // batch
    w_main = w_in.astype(BF16)
    wg = jnp.pad(w_in[:, C_MAIN:], ((0, 0), (0, LANES - 3 * C_HEADS))).astype(BF16)
    ca, sn = _rope_tables(S)
    qkv, gates = _nsa_proj(h, mix_g.reshape(1, D_MODEL), w_main, wg, ca, sn, batch)
    cmp_kv = _compress(qkv, cmp_pos, cmp_w1.reshape(2, CMP_LEN, C_HEAD_DIM, C_HEAD_DIM).astype(BF16),
                       cmp_w2.astype(BF16))
    overlap_t = _overlap_matrix(S // CMP_STRIDE, S // SEL_LEN).T
    y = _nsa_attn(qkv, cmp_kv, gates, overlap_t)
    return _outproj(h, [y], w_out.astype(BF16))


def kernel(x, ffn_norm, ffn_w_gate, ffn_w_up, ffn_w_down, mix_norm, ab_w_in, mlstm_gate_bias, gmlp_norm, gmlp_w_s,
           gmlp_b_s, mlstm_conv, mlstm_norm, ab_w_out, nsa_w_in, nsa_cmp_pos, nsa_cmp_w1, nsa_cmp_w2, nsa_w_out,
           final_norm):
    batch, S, _ = x.shape
    depth = ffn_norm.shape[0]
    h = x.reshape(batch * S, D_MODEL)
    fg = final_norm.reshape(1, D_MODEL)

    wg, wu, wd = ffn_w_gate.astype(BF16), ffn_w_up.astype(BF16), ffn_w_down.astype(BF16)

    def ffn(h, layer, half, final=False):
        return _ffn(h, ffn_norm[layer, half].reshape(1, D_MODEL), wg, wu, wd, fg, layer, half, final=final)

    for layer in range(depth):
        j = layer // 2
        h = ffn(h, layer, 0)
        if layer % 2 == 0:
            h = _gmlp_mlstm_layer(h, mix_norm[layer], ab_w_in[j], mlstm_gate_bias[j], gmlp_norm[j], gmlp_w_s[j],
                                  gmlp_b_s[j], mlstm_conv[j], mlstm_norm[j], ab_w_out[j], batch)
        else:
            h = _nsa_layer(h, mix_norm[layer], nsa_w_in[j], nsa_cmp_pos[j], nsa_cmp_w1[j], nsa_cmp_w2[j],
                           nsa_w_out[j], batch)
        h = ffn(h, layer, 1, final=(layer == depth - 1))
    return h.reshape(batch, S, D_MODEL)
```

```python
import functools
import math

import jax
import jax.numpy as jnp
from jax import lax
from jax.experimental import pallas as pl
from jax.experimental.pallas import tpu as pltpu

F32 = jnp.float32
BF16 = jnp.bfloat16

D_MODEL = 2048
D_FF = 5632
EPS = 1e-6

A_WIDTH = D_MODEL // 2
A_CHUNK = 128
A_GROUPS = 8
A_GROUP_DIM = A_WIDTH // A_GROUPS

B_HEADS = 4
B_WIDTH = D_MODEL // 2
B_HEAD_DIM = B_WIDTH // B_HEADS
B_CHUNK = 128
B_CONV = 4
AB_MAIN = 2 * A_WIDTH + 4 * B_WIDTH

C_HEADS = 16
C_KV = 4
C_REP = C_HEADS // C_KV
C_HEAD_DIM = D_MODEL // C_HEADS
CMP_LEN = 32
CMP_STRIDE = 16
SEL_LEN = 64
SEL_SHIFT = SEL_LEN.bit_length() - 1
SEL_TOPK = 16
WINDOW = 512
C_MAIN = C_HEADS * C_HEAD_DIM + 6 * C_KV * C_HEAD_DIM
ROPE_THETA = 500000.0
ROPE_DIMS = C_HEAD_DIM // 4
NEG_INF = -1e30
BIG = 1e9
LOG2E = math.log2(math.e)

LANES = 128
SUBLANES = 8
VMEM_LIMIT = 60 * 1024 * 1024

HM_HEADS = C_HEADS + 6 * C_KV
HM_KCMP, HM_VCMP, HM_KSEL, HM_VSEL, HM_KWIN, HM_VWIN = (C_HEADS + i * C_KV for i in range(6))


def _params(*sem):
    return pltpu.CompilerParams(dimension_semantics=sem, vmem_limit_bytes=VMEM_LIMIT)


def _rms(x, g):
    return x * lax.rsqrt(jnp.mean(x * x, axis=-1, keepdims=True) + EPS) * g


def _dot(a, b):
    return jnp.dot(a, b, preferred_element_type=F32)


def _dot_nt(a, b):
    return lax.dot_general(a, b, (((1,), (1,)), ((), ())), preferred_element_type=F32)


def _dot_tn(a, b):
    return lax.dot_general(a, b, (((0,), (0,)), ((), ())), preferred_element_type=F32)


def _dot_f32(a, b):
    return jnp.dot(a, b, preferred_element_type=F32, precision=lax.Precision.HIGHEST)


def _ffn_kernel(h_ref, g_ref, wg_ref, wu_ref, wd_ref, fg_ref, o_ref, n_ref, *, final):
    j = pl.program_id(1)

    @pl.when(j == 0)
    def _():
        x = h_ref[...]
        n_ref[...] = _rms(x, g_ref[...]).astype(BF16)
        o_ref[...] = x

    n = n_ref[...]
    a = _dot(n, wg_ref[...])
    b = _dot(n, wu_ref[...])
    t = (a * jax.nn.sigmoid(a) * (0.5 * b)).astype(BF16)
    o_ref[...] += _dot(t, wd_ref[...])

    if final:
        @pl.when(j == pl.num_programs(1) - 1)
        def _():
            o_ref[...] = _rms(o_ref[...], fg_ref[...])


def _ffn(h, g, wg, wu, wd, fg, layer, half, *, final, tm=1024, tf=512):
    T = h.shape[0]
    return pl.pallas_call(
        functools.partial(_ffn_kernel, final=final),
        grid=(T // tm, D_FF // tf),
        in_specs=[
            pl.BlockSpec((tm, D_MODEL), lambda i, j: (i, 0)),
            pl.BlockSpec((1, D_MODEL), lambda i, j: (0, 0)),
            pl.BlockSpec((None, None, D_MODEL, tf), lambda i, j: (layer, half, 0, j)),
            pl.BlockSpec((None, None, D_MODEL, tf), lambda i, j: (layer, half, 0, j)),
            pl.BlockSpec((None, None, tf, D_MODEL), lambda i, j: (layer, half, j, 0)),
            pl.BlockSpec((1, D_MODEL), lambda i, j: (0, 0)),
        ],
        out_specs=pl.BlockSpec((tm, D_MODEL), lambda i, j: (i, 0)),
        out_shape=jax.ShapeDtypeStruct((T, D_MODEL), F32),
        scratch_shapes=[pltpu.VMEM((tm, D_MODEL), BF16)],
        compiler_params=_params("parallel", "arbitrary"),
        name="ffn_final" if final else "ffn",
    )(h, g, wg, wu, wd, fg)


def _ab_proj_kernel(h_ref, g_ref, w_ref, wgate_ref, o_ref, og_ref, n_ref):
    j = pl.program_id(1)

    @pl.when(j == 0)
    def _():
        n = _rms(h_ref[...], g_ref[...]).astype(BF16)
        n_ref[...] = n
        og_ref[...] = _dot(n, wgate_ref[...])

    o_ref[...] = _dot(n_ref[...], w_ref[...])


def _ab_proj(h, g, w, wgate, *, tm=1024, tn=2048):
    T = h.shape[0]
    return pl.pallas_call(
        _ab_proj_kernel,
        grid=(T // tm, AB_MAIN // tn),
        in_specs=[
            pl.BlockSpec((tm, D_MODEL), lambda i, j: (i, 0)),
            pl.BlockSpec((1, D_MODEL), lambda i, j: (0, 0)),
            pl.BlockSpec((D_MODEL, tn), lambda i, j: (0, j)),
            pl.BlockSpec((D_MODEL, LANES), lambda i, j: (0, 0)),
        ],
        out_specs=[
            pl.BlockSpec((tm, tn), lambda i, j: (i, j)),
            pl.BlockSpec((tm, LANES), lambda i, j: (i, 0)),
        ],
        out_shape=[
            jax.ShapeDtypeStruct((T, AB_MAIN), F32),
            jax.ShapeDtypeStruct((T, LANES), F32),
        ],
        scratch_shapes=[pltpu.VMEM((tm, D_MODEL), BF16)],
        compiler_params=_params("parallel", "arbitrary"),
        name="ab_proj",
    )(h, g, w, wgate)


def _gmlp_kernel(u_ref, v_ref, g_ref, ws_ref, bs_ref, o_ref):
    zu = jax.nn.gelu(u_ref[...])
    vn = _rms(jax.nn.gelu(v_ref[...]), g_ref[...]).astype(BF16)
    row = lax.broadcasted_iota(jnp.int32, (A_CHUNK, A_CHUNK), 0)
    col = lax.broadcasted_iota(jnp.int32, (A_CHUNK, A_CHUNK), 1)
    causal = col <= row
    bs = bs_ref[...]
    for gi in range(A_GROUPS):
        sl = slice(gi * A_GROUP_DIM, (gi + 1) * A_GROUP_DIM)
        w = jnp.where(causal, ws_ref[gi], 0.0).astype(BF16)
        for c in range(u_ref.shape[0] // A_CHUNK):
            rows = slice(c * A_CHUNK, (c + 1) * A_CHUNK)
            sv = _dot(w, vn[rows, sl]) + bs[:, gi:gi + 1]
            o_ref[rows, sl] = (zu[rows, sl] * sv).astype(BF16)


def _gmlp(p, g, ws, bs_t, *, chunks_per_step=8):
    T = p.shape[0]
    tm = chunks_per_step * A_CHUNK
    return pl.pallas_call(
        _gmlp_kernel,
        grid=(T // tm,),
        in_specs=[
            pl.BlockSpec((tm, A_WIDTH), lambda c: (c, 0)),
            pl.BlockSpec((tm, A_WIDTH), lambda c: (c, 1)),
            pl.BlockSpec((1, A_WIDTH), lambda c: (0, 0)),
            pl.BlockSpec((A_GROUPS, A_CHUNK, A_CHUNK), lambda c: (0, 0, 0)),
            pl.BlockSpec((A_CHUNK, A_GROUPS), lambda c: (0, 0)),
        ],
        out_specs=pl.BlockSpec((tm, A_WIDTH), lambda c: (c, 0)),
        out_shape=jax.ShapeDtypeStruct((T, A_WIDTH), BF16),
        compiler_params=_params("parallel"),
        name="gmlp",
    )(p, p, g, ws, bs_t)


def _log_sigmoid(x):
    return jnp.minimum(x, 0.0) - jnp.log1p(jnp.exp(-jnp.abs(x)))


def _mlstm_kernel(qk_ref, v_ref, o_ref, gate_ref, gb_ref, cw_ref, mn_ref, y_ref,
                  xx_ref, c_ref, n_ref, m_ref):
    n_batch = qk_ref.shape[0]

    @pl.when(pl.program_id(1) == 0)
    def _():
        xx_ref[:, 0:SUBLANES, :] = jnp.zeros((n_batch, SUBLANES, 2 * B_WIDTH), F32)
        c_ref[...] = jnp.zeros_like(c_ref)
        n_ref[...] = jnp.zeros_like(n_ref)
        m_ref[...] = jnp.zeros_like(m_ref)

    for bi in range(n_batch):
        _mlstm_chunk(qk_ref.at[bi], v_ref.at[bi], o_ref.at[bi], gate_ref.at[bi], gb_ref, cw_ref, mn_ref,
                     y_ref.at[bi], xx_ref.at[bi], c_ref.at[bi], n_ref.at[bi], m_ref.at[bi])


def _mlstm_chunk(qk_ref, v_ref, o_ref, gate_ref, gb_ref, cw_ref, mn_ref, y_ref, xx_ref, c_ref, n_ref, m_ref):
    L, d = B_CHUNK, B_HEAD_DIM
    tail = SUBLANES

    x = qk_ref[...]
    prev = xx_ref[0:tail, :]
    cw = cw_ref[...]
    sub = lax.broadcasted_iota(jnp.int32, (tail, 2 * B_WIDTH), 0)
    conv = cw[B_CONV - 1:B_CONV, :] * x
    for k in range(1, B_CONV):
        r = pltpu.roll(x, k, 0)
        head = jnp.where(sub < k, pltpu.roll(prev, k, 0), r[0:tail])
        conv = conv + cw[B_CONV - 1 - k:B_CONV - k, :] * jnp.concatenate([head, r[tail:]], axis=0)
    xx_ref[0:tail, :] = x[L - tail:L]
    qk = conv * jax.nn.sigmoid(conv)

    gcol = gate_ref[...] + gb_ref[...]
    grow = gcol.T
    r_i = lax.broadcasted_iota(jnp.int32, (L, L), 0)
    c_i = lax.broadcasted_iota(jnp.int32, (L, L), 1)
    causal = c_i <= r_i
    tril = causal.astype(F32)
    triu = (r_i <= c_i).astype(F32)
    bcol_all = _dot_f32(tril, _log_sigmoid(gcol))
    brow_all = _dot_f32(_log_sigmoid(grow), triu)

    for hh in range(B_HEADS):
        q = qk[:, hh * d:(hh + 1) * d]
        k = qk[:, B_WIDTH + hh * d:B_WIDTH + (hh + 1) * d] * (d ** -0.5)
        v = v_ref[:, hh * d:(hh + 1) * d]
        qb, kb, vb = q.astype(BF16), k.astype(BF16), v.astype(BF16)
        i_col = gcol[:, hh:hh + 1]
        i_row = grow[hh:hh + 1, :]
        b_col = bcol_all[:, B_HEADS + hh:B_HEADS + hh + 1]
        b_row = brow_all[B_HEADS + hh:B_HEADS + hh + 1, :]
        b_last = b_col[L - 1:L, :]
        m_old = m_ref[hh:hh + 1, 0:1]

        logd = jnp.where(causal, b_col - b_row + i_row, NEG_INF)
        inter = b_col + m_old
        m_t = jnp.maximum(jnp.max(logd, axis=-1, keepdims=True), inter)
        dm = jnp.where(causal, jnp.exp(logd - m_t), 0.0)
        w_inter = jnp.exp(inter - m_t)
        s = _dot_nt(qb, kb) * dm
        ct = c_ref[hh]
        nvec = n_ref[hh:hh + 1, :]
        num = _dot(s.astype(BF16), vb) + w_inter * _dot(qb, ct.astype(BF16))
        den = jnp.sum(s, axis=-1, keepdims=True) + w_inter * jnp.sum(q * nvec, axis=-1, keepdims=True)
        hout = num / jnp.maximum(jnp.abs(den), jnp.exp(-m_t))

        m_new = m_t[L - 1:L, :]
        w_s = jnp.exp(b_last - b_col + i_col - m_new)
        w_prev = jnp.exp(b_last + m_old - m_new)
        c_ref[hh] = w_prev * ct + _dot_tn(kb, (w_s * v).astype(BF16))
        n_ref[hh:hh + 1, :] = w_prev * nvec + jnp.sum(w_s * k, axis=0, keepdims=True)
        m_ref[hh:hh + 1, :] = jnp.broadcast_to(m_new, (1, LANES))

        hn = _rms(hout, mn_ref[:, hh * d:(hh + 1) * d])
        og = jax.nn.sigmoid(o_ref[:, hh * d:(hh + 1) * d])
        y_ref[:, hh * d:(hh + 1) * d] = (og * hn).astype(BF16)


def _mlstm(p, gates, gate_b, conv_w, m_norm, batch, *, batches_per_step=1):
    T = p.shape[0]
    S = T // batch
    L, nb = B_CHUNK, batches_per_step
    qk_blk = (2 * A_WIDTH) // (2 * B_WIDTH)
    v_blk = (2 * A_WIDTH + 2 * B_WIDTH) // B_WIDTH
    p3 = p.reshape(batch, S, p.shape[1])
    y = pl.pallas_call(
        _mlstm_kernel,
        grid=(batch // nb, S // L),
        in_specs=[
            pl.BlockSpec((nb, L, 2 * B_WIDTH), lambda b, c: (b, c, qk_blk)),
            pl.BlockSpec((nb, L, B_WIDTH), lambda b, c: (b, c, v_blk)),
            pl.BlockSpec((nb, L, B_WIDTH), lambda b, c: (b, c, v_blk + 1)),
            pl.BlockSpec((nb, L, LANES), lambda b, c: (b, c, 0)),
            pl.BlockSpec((1, LANES), lambda b, c: (0, 0)),
            pl.BlockSpec((B_CONV, 2 * B_WIDTH), lambda b, c: (0, 0)),
            pl.BlockSpec((1, B_WIDTH), lambda b, c: (0, 0)),
        ],
        out_specs=pl.BlockSpec((nb, L, B_WIDTH), lambda b, c: (b, c, 0)),
        out_shape=jax.ShapeDtypeStruct((batch, S, B_WIDTH), BF16),
        scratch_shapes=[
            pltpu.VMEM((nb, L + SUBLANES, 2 * B_WIDTH), F32),
            pltpu.VMEM((nb, B_HEADS, B_HEAD_DIM, B_HEAD_DIM), F32),
            pltpu.VMEM((nb, SUBLANES, B_HEAD_DIM), F32),
            pltpu.VMEM((nb, SUBLANES, LANES), F32),
        ],
        compiler_params=_params("arbitrary", "arbitrary"),
        name="mlstm",
    )(p3, p3, p3, gates.reshape(batch, S, LANES), gate_b, conv_w, m_norm)
    return y.reshape(T, B_WIDTH)


def _outproj_kernel(*refs, n_lhs):
    h_ref = refs[0]
    o_ref = refs[1 + 2 * n_lhs]
    acc = h_ref[...]
    for i in range(n_lhs):
        acc = acc + _dot(refs[1 + i][...], refs[1 + n_lhs + i][...])
    o_ref[...] = acc


def _outproj(h, lhs, w, *, tm=1024):
    T = h.shape[0]
    n = len(lhs)
    width = lhs[0].shape[1]
    assert all(a.shape[1] == width for a in lhs) and w.shape[0] == n * width
    in_specs = [pl.BlockSpec((tm, D_MODEL), lambda i: (i, 0))]
    in_specs += [pl.BlockSpec((tm, width), lambda i: (i, 0)) for _ in lhs]
    in_specs += [pl.BlockSpec((width, D_MODEL), functools.partial(lambda i, k: (k, 0), k=k)) for k in range(n)]
    return pl.pallas_call(
        functools.partial(_outproj_kernel, n_lhs=n),
        grid=(T // tm,),
        in_specs=in_specs,
        out_specs=pl.BlockSpec((tm, D_MODEL), lambda i: (i, 0)),
        out_shape=jax.ShapeDtypeStruct((T, D_MODEL), F32),
        compiler_params=_params("parallel"),
        name="outproj",
    )(h, *lhs, *([w] * n))


def _rope(x, ca, sn):
    half = ROPE_DIMS // 2
    lane = lax.broadcasted_iota(jnp.int32, x.shape, 1)
    partner = jnp.where(lane < half, lane + half, jnp.where(lane < ROPE_DIMS, lane - half, lane))
    return x * ca + jnp.take_along_axis(x, partner, axis=1) * sn


def _nsa_proj_kernel(h_ref, g_ref, w_ref, wgate_ref, ca_ref, sn_ref, o_ref, og_ref, n_ref):
    j = pl.program_id(1)

    @pl.when(j == 0)
    def _():
        n = _rms(h_ref[...], g_ref[...]).astype(BF16)
        n_ref[...] = n
        og_ref[...] = jax.nn.sigmoid(_dot(n, wgate_ref[...]))

    res = _dot(n_ref[...], w_ref[...])
    heads = o_ref.shape[1]
    is_q = j < C_HEADS // heads

    def head(k):
        return res[:, k * LANES:(k + 1) * LANES]

    @pl.when(is_q)
    def _():
        ca, sn = ca_ref[0], sn_ref[0]
        for k in range(heads):
            o_ref[0, k] = _rope(head(k), ca, sn).astype(BF16)

    @pl.when(jnp.logical_not(is_q))
    def _():
        ca, sn = ca_ref[0], sn_ref[0]
        for k in range(C_KV):
            o_ref[0, k] = _rope(head(k), ca, sn).astype(BF16)
            o_ref[0, C_KV + k] = head(C_KV + k).astype(BF16)


def _nsa_proj(h, g, w, wgate, ca, sn, batch, *, tm=1024):
    T = h.shape[0]
    S = T // batch
    nts = S // tm
    heads = 2 * C_KV
    tn = heads * C_HEAD_DIM
    n_q = C_HEADS // heads
    table = pl.BlockSpec((1, tm, LANES), lambda i, j: (jnp.where(j < n_q, 0, 1), i % nts, 0))
    return pl.pallas_call(
        _nsa_proj_kernel,
        grid=(T // tm, C_MAIN // tn),
        in_specs=[
            pl.BlockSpec((tm, D_MODEL), lambda i, j: (i, 0)),
            pl.BlockSpec((1, D_MODEL), lambda i, j: (0, 0)),
            pl.BlockSpec((D_MODEL, tn), lambda i, j: (0, j)),
            pl.BlockSpec((D_MODEL, LANES), lambda i, j: (0, 0)),
            table, table,
        ],
        out_specs=[
            pl.BlockSpec((1, heads, tm, C_HEAD_DIM), lambda i, j: (i // nts, j, i % nts, 0)),
            pl.BlockSpec((tm, LANES), lambda i, j: (i, 0)),
        ],
        out_shape=[
            jax.ShapeDtypeStruct((batch, HM_HEADS, S, C_HEAD_DIM), BF16),
            jax.ShapeDtypeStruct((T, LANES), F32),
        ],
        scratch_shapes=[pltpu.VMEM((tm, D_MODEL), BF16)],
        compiler_params=_params("parallel", "arbitrary"),
        name="nsa_proj",
    )(h, g, w, wgate, ca, sn)


def _compress_kernel(x_ref, pos_ref, w1_ref, w2_ref, o_ref, xs_ref):
    xs_ref[...] = x_ref[0].astype(F32)
    G, S, dh = xs_ref.shape
    n_rows = S // CMP_STRIDE
    ya = jnp.zeros((G * n_rows, dh), F32)
    yb = jnp.zeros((G * n_rows, dh), F32)
    for l in range(CMP_STRIDE):
        xl = xs_ref[:, pl.ds(l, n_rows, stride=CMP_STRIDE), :].reshape(G * n_rows, dh)
        ya = ya + _dot((xl + pos_ref[0, l:l + 1, :]).astype(BF16), w1_ref[0, l])
        lb = CMP_STRIDE + l
        yb = yb + _dot((xl + pos_ref[0, lb:lb + 1, :]).astype(BF16), w1_ref[0, lb])
    pre = ya + pltpu.roll(yb, G * n_rows - 1, 0)
    out = _dot(jax.nn.gelu(pre).astype(BF16), w2_ref[0]).reshape(G, n_rows, dh)
    rows = lax.broadcasted_iota(jnp.int32, out.shape, 1)
    o_ref[0, 0] = jnp.where(rows < n_rows - 1, out, 0.0).astype(BF16)


def _compress(qkv, pos, w1, w2):
    B_, _, S, dh = qkv.shape
    rows = S // CMP_STRIDE
    return pl.pallas_call(
        _compress_kernel,
        grid=(B_, 2),
        in_specs=[
            pl.BlockSpec((1, C_KV, S, dh), lambda b, kv: (b, HM_KCMP // C_KV + kv, 0, 0)),
            pl.BlockSpec((1, CMP_LEN, dh), lambda b, kv: (kv, 0, 0)),
            pl.BlockSpec((1, CMP_LEN, dh, dh), lambda b, kv: (kv, 0, 0, 0)),
            pl.BlockSpec((1, dh, dh), lambda b, kv: (kv, 0, 0)),
        ],
        out_specs=pl.BlockSpec((1, 1, C_KV, rows, dh), lambda b, kv: (b, kv, 0, 0, 0)),
        out_shape=jax.ShapeDtypeStruct((B_, 2, C_KV, rows, dh), BF16),
        scratch_shapes=[pltpu.VMEM((C_KV, S, dh), F32)],
        compiler_params=_params("parallel", "parallel"),
        name="nsa_compress",
    )(qkv, pos, w1, w2)


def _softmax_parts_t(s):
    m = jnp.max(s, axis=0, keepdims=True)
    p = jnp.exp2(s - m)
    return p, jnp.sum(p, axis=0, keepdims=True)


def _topk_membership_t(imp_t, n_top):
    n_blk, tq = imp_t.shape
    groups = n_blk // SUBLANES
    va = [imp_t[a * SUBLANES:(a + 1) * SUBLANES] for a in range(groups)]
    cnt = [jnp.zeros((SUBLANES, tq), F32) for _ in range(groups)]
    sub = lax.broadcasted_iota(jnp.int32, (SUBLANES, tq), 0)
    for i in range(n_blk):
        vi = jnp.broadcast_to(imp_t[i:i + 1, :], (SUBLANES, tq))
        for a in range(groups):
            if i < a * SUBLANES:
                ahead = jnp.where(vi >= va[a], 1.0, 0.0)
            elif i >= (a + 1) * SUBLANES:
                ahead = jnp.where(vi > va[a], 1.0, 0.0)
            else:
                ahead = jnp.where(sub > i - a * SUBLANES, jnp.where(vi >= va[a], 1.0, 0.0),
                                  jnp.where(vi > va[a], 1.0, 0.0))
            cnt[a] = cnt[a] + ahead
    return jnp.concatenate([jnp.where(c < n_top, 1.0, 0.0) for c in cnt], axis=0)


def _nsa_attn_kernel(q_ref, kc_ref, vc_ref, ks_ref, vs_ref, kw_ref, vw_ref, gate_ref, ovt_ref, y_ref,
                     qx_ref, m_ref, l_ref, acc_ref, oc_ref, ow_ref, *, tq, tk, n_slc, n_top):
    G, R, dh = C_KV, C_REP, C_HEAD_DIM
    t0 = pl.program_id(1) * tq
    pos = t0 + lax.broadcasted_iota(jnp.int32, (1, tq), 1)

    def q_of(g):
        return q_ref[0, g * R:(g + 1) * R].reshape(R * tq, dh)

    def per_head(x):
        return jnp.concatenate([x] * R, axis=1)

    n_cmp_rows = kc_ref.shape[3]
    cmp_end = lax.broadcasted_iota(jnp.int32, (n_cmp_rows, tq), 0) * CMP_STRIDE + (CMP_LEN - 1)
    ok_c = jnp.logical_and(cmp_end <= pos, cmp_end < n_cmp_rows * CMP_STRIDE)
    bias_c = per_head(jnp.where(ok_c, 0.0, NEG_INF))
    has_c = per_head(pos >= CMP_LEN - 1)
    slab = WINDOW + tq
    w0 = pl.multiple_of(jnp.maximum(t0 - WINDOW, 0), tq)
    kpos_w = w0 + lax.broadcasted_iota(jnp.int32, (slab, tq), 0)
    ok_w = jnp.logical_and(kpos_w <= pos, pos - kpos_w < WINDOW)
    bias_w = per_head(jnp.where(ok_w, 0.0, NEG_INF))
    blk_t = lax.broadcasted_iota(jnp.int32, (n_slc, tq), 0)
    cur_t = jnp.right_shift(pos, SEL_SHIFT)
    forced = jnp.logical_or(blk_t == 0, jnp.logical_or(blk_t == cur_t, blk_t == cur_t - 1))
    future = blk_t > cur_t

    s_cmp = [_dot_nt(kc_ref[0, 0, g], q_of(g)) + bias_c for g in range(G)]
    s_win = [_dot_nt(kw_ref[0, g, pl.ds(w0, slab), :], q_of(g)) + bias_w for g in range(G)]
    for g in range(G):
        p_c, l_c = _softmax_parts_t(s_cmp[g])
        p_c = p_c * jnp.where(has_c, 1.0 / l_c, 0.0)
        oc_ref[g] = _dot_tn(vc_ref[0, 0, g], p_c.astype(BF16))

        p_w, l_w = _softmax_parts_t(s_win[g])
        ow_ref[g] = _dot_tn(vw_ref[0, g, pl.ds(w0, slab), :], p_w.astype(BF16)) * (1.0 / l_w)

        p_sum = p_c[:, 0:tq]
        for r in range(1, R):
            p_sum = p_sum + p_c[:, r * tq:(r + 1) * tq]
        imp_t = jnp.where(forced, BIG, jnp.where(future, -BIG, _dot_f32(ovt_ref[...], p_sum)))
        sel_t = _topk_membership_t(imp_t, n_top)
        sel = jnp.concatenate([sel_t, jnp.zeros((LANES - n_slc, tq), F32)], axis=0).T
        penalty = jnp.where(sel > 0.5, 0.0, NEG_INF).astype(BF16)
        qx_ref[g] = jnp.concatenate([q_of(g), jnp.concatenate([penalty] * R, axis=0)], axis=1)

    m_ref[...] = jnp.full(m_ref.shape, NEG_INF, F32)
    l_ref[...] = jnp.zeros(l_ref.shape, F32)
    acc_ref[...] = jnp.zeros(acc_ref.shape, F32)
    blocks_per_tile = tk // SEL_LEN

    def sweep_tile(kt, diagonal):
        k0 = pl.multiple_of(kt * tk, tk)
        key_blk = kt * blocks_per_tile + jnp.right_shift(lax.broadcasted_iota(jnp.int32, (tk, LANES), 0), SEL_SHIFT)
        onehot = jnp.where(lax.broadcasted_iota(jnp.int32, (tk, LANES), 1) == key_blk, 1.0, 0.0).astype(BF16)
        s_all = []
        for g in range(G):
            k_ext = jnp.concatenate([ks_ref[0, g, pl.ds(k0, tk), :], onehot], axis=1)
            s = _dot_nt(k_ext, qx_ref[g])
            if diagonal:
                causal = k0 + lax.broadcasted_iota(jnp.int32, (tk, tq), 0) <= pos
                s = s + per_head(jnp.where(causal, 0.0, NEG_INF))
            s_all.append(s)
        for g, s in enumerate(s_all):
            m_old = m_ref[g]
            m_new = jnp.maximum(m_old, jnp.max(s, axis=0, keepdims=True))
            p = jnp.exp2(s - m_new)
            alpha = jnp.exp2(m_old - m_new)
            l_ref[g] = alpha * l_ref[g] + jnp.sum(p, axis=0, keepdims=True)
            acc_ref[g] = alpha * acc_ref[g] + _dot_tn(vs_ref[0, g, pl.ds(k0, tk), :], p.astype(BF16))
            m_ref[g] = m_new

    def sweep(kt, carry):
        sweep_tile(kt, diagonal=False)
        return carry

    last_tile = t0 // tk
    lax.fori_loop(0, last_tile, sweep, 0)
    sweep_tile(last_tile, diagonal=True)

    gates_t = gate_ref[...].T
    for g in range(G):
        os_t = acc_ref[g] * (1.0 / l_ref[g])
        oc_t, ow_t = oc_ref[g], ow_ref[g]
        for r in range(R):
            cols = slice(r * tq, (r + 1) * tq)
            hd = g * R + r
            merged_t = (gates_t[hd:hd + 1, :] * oc_t[:, cols]
                        + gates_t[C_HEADS + hd:C_HEADS + hd + 1, :] * os_t[:, cols]
                        + gates_t[2 * C_HEADS + hd:2 * C_HEADS + hd + 1, :] * ow_t[:, cols])
            y_ref[:, (g * R + r) * dh:(g * R + r + 1) * dh] = merged_t.T.astype(BF16)


def _nsa_attn(qkv, cmp_kv, gates, overlap_t, *, tq=256, tk=512):
    B_, _, S, dh = qkv.shape
    nq = S // tq
    n_slc = S // SEL_LEN
    n_top = min(SEL_TOPK, n_slc)
    assert n_top >= 3 and n_slc % SUBLANES == 0
    tk = min(tk, S)
    assert tq % LANES == 0 and tk % tq == 0 and n_slc <= LANES
    n_cmp_rows = cmp_kv.shape[3]
    G, R = C_KV, C_REP
    seq = lambda base: pl.BlockSpec((1, G, S, dh), lambda b, i: (b, base // G, 0, 0), pipeline_mode=pl.Buffered(1))
    return pl.pallas_call(
        functools.partial(_nsa_attn_kernel, tq=tq, tk=tk, n_slc=n_slc, n_top=n_top),
        grid=(B_, nq),
        in_specs=[
            pl.BlockSpec((1, C_HEADS, tq, dh), lambda b, i: (b, 0, i, 0)),
            pl.BlockSpec((1, 1, G, n_cmp_rows, dh), lambda b, i: (b, 0, 0, 0, 0)),
            pl.BlockSpec((1, 1, G, n_cmp_rows, dh), lambda b, i: (b, 1, 0, 0, 0)),
            seq(HM_KSEL), seq(HM_VSEL), seq(HM_KWIN), seq(HM_VWIN),
            pl.BlockSpec((tq, LANES), lambda b, i: (b * nq + i, 0)),
            pl.BlockSpec((n_slc, n_cmp_rows), lambda b, i: (0, 0)),
        ],
        out_specs=pl.BlockSpec((tq, C_HEADS * dh), lambda b, i: (b * nq + i, 0)),
        out_shape=jax.ShapeDtypeStruct((B_ * S, C_HEADS * dh), BF16),
        scratch_shapes=[
            pltpu.VMEM((G, R * tq, dh + LANES), BF16),
            pltpu.VMEM((G, 1, R * tq), F32),
            pltpu.VMEM((G, 1, R * tq), F32),
            pltpu.VMEM((G, dh, R * tq), F32),
            pltpu.VMEM((G, dh, R * tq), F32),
            pltpu.VMEM((G, dh, R * tq), F32),
        ],
        compiler_params=_params("parallel", "arbitrary"),
        name="nsa_attn",
    )(qkv, cmp_kv, cmp_kv, qkv, qkv, qkv, qkv, gates, overlap_t)


def _rope_tables(S):
    half = ROPE_DIMS // 2
    inv = 1.0 / (ROPE_THETA ** (jnp.arange(half, dtype=F32) / half))
    ang = jnp.arange(S, dtype=F32)[:, None] * inv[None, :]
    cos, sin = jnp.cos(ang), jnp.sin(ang)
    rest = LANES - ROPE_DIMS
    ca = jnp.concatenate([cos, cos, jnp.ones((S, rest), F32)], axis=-1)
    sn = jnp.concatenate([-sin, sin, jnp.zeros((S, rest), F32)], axis=-1)
    q_scale = C_HEAD_DIM ** -0.5 * LOG2E
    return jnp.stack([ca * q_scale, ca]), jnp.stack([sn * q_scale, sn])


def _overlap_matrix(n_cmp_rows, n_slc):
    ci = jnp.arange(n_cmp_rows)[:, None] * CMP_STRIDE
    sj = jnp.arange(n_slc)[None, :] * SEL_LEN
    ov = jnp.clip(jnp.minimum(ci + CMP_LEN, sj + SEL_LEN) - jnp.maximum(ci, sj), 0).astype(F32) / CMP_LEN
    return ov


def _gmlp_mlstm_layer(h, mix_g, w_in, gate_b, g_norm, g_ws, g_bs, conv_w, m_norm, w_out, batch):
    w_main = w_in.astype(BF16)
    w_gate = jnp.pad(w_in[:, AB_MAIN:], ((0, 0), (0, LANES - 2 * B_HEADS))).astype(BF16)
    p, gates = _ab_proj(h, mix_g.reshape(1, D_MODEL), w_main, w_gate)
    y_a = _gmlp(p, g_norm.reshape(1, A_WIDTH), g_ws, g_bs.T)
    gb = jnp.pad(gate_b.reshape(1, 2 * B_HEADS), ((0, 0), (0, LANES - 2 * B_HEADS)))
    y_b = _mlstm(p, gates, gb, conv_w, m_norm.reshape(1, B_WIDTH), batch)
    return _outproj(h, [y_a, y_b], w_out.astype(BF16))


def _nsa_layer(h, mix_g, w_in, cmp_pos, cmp_w1, cmp_w2, w_out, batch):
    T = h.shape[0]
    S = T // batch
    w_main = w_in.astype(BF16)
    wg = jnp.pad(w_in[:, C_MAIN:], ((0, 0), (0, LANES - 3 * C_HEADS))).astype(BF16)
    ca, sn = _rope_tables(S)
    qkv, gates = _nsa_proj(h, mix_g.reshape(1, D_MODEL), w_main, wg, ca, sn, batch)
    cmp_kv = _compress(qkv, cmp_pos, cmp_w1.reshape(2, CMP_LEN, C_HEAD_DIM, C_HEAD_DIM).astype(BF16),
                       cmp_w2.astype(BF16))
    overlap_t = _overlap_matrix(S // CMP_STRIDE, S // SEL_LEN).T
    y = _nsa_attn(qkv, cmp_kv, gates, overlap_t)
    return _outproj(h, [y], w_out.astype(BF16))


def kernel(x, ffn_norm, ffn_w_gate, ffn_w_up, ffn_w_down, mix_norm, ab_w_in, mlstm_gate_bias, gmlp_norm, gmlp_w_s,
           gmlp_b_s, mlstm_conv, mlstm_norm, ab_w_out, nsa_w_in, nsa_cmp_pos, nsa_cmp_w1, nsa_cmp_w2, nsa_w_out,
           final_norm):
    batch, S, _ = x.shape
    depth = ffn_norm.shape[0]
    h = x.reshape(batch * S, D_MODEL)
    fg = final_norm.reshape(1, D_MODEL)

    wg, wu, wd = ffn_w_gate.astype(BF16), ffn_w_up.astype(BF16), ffn_w_down.astype(BF16)

    def ffn(h, layer, half, final=False):
        return _ffn(h, ffn_norm[layer, half].reshape(1, D_MODEL), wg, wu, wd, fg, layer, half, final=final)

    for layer in range(depth):
        j = layer // 2
        h = ffn(h, layer, 0)
        if layer % 2 == 0:
            h = _gmlp_mlstm_layer(h, mix_norm[layer], ab_w_in[j], mlstm_gate_bias[j], gmlp_norm[j], gmlp_w_s[j],
                                  gmlp_b_s[j], mlstm_conv[j], mlstm_norm[j], ab_w_out[j], batch)
        else:
            h = _nsa_layer(h, mix_norm[layer], nsa_w_in[j], nsa_cmp_pos[j], nsa_cmp_w1[j], nsa_cmp_w2[j],
                           nsa_w_out[j], batch)
        h = ffn(h, layer, 1, final=(layer == depth - 1))
    return h.reshape(batch, S, D_MODEL)
```

```python
import functools
import math

import jax
import jax.numpy as jnp
from jax import lax
from jax.experimental import pallas as pl
from jax.experimental.pallas import tpu as pltpu

F32 = jnp.float32
BF16 = jnp.bfloat16

D_MODEL = 2048
D_FF = 5632
EPS = 1e-6

A_WIDTH = D_MODEL // 2
A_CHUNK = 128
A_GROUPS = 8
A_GROUP_DIM = A_WIDTH // A_GROUPS

B_HEADS = 4
B_WIDTH = D_MODEL // 2
B_HEAD_DIM = B_WIDTH // B_HEADS
B_CHUNK = 128
B_CONV = 4
AB_MAIN = 2 * A_WIDTH + 4 * B_WIDTH

C_HEADS = 16
C_KV = 4
C_REP = C_HEADS // C_KV
C_HEAD_DIM = D_MODEL // C_HEADS
CMP_LEN = 32
CMP_STRIDE = 16
SEL_LEN = 64
SEL_SHIFT = SEL_LEN.bit_length() - 1
SEL_TOPK = 16
WINDOW = 512
C_MAIN = C_HEADS * C_HEAD_DIM + 6 * C_KV * C_HEAD_DIM
ROPE_THETA = 500000.0
ROPE_DIMS = C_HEAD_DIM // 4
NEG_INF = -1e30
BIG = 1e9
LOG2E = math.log2(math.e)

LANES = 128
SUBLANES = 8
VMEM_LIMIT = 60 * 1024 * 1024

HM_HEADS = C_HEADS + 6 * C_KV
HM_KCMP, HM_VCMP, HM_KSEL, HM_VSEL, HM_KWIN, HM_VWIN = (C_HEADS + i * C_KV for i in range(6))


def _params(*sem):
    return pltpu.CompilerParams(dimension_semantics=sem, vmem_limit_bytes=VMEM_LIMIT)


def _rms(x, g):
    return x * lax.rsqrt(jnp.mean(x * x, axis=-1, keepdims=True) + EPS) * g


def _dot(a, b):
    return jnp.dot(a, b, preferred_element_type=F32)


def _dot_nt(a, b):
    return lax.dot_general(a, b, (((1,), (1,)), ((), ())), preferred_element_type=F32)


def _dot_tn(a, b):
    return lax.dot_general(a, b, (((0,), (0,)), ((), ())), preferred_element_type=F32)


def _dot_f32(a, b):
    return jnp.dot(a, b, preferred_element_type=F32, precision=lax.Precision.HIGHEST)


def _ffn_kernel(h_ref, g_ref, wg_ref, wu_ref, wd_ref, fg_ref, o_ref, n_ref, *, final):
    j = pl.program_id(1)

    @pl.when(j == 0)
    def _():
        x = h_ref[...]
        n_ref[...] = _rms(x, g_ref[...]).astype(BF16)
        o_ref[...] = x

    n = n_ref[...]
    a = _dot(n, wg_ref[...])
    b = _dot(n, wu_ref[...])
    t = (a * jax.nn.sigmoid(a) * (0.5 * b)).astype(BF16)
    o_ref[...] += _dot(t, wd_ref[...])

    if final:
        @pl.when(j == pl.num_programs(1) - 1)
        def _():
            o_ref[...] = _rms(o_ref[...], fg_ref[...])


def _ffn(h, g, wg, wu, wd, fg, layer, half, *, final, tm=1024, tf=512):
    T = h.shape[0]
    return pl.pallas_call(
        functools.partial(_ffn_kernel, final=final),
        grid=(T // tm, D_FF // tf),
        in_specs=[
            pl.BlockSpec((tm, D_MODEL), lambda i, j: (i, 0)),
            pl.BlockSpec((1, D_MODEL), lambda i, j: (0, 0)),
            pl.BlockSpec((None, None, D_MODEL, tf), lambda i, j: (layer, half, 0, j)),
            pl.BlockSpec((None, None, D_MODEL, tf), lambda i, j: (layer, half, 0, j)),
            pl.BlockSpec((None, None, tf, D_MODEL), lambda i, j: (layer, half, j, 0)),
            pl.BlockSpec((1, D_MODEL), lambda i, j: (0, 0)),
        ],
        out_specs=pl.BlockSpec((tm, D_MODEL), lambda i, j: (i, 0)),
        out_shape=jax.ShapeDtypeStruct((T, D_MODEL), F32),
        scratch_shapes=[pltpu.VMEM((tm, D_MODEL), BF16)],
        compiler_params=_params("parallel", "arbitrary"),
        name="ffn_final" if final else "ffn",
    )(h, g, wg, wu, wd, fg)


def _ab_proj_kernel(h_ref, g_ref, w_ref, wgate_ref, o_ref, og_ref, n_ref):
    j = pl.program_id(1)

    @pl.when(j == 0)
    def _():
        n = _rms(h_ref[...], g_ref[...]).astype(BF16)
        n_ref[...] = n
        og_ref[...] = _dot(n, wgate_ref[...])

    o_ref[...] = _dot(n_ref[...], w_ref[...])


def _ab_proj(h, g, w, wgate, *, tm=1024, tn=2048):
    T = h.shape[0]
    return pl.pallas_call(
        _ab_proj_kernel,
        grid=(T // tm, AB_MAIN // tn),
        in_specs=[
            pl.BlockSpec((tm, D_MODEL), lambda i, j: (i, 0)),
            pl.BlockSpec((1, D_MODEL), lambda i, j: (0, 0)),
            pl.BlockSpec((D_MODEL, tn), lambda i, j: (0, j)),
            pl.BlockSpec((D_MODEL, LANES), lambda i, j: (0, 0)),
        ],
        out_specs=[
            pl.BlockSpec((tm, tn), lambda i, j: (i, j)),
            pl.BlockSpec((tm, LANES), lambda i, j: (i, 0)),
        ],
        out_shape=[
            jax.ShapeDtypeStruct((T, AB_MAIN), F32),
            jax.ShapeDtypeStruct((T, LANES), F32),
        ],
        scratch_shapes=[pltpu.VMEM((tm, D_MODEL), BF16)],
        compiler_params=_params("parallel", "arbitrary"),
        name="ab_proj",
    )(h, g, w, wgate)


def _gmlp_kernel(u_ref, v_ref, g_ref, ws_ref, bs_ref, o_ref):
    zu = jax.nn.gelu(u_ref[...])
    vn = _rms(jax.nn.gelu(v_ref[...]), g_ref[...]).astype(BF16)
    row = lax.broadcasted_iota(jnp.int32, (A_CHUNK, A_CHUNK), 0)
    col = lax.broadcasted_iota(jnp.int32, (A_CHUNK, A_CHUNK), 1)
    causal = col <= row
    bs = bs_ref[...]
    for gi in range(A_GROUPS):
        sl = slice(gi * A_GROUP_DIM, (gi + 1) * A_GROUP_DIM)
        w = jnp.where(causal, ws_ref[gi], 0.0).astype(BF16)
        for c in range(u_ref.shape[0] // A_CHUNK):
            rows = slice(c * A_CHUNK, (c + 1) * A_CHUNK)
            sv = _dot(w, vn[rows, sl]) + bs[:, gi:gi + 1]
            o_ref[rows, sl] = (zu[rows, sl] * sv).astype(BF16)


def _gmlp(p, g, ws, bs_t, *, chunks_per_step=8):
    T = p.shape[0]
    tm = chunks_per_step * A_CHUNK
    return pl.pallas_call(
        _gmlp_kernel,
        grid=(T // tm,),
        in_specs=[
            pl.BlockSpec((tm, A_WIDTH), lambda c: (c, 0)),
            pl.BlockSpec((tm, A_WIDTH), lambda c: (c, 1)),
            pl.BlockSpec((1, A_WIDTH), lambda c: (0, 0)),
            pl.BlockSpec((A_GROUPS, A_CHUNK, A_CHUNK), lambda c: (0, 0, 0)),
            pl.BlockSpec((A_CHUNK, A_GROUPS), lambda c: (0, 0)),
        ],
        out_specs=pl.BlockSpec((tm, A_WIDTH), lambda c: (c, 0)),
        out_shape=jax.ShapeDtypeStruct((T, A_WIDTH), BF16),
        compiler_params=_params("parallel"),
        name="gmlp",
    )(p, p, g, ws, bs_t)


def _log_sigmoid(x):
    return jnp.minimum(x, 0.0) - jnp.log1p(jnp.exp(-jnp.abs(x)))


def _mlstm_kernel(qk_ref, v_ref, o_ref, gate_ref, gb_ref, cw_ref, mn_ref, y_ref,
                  xx_ref, c_ref, n_ref, m_ref):
    L, d = B_CHUNK, B_HEAD_DIM
    tail = SUBLANES

    @pl.when(pl.program_id(1) == 0)
    def _():
        xx_ref[...] = jnp.zeros_like(xx_ref)
        c_ref[...] = jnp.zeros_like(c_ref)
        n_ref[...] = jnp.zeros_like(n_ref)
        m_ref[...] = jnp.zeros_like(m_ref)

    x = qk_ref[...]
    prev = xx_ref[...]
    cw = cw_ref[...]
    sub = lax.broadcasted_iota(jnp.int32, (tail, 2 * B_WIDTH), 0)
    conv = cw[B_CONV - 1:B_CONV, :] * x
    for k in range(1, B_CONV):
        r = pltpu.roll(x, k, 0)
        head = jnp.where(sub < k, pltpu.roll(prev, k, 0), r[0:tail])
        conv = conv + cw[B_CONV - 1 - k:B_CONV - k, :] * jnp.concatenate([head, r[tail:]], axis=0)
    xx_ref[...] = x[L - tail:L]
    qk = conv * jax.nn.sigmoid(conv)

    gcol = gate_ref[...] + gb_ref[...]
    grow = gcol.T
    r_i = lax.broadcasted_iota(jnp.int32, (L, L), 0)
    c_i = lax.broadcasted_iota(jnp.int32, (L, L), 1)
    causal = c_i <= r_i
    tril = causal.astype(F32)
    triu = (r_i <= c_i).astype(F32)
    bcol_all = _dot_f32(tril, _log_sigmoid(gcol))
    brow_all = _dot_f32(_log_sigmoid(grow), triu)

    for hh in range(B_HEADS):
        q = qk[:, hh * d:(hh + 1) * d]
        k = qk[:, B_WIDTH + hh * d:B_WIDTH + (hh + 1) * d] * (d ** -0.5)
        v = v_ref[:, hh * d:(hh + 1) * d]
        qb, kb, vb = q.astype(BF16), k.astype(BF16), v.astype(BF16)
        i_col = gcol[:, hh:hh + 1]
        i_row = grow[hh:hh + 1, :]
        b_col = bcol_all[:, B_HEADS + hh:B_HEADS + hh + 1]
        b_row = brow_all[B_HEADS + hh:B_HEADS + hh + 1, :]
        b_last = b_col[L - 1:L, :]
        m_old = m_ref[hh:hh + 1, 0:1]

        logd = jnp.where(causal, b_col - b_row + i_row, NEG_INF)
        inter = b_col + m_old
        m_t = jnp.maximum(jnp.max(logd, axis=-1, keepdims=True), inter)
        dm = jnp.where(causal, jnp.exp(logd - m_t), 0.0)
        w_inter = jnp.exp(inter - m_t)
        s = _dot_nt(qb, kb) * dm
        ct = c_ref[hh]
        nvec = n_ref[hh:hh + 1, :]
        num = _dot(s.astype(BF16), vb) + w_inter * _dot(qb, ct.astype(BF16))
        den = jnp.sum(s, axis=-1, keepdims=True) + w_inter * jnp.sum(q * nvec, axis=-1, keepdims=True)
        hout = num / jnp.maximum(jnp.abs(den), jnp.exp(-m_t))

        m_new = m_t[L - 1:L, :]
        w_s = jnp.exp(b_last - b_col + i_col - m_new)
        w_prev = jnp.exp(b_last + m_old - m_new)
        c_ref[hh] = w_prev * ct + _dot_tn(kb, (w_s * v).astype(BF16))
        n_ref[hh:hh + 1, :] = w_prev * nvec + jnp.sum(w_s * k, axis=0, keepdims=True)
        m_ref[hh:hh + 1, :] = jnp.broadcast_to(m_new, (1, LANES))

        hn = _rms(hout, mn_ref[:, hh * d:(hh + 1) * d])
        og = jax.nn.sigmoid(o_ref[:, hh * d:(hh + 1) * d])
        y_ref[:, hh * d:(hh + 1) * d] = (og * hn).astype(BF16)


def _mlstm(p, gates, gate_b, conv_w, m_norm, batch):
    T = p.shape[0]
    L = B_CHUNK
    nc = T // batch // L
    qk_blk = (2 * A_WIDTH) // (2 * B_WIDTH)
    v_blk = (2 * A_WIDTH + 2 * B_WIDTH) // B_WIDTH
    row = lambda b, c: b * nc + c
    return pl.pallas_call(
        _mlstm_kernel,
        grid=(batch, nc),
        in_specs=[
            pl.BlockSpec((L, 2 * B_WIDTH), lambda b, c: (row(b, c), qk_blk)),
            pl.BlockSpec((L, B_WIDTH), lambda b, c: (row(b, c), v_blk)),
            pl.BlockSpec((L, B_WIDTH), lambda b, c: (row(b, c), v_blk + 1)),
            pl.BlockSpec((L, LANES), lambda b, c: (row(b, c), 0)),
            pl.BlockSpec((1, LANES), lambda b, c: (0, 0)),
            pl.BlockSpec((B_CONV, 2 * B_WIDTH), lambda b, c: (0, 0)),
            pl.BlockSpec((1, B_WIDTH), lambda b, c: (0, 0)),
        ],
        out_specs=pl.BlockSpec((L, B_WIDTH), lambda b, c: (row(b, c), 0)),
        out_shape=jax.ShapeDtypeStruct((T, B_WIDTH), BF16),
        scratch_shapes=[
            pltpu.VMEM((SUBLANES, 2 * B_WIDTH), F32),
            pltpu.VMEM((B_HEADS, B_HEAD_DIM, B_HEAD_DIM), F32),
            pltpu.VMEM((SUBLANES, B_HEAD_DIM), F32),
            pltpu.VMEM((SUBLANES, LANES), F32),
        ],
        compiler_params=_params("arbitrary", "arbitrary"),
        name="mlstm",
    )(p, p, p, gates, gate_b, conv_w, m_norm)


def _outproj_kernel(*refs, n_lhs):
    h_ref = refs[0]
    o_ref = refs[1 + 2 * n_lhs]
    acc = h_ref[...]
    for i in range(n_lhs):
        acc = acc + _dot(refs[1 + i][...], refs[1 + n_lhs + i][...])
    o_ref[...] = acc


def _outproj(h, lhs, w, *, tm=1024):
    T = h.shape[0]
    n = len(lhs)
    width = lhs[0].shape[1]
    assert all(a.shape[1] == width for a in lhs) and w.shape[0] == n * width
    in_specs = [pl.BlockSpec((tm, D_MODEL), lambda i: (i, 0))]
    in_specs += [pl.BlockSpec((tm, width), lambda i: (i, 0)) for _ in lhs]
    in_specs += [pl.BlockSpec((width, D_MODEL), functools.partial(lambda i, k: (k, 0), k=k)) for k in range(n)]
    return pl.pallas_call(
        functools.partial(_outproj_kernel, n_lhs=n),
        grid=(T // tm,),
        in_specs=in_specs,
        out_specs=pl.BlockSpec((tm, D_MODEL), lambda i: (i, 0)),
        out_shape=jax.ShapeDtypeStruct((T, D_MODEL), F32),
        compiler_params=_params("parallel"),
        name="outproj",
    )(h, *lhs, *([w] * n))


def _rope(x, ca, sn):
    half = ROPE_DIMS // 2
    lane = lax.broadcasted_iota(jnp.int32, x.shape, 1)
    partner = jnp.where(lane < half, lane + half, jnp.where(lane < ROPE_DIMS, lane - half, lane))
    return x * ca + jnp.take_along_axis(x, partner, axis=1) * sn


def _nsa_proj_kernel(h_ref, g_ref, w_ref, wgate_ref, ca_ref, sn_ref, o_ref, og_ref, n_ref):
    j = pl.program_id(1)

    @pl.when(j == 0)
    def _():
        n = _rms(h_ref[...], g_ref[...]).astype(BF16)
        n_ref[...] = n
        og_ref[...] = jax.nn.sigmoid(_dot(n, wgate_ref[...]))

    res = _dot(n_ref[...], w_ref[...])
    heads = o_ref.shape[1]
    is_q = j < C_HEADS // heads

    def head(k):
        return res[:, k * LANES:(k + 1) * LANES]

    @pl.when(is_q)
    def _():
        ca, sn = ca_ref[0], sn_ref[0]
        for k in range(heads):
            o_ref[0, k] = _rope(head(k), ca, sn).astype(BF16)

    @pl.when(jnp.logical_not(is_q))
    def _():
        ca, sn = ca_ref[0], sn_ref[0]
        for k in range(C_KV):
            o_ref[0, k] = _rope(head(k), ca, sn).astype(BF16)
            o_ref[0, C_KV + k] = head(C_KV + k).astype(BF16)


def _nsa_proj(h, g, w, wgate, ca, sn, batch, *, tm=1024):
    T = h.shape[0]
    S = T // batch
    nts = S // tm
    heads = 2 * C_KV
    tn = heads * C_HEAD_DIM
    n_q = C_HEADS // heads
    table = pl.BlockSpec((1, tm, LANES), lambda i, j: (jnp.where(j < n_q, 0, 1), i % nts, 0))
    return pl.pallas_call(
        _nsa_proj_kernel,
        grid=(T // tm, C_MAIN // tn),
        in_specs=[
            pl.BlockSpec((tm, D_MODEL), lambda i, j: (i, 0)),
            pl.BlockSpec((1, D_MODEL), lambda i, j: (0, 0)),
            pl.BlockSpec((D_MODEL, tn), lambda i, j: (0, j)),
            pl.BlockSpec((D_MODEL, LANES), lambda i, j: (0, 0)),
            table, table,
        ],
        out_specs=[
            pl.BlockSpec((1, heads, tm, C_HEAD_DIM), lambda i, j: (i // nts, j, i % nts, 0)),
            pl.BlockSpec((tm, LANES), lambda i, j: (i, 0)),
        ],
        out_shape=[
            jax.ShapeDtypeStruct((batch, HM_HEADS, S, C_HEAD_DIM), BF16),
            jax.ShapeDtypeStruct((T, LANES), F32),
        ],
        scratch_shapes=[pltpu.VMEM((tm, D_MODEL), BF16)],
        compiler_params=_params("parallel", "arbitrary"),
        name="nsa_proj",
    )(h, g, w, wgate, ca, sn)


def _compress_kernel(x_ref, pos_ref, w1_ref, w2_ref, o_ref, xs_ref):
    xs_ref[...] = x_ref[0].astype(F32)
    G, S, dh = xs_ref.shape
    n_rows = S // CMP_STRIDE
    ya = jnp.zeros((G * n_rows, dh), F32)
    yb = jnp.zeros((G * n_rows, dh), F32)
    for l in range(CMP_STRIDE):
        xl = xs_ref[:, pl.ds(l, n_rows, stride=CMP_STRIDE), :].reshape(G * n_rows, dh)
        ya = ya + _dot((xl + pos_ref[0, l:l + 1, :]).astype(BF16), w1_ref[0, l])
        lb = CMP_STRIDE + l
        yb = yb + _dot((xl + pos_ref[0, lb:lb + 1, :]).astype(BF16), w1_ref[0, lb])
    pre = ya + pltpu.roll(yb, G * n_rows - 1, 0)
    out = _dot(jax.nn.gelu(pre).astype(BF16), w2_ref[0]).reshape(G, n_rows, dh)
    rows = lax.broadcasted_iota(jnp.int32, out.shape, 1)
    o_ref[0, 0] = jnp.where(rows < n_rows - 1, out, 0.0).astype(BF16)


def _compress(qkv, pos, w1, w2):
    B_, _, S, dh = qkv.shape
    rows = S // CMP_STRIDE
    return pl.pallas_call(
        _compress_kernel,
        grid=(B_, 2),
        in_specs=[
            pl.BlockSpec((1, C_KV, S, dh), lambda b, kv: (b, HM_KCMP // C_KV + kv, 0, 0)),
            pl.BlockSpec((1, CMP_LEN, dh), lambda b, kv: (kv, 0, 0)),
            pl.BlockSpec((1, CMP_LEN, dh, dh), lambda b, kv: (kv, 0, 0, 0)),
            pl.BlockSpec((1, dh, dh), lambda b, kv: (kv, 0, 0)),
        ],
        out_specs=pl.BlockSpec((1, 1, C_KV, rows, dh), lambda b, kv: (b, kv, 0, 0, 0)),
        out_shape=jax.ShapeDtypeStruct((B_, 2, C_KV, rows, dh), BF16),
        scratch_shapes=[pltpu.VMEM((C_KV, S, dh), F32)],
        compiler_params=_params("parallel", "parallel"),
        name="nsa_compress",
    )(qkv, pos, w1, w2)


def _softmax_parts_t(s):
    m = jnp.max(s, axis=0, keepdims=True)
    p = jnp.exp2(s - m)
    return p, jnp.sum(p, axis=0, keepdims=True)


def _topk_membership_t(imp_t, n_top):
    n_blk, tq = imp_t.shape
    groups = n_blk // SUBLANES
    va = [imp_t[a * SUBLANES:(a + 1) * SUBLANES] for a in range(groups)]
    cnt = [jnp.zeros((SUBLANES, tq), F32) for _ in range(groups)]
    sub = lax.broadcasted_iota(jnp.int32, (SUBLANES, tq), 0)
    for i in range(n_blk):
        vi = jnp.broadcast_to(imp_t[i:i + 1, :], (SUBLANES, tq))
        for a in range(groups):
            if i < a * SUBLANES:
                ahead = jnp.where(vi >= va[a], 1.0, 0.0)
            elif i >= (a + 1) * SUBLANES:
                ahead = jnp.where(vi > va[a], 1.0, 0.0)
            else:
                ahead = jnp.where(sub > i - a * SUBLANES, jnp.where(vi >= va[a], 1.0, 0.0),
                                  jnp.where(vi > va[a], 1.0, 0.0))
            cnt[a] = cnt[a] + ahead
    return jnp.concatenate([jnp.where(c < n_top, 1.0, 0.0) for c in cnt], axis=0)


def _nsa_attn_kernel(q_ref, kc_ref, vc_ref, ks_ref, vs_ref, kw_ref, vw_ref, gate_ref, ovt_ref, y_ref,
                     qx_ref, m_ref, l_ref, acc_ref, oc_ref, ow_ref, *, tq, tk, n_slc, n_top):
    G, R, dh = C_KV, C_REP, C_HEAD_DIM
    t0 = pl.program_id(1) * tq
    pos = t0 + lax.broadcasted_iota(jnp.int32, (1, tq), 1)

    def q_of(g):
        return q_ref[0, g * R:(g + 1) * R].reshape(R * tq, dh)

    def per_head(x):
        return jnp.concatenate([x] * R, axis=1)

    n_cmp_rows = kc_ref.shape[3]
    cmp_end = lax.broadcasted_iota(jnp.int32, (n_cmp_rows, tq), 0) * CMP_STRIDE + (CMP_LEN - 1)
    ok_c = jnp.logical_and(cmp_end <= pos, cmp_end < n_cmp_rows * CMP_STRIDE)
    bias_c = per_head(jnp.where(ok_c, 0.0, NEG_INF))
    has_c = per_head(pos >= CMP_LEN - 1)
    slab = WINDOW + tq
    w0 = pl.multiple_of(jnp.maximum(t0 - WINDOW, 0), tq)
    kpos_w = w0 + lax.broadcasted_iota(jnp.int32, (slab, tq), 0)
    ok_w = jnp.logical_and(kpos_w <= pos, pos - kpos_w < WINDOW)
    bias_w = per_head(jnp.where(ok_w, 0.0, NEG_INF))
    blk_t = lax.broadcasted_iota(jnp.int32, (n_slc, tq), 0)
    cur_t = jnp.right_shift(pos, SEL_SHIFT)
    forced = jnp.logical_or(blk_t == 0, jnp.logical_or(blk_t == cur_t, blk_t == cur_t - 1))
    future = blk_t > cur_t

    s_cmp = [_dot_nt(kc_ref[0, 0, g], q_of(g)) + bias_c for g in range(G)]
    s_win = [_dot_nt(kw_ref[0, g, pl.ds(w0, slab), :], q_of(g)) + bias_w for g in range(G)]
    for g in range(G):
        p_c, l_c = _softmax_parts_t(s_cmp[g])
        p_c = p_c * jnp.where(has_c, 1.0 / l_c, 0.0)
        oc_ref[g] = _dot_tn(vc_ref[0, 0, g], p_c.astype(BF16))

        p_w, l_w = _softmax_parts_t(s_win[g])
        ow_ref[g] = _dot_tn(vw_ref[0, g, pl.ds(w0, slab), :], p_w.astype(BF16)) * (1.0 / l_w)

        p_sum = p_c[:, 0:tq]
        for r in range(1, R):
            p_sum = p_sum + p_c[:, r * tq:(r + 1) * tq]
        imp_t = jnp.where(forced, BIG, jnp.where(future, -BIG, _dot_f32(ovt_ref[...], p_sum)))
        sel_t = _topk_membership_t(imp_t, n_top)
        sel = jnp.concatenate([sel_t, jnp.zeros((LANES - n_slc, tq), F32)], axis=0).T
        penalty = jnp.where(sel > 0.5, 0.0, NEG_INF).astype(BF16)
        qx_ref[g] = jnp.concatenate([q_of(g), jnp.concatenate([penalty] * R, axis=0)], axis=1)

    m_ref[...] = jnp.full(m_ref.shape, NEG_INF, F32)
    l_ref[...] = jnp.zeros(l_ref.shape, F32)
    acc_ref[...] = jnp.zeros(acc_ref.shape, F32)
    blocks_per_tile = tk // SEL_LEN

    def sweep_tile(kt, diagonal):
        k0 = pl.multiple_of(kt * tk, tk)
        key_blk = kt * blocks_per_tile + jnp.right_shift(lax.broadcasted_iota(jnp.int32, (tk, LANES), 0), SEL_SHIFT)
        onehot = jnp.where(lax.broadcasted_iota(jnp.int32, (tk, LANES), 1) == key_blk, 1.0, 0.0).astype(BF16)
        s_all = []
        for g in range(G):
            k_ext = jnp.concatenate([ks_ref[0, g, pl.ds(k0, tk), :], onehot], axis=1)
            s = _dot_nt(k_ext, qx_ref[g])
            if diagonal:
                causal = k0 + lax.broadcasted_iota(jnp.int32, (tk, tq), 0) <= pos
                s = s + per_head(jnp.where(causal, 0.0, NEG_INF))
            s_all.append(s)
        for g, s in enumerate(s_all):
            m_old = m_ref[g]
            m_new = jnp.maximum(m_old, jnp.max(s, axis=0, keepdims=True))
            p = jnp.exp2(s - m_new)
            alpha = jnp.exp2(m_old - m_new)
            l_ref[g] = alpha * l_ref[g] + jnp.sum(p, axis=0, keepdims=True)
            acc_ref[g] = alpha * acc_ref[g] + _dot_tn(vs_ref[0, g, pl.ds(k0, tk), :], p.astype(BF16))
            m_ref[g] = m_new

    def sweep(kt, carry):
        sweep_tile(kt, diagonal=False)
        return carry

    last_tile = t0 // tk
    lax.fori_loop(0, last_tile, sweep, 0)
    sweep_tile(last_tile, diagonal=True)

    gates_t = gate_ref[...].T
    for g in range(G):
        os_t = acc_ref[g] * (1.0 / l_ref[g])
        oc_t, ow_t = oc_ref[g], ow_ref[g]
        for r in range(R):
            cols = slice(r * tq, (r + 1) * tq)
            hd = g * R + r
            merged_t = (gates_t[hd:hd + 1, :] * oc_t[:, cols]
                        + gates_t[C_HEADS + hd:C_HEADS + hd + 1, :] * os_t[:, cols]
                        + gates_t[2 * C_HEADS + hd:2 * C_HEADS + hd + 1, :] * ow_t[:, cols])
            y_ref[:, (g * R + r) * dh:(g * R + r + 1) * dh] = merged_t.T.astype(BF16)


def _nsa_attn(qkv, cmp_kv, gates, overlap_t, *, tq=256, tk=512):
    B_, _, S, dh = qkv.shape
    nq = S // tq
    n_slc = S // SEL_LEN
    n_top = min(SEL_TOPK, n_slc)
    assert n_top >= 3 and n_slc % SUBLANES == 0
    tk = min(tk, S)
    assert tq % LANES == 0 and tk % tq == 0 and n_slc <= LANES
    n_cmp_rows = cmp_kv.shape[3]
    G, R = C_KV, C_REP
    seq = lambda base: pl.BlockSpec((1, G, S, dh), lambda b, i: (b, base // G, 0, 0), pipeline_mode=pl.Buffered(1))
    return pl.pallas_call(
        functools.partial(_nsa_attn_kernel, tq=tq, tk=tk, n_slc=n_slc, n_top=n_top),
        grid=(B_, nq),
        in_specs=[
            pl.BlockSpec((1, C_HEADS, tq, dh), lambda b, i: (b, 0, i, 0)),
            pl.BlockSpec((1, 1, G, n_cmp_rows, dh), lambda b, i: (b, 0, 0, 0, 0)),
            pl.BlockSpec((1, 1, G, n_cmp_rows, dh), lambda b, i: (b, 1, 0, 0, 0)),
            seq(HM_KSEL), seq(HM_VSEL), seq(HM_KWIN), seq(HM_VWIN),
            pl.BlockSpec((tq, LANES), lambda b, i: (b * nq + i, 0)),
            pl.BlockSpec((n_slc, n_cmp_rows), lambda b, i: (0, 0)),
        ],
        out_specs=pl.BlockSpec((tq, C_HEADS * dh), lambda b, i: (b * nq + i, 0)),
        out_shape=jax.ShapeDtypeStruct((B_ * S, C_HEADS * dh), BF16),
        scratch_shapes=[
            pltpu.VMEM((G, R * tq, dh + LANES), BF16),
            pltpu.VMEM((G, 1, R * tq), F32),
            pltpu.VMEM((G, 1, R * tq), F32),
            pltpu.VMEM((G, dh, R * tq), F32),
            pltpu.VMEM((G, dh, R * tq), F32),
            pltpu.VMEM((G, dh, R * tq), F32),
        ],
        compiler_params=_params("parallel", "arbitrary"),
        name="nsa_attn",
    )(qkv, cmp_kv, cmp_kv, qkv, qkv, qkv, qkv, gates, overlap_t)


def _rope_tables(S):
    half = ROPE_DIMS // 2
    inv = 1.0 / (ROPE_THETA ** (jnp.arange(half, dtype=F32) / half))
    ang = jnp.arange(S, dtype=F32)[:, None] * inv[None, :]
    cos, sin = jnp.cos(ang), jnp.sin(ang)
    rest = LANES - ROPE_DIMS
    ca = jnp.concatenate([cos, cos, jnp.ones((S, rest), F32)], axis=-1)
    sn = jnp.concatenate([-sin, sin, jnp.zeros((S, rest), F32)], axis=-1)
    q_scale = C_HEAD_DIM ** -0.5 * LOG2E
    return jnp.stack([ca * q_scale, ca]), jnp.stack([sn * q_scale, sn])


def _overlap_matrix(n_cmp_rows, n_slc):
    ci = jnp.arange(n_cmp_rows)[:, None] * CMP_STRIDE
    sj = jnp.arange(n_slc)[None, :] * SEL_LEN
    ov = jnp.clip(jnp.minimum(ci + CMP_LEN, sj + SEL_LEN) - jnp.maximum(ci, sj), 0).astype(F32) / CMP_LEN
    return ov


def _gmlp_mlstm_layer(h, mix_g, w_in, gate_b, g_norm, g_ws, g_bs, conv_w, m_norm, w_out, batch):
    w_main = w_in.astype(BF16)
    w_gate = jnp.pad(w_in[:, AB_MAIN:], ((0, 0), (0, LANES - 2 * B_HEADS))).astype(BF16)
    p, gates = _ab_proj(h, mix_g.reshape(1, D_MODEL), w_main, w_gate)
    y_a = _gmlp(p, g_norm.reshape(1, A_WIDTH), g_ws, g_bs.T)
    gb = jnp.pad(gate_b.reshape(1, 2 * B_HEADS), ((0, 0), (0, LANES - 2 * B_HEADS)))
    y_b = _mlstm(p, gates, gb, conv_w, m_norm.reshape(1, B_WIDTH), batch)
    return _outproj(h, [y_a, y_b], w_out.astype(BF16))


def _nsa_layer(h, mix_g, w_in, cmp_pos, cmp_w1, cmp_w2, w_out, batch):
    T = h.shape[0]
    S = T // batch
    w_main = w_in.astype(BF16)
    wg = jnp.pad(w_in[:, C_MAIN:], ((0, 0), (0, LANES - 3 * C_HEADS))).astype(BF16)
    ca, sn = _rope_tables(S)
    qkv, gates = _nsa_proj(h, mix_g.reshape(1, D_MODEL), w_main, wg, ca, sn, batch)
    cmp_kv = _compress(qkv, cmp_pos, cmp_w1.reshape(2, CMP_LEN, C_HEAD_DIM, C_HEAD_DIM).astype(BF16),
                       cmp_w2.astype(BF16))
    overlap_t = _overlap_matrix(S // CMP_STRIDE, S // SEL_LEN).T
    y = _nsa_attn(qkv, cmp_kv, gates, overlap_t)
    return _outproj(h, [y], w_out.astype(BF16))


def kernel(x, ffn_norm, ffn_w_gate, ffn_w_up, ffn_w_down, mix_norm, ab_w_in, mlstm_gate_bias, gmlp_norm, gmlp_w_s,
           gmlp_b_s, mlstm_conv, mlstm_norm, ab_w_out, nsa_w_in, nsa_cmp_pos, nsa_cmp_w1, nsa_cmp_w2, nsa_w_out,
           final_norm):
    batch, S, _ = x.shape
    depth = ffn_norm.shape[0]
    h = x.reshape(batch * S, D_MODEL)
    fg = final_norm.reshape(1, D_MODEL)

    wg, wu, wd = ffn_w_gate.astype(BF16), ffn_w_up.astype(BF16), ffn_w_down.astype(BF16)

    def ffn(h, layer, half, final=False):
        return _ffn(h, ffn_norm[layer, half].reshape(1, D_MODEL), wg, wu, wd, fg, layer, half, final=final)

    for layer in range(depth):
        j = layer // 2
        h = ffn(h, layer, 0)
        if layer % 2 == 0:
            h = _gmlp_mlstm_layer(h, mix_norm[layer], ab_w_in[j], mlstm_gate_bias[j], gmlp_norm[j], gmlp_w_s[j],
                                  gmlp_b_s[j], mlstm_conv[j], mlstm_norm[j], ab_w_out[j], batch)
        else:
            h = _nsa_layer(h, mix_norm[layer], nsa_w_in[j], nsa_cmp_pos[j], nsa_cmp_w1[j], nsa_cmp_w2[j],
                           nsa_w_out[j], batch)
        h = ffn(h, layer, 1, final=(layer == depth - 1))
    return h.reshape(batch, S, D_MODEL)
```

```python
import functools
import math

import jax
import jax.numpy as jnp
from jax import lax
from jax.experimental import pallas as pl
from jax.experimental.pallas import tpu as pltpu

F32 = jnp.float32
BF16 = jnp.bfloat16

D_MODEL = 2048
D_FF = 5632
EPS = 1e-6

A_WIDTH = D_MODEL // 2
A_CHUNK = 128
A_GROUPS = 8
A_GROUP_DIM = A_WIDTH // A_GROUPS

B_HEADS = 4
B_WIDTH = D_MODEL // 2
B_HEAD_DIM = B_WIDTH // B_HEADS
B_CHUNK = 128
B_CONV = 4
AB_MAIN = 2 * A_WIDTH + 4 * B_WIDTH

C_HEADS = 16
C_KV = 4
C_REP = C_HEADS // C_KV
C_HEAD_DIM = D_MODEL // C_HEADS
CMP_LEN = 32
CMP_STRIDE = 16
SEL_LEN = 64
SEL_SHIFT = SEL_LEN.bit_length() - 1
SEL_TOPK = 16
WINDOW = 512
C_MAIN = C_HEADS * C_HEAD_DIM + 6 * C_KV * C_HEAD_DIM
ROPE_THETA = 500000.0
ROPE_DIMS = C_HEAD_DIM // 4
NEG_INF = -1e30
BIG = 1e9
LOG2E = math.log2(math.e)

LANES = 128
SUBLANES = 8
VMEM_LIMIT = 60 * 1024 * 1024

HM_HEADS = C_HEADS + 6 * C_KV
HM_KCMP, HM_VCMP, HM_KSEL, HM_VSEL, HM_KWIN, HM_VWIN = (C_HEADS + i * C_KV for i in range(6))


def _params(*sem):
    return pltpu.CompilerParams(dimension_semantics=sem, vmem_limit_bytes=VMEM_LIMIT)


def _rms(x, g):
    return x * lax.rsqrt(jnp.mean(x * x, axis=-1, keepdims=True) + EPS) * g


def _dot(a, b):
    return jnp.dot(a, b, preferred_element_type=F32)


def _dot_nt(a, b):
    return lax.dot_general(a, b, (((1,), (1,)), ((), ())), preferred_element_type=F32)


def _dot_tn(a, b):
    return lax.dot_general(a, b, (((0,), (0,)), ((), ())), preferred_element_type=F32)


def _dot_f32(a, b):
    return jnp.dot(a, b, preferred_element_type=F32, precision=lax.Precision.HIGHEST)


def _ffn_kernel(h_ref, g_ref, wg_ref, wu_ref, wd_ref, fg_ref, o_ref, n_ref, *, final):
    j = pl.program_id(1)

    @pl.when(j == 0)
    def _():
        x = h_ref[...]
        n_ref[...] = _rms(x, g_ref[...]).astype(BF16)
        o_ref[...] = x

    n = n_ref[...]
    tf = wd_ref.shape[0]
    down = None
    for lo in range(0, tf, tf // 2):
        cols = slice(lo, lo + tf // 2)
        a = _dot(n, wg_ref[:, cols])
        b = _dot(n, wu_ref[:, cols])
        t = (a * jax.nn.sigmoid(a) * (0.5 * b)).astype(BF16)
        part = _dot(t, wd_ref[cols, :])
        down = part if down is None else down + part
    o_ref[...] += down

    if final:
        @pl.when(j == pl.num_programs(1) - 1)
        def _():
            o_ref[...] = _rms(o_ref[...], fg_ref[...])


def _ffn(h, g, wg, wu, wd, fg, layer, half, *, final, tm=1024, tf=512):
    T = h.shape[0]
    return pl.pallas_call(
        functools.partial(_ffn_kernel, final=final),
        grid=(T // tm, D_FF // tf),
        in_specs=[
            pl.BlockSpec((tm, D_MODEL), lambda i, j: (i, 0)),
            pl.BlockSpec((1, D_MODEL), lambda i, j: (0, 0)),
            pl.BlockSpec((None, None, D_MODEL, tf), lambda i, j: (layer, half, 0, j)),
            pl.BlockSpec((None, None, D_MODEL, tf), lambda i, j: (layer, half, 0, j)),
            pl.BlockSpec((None, None, tf, D_MODEL), lambda i, j: (layer, half, j, 0)),
            pl.BlockSpec((1, D_MODEL), lambda i, j: (0, 0)),
        ],
        out_specs=pl.BlockSpec((tm, D_MODEL), lambda i, j: (i, 0)),
        out_shape=jax.ShapeDtypeStruct((T, D_MODEL), F32),
        scratch_shapes=[pltpu.VMEM((tm, D_MODEL), BF16)],
        compiler_params=_params("parallel", "arbitrary"),
        name="ffn_final" if final else "ffn",
    )(h, g, wg, wu, wd, fg)


def _ab_proj_kernel(h_ref, g_ref, w_ref, wgate_ref, o_ref, og_ref, n_ref):
    j = pl.program_id(1)

    @pl.when(j == 0)
    def _():
        n = _rms(h_ref[...], g_ref[...]).astype(BF16)
        n_ref[...] = n
        og_ref[...] = _dot(n, wgate_ref[...])

    o_ref[...] = _dot(n_ref[...], w_ref[...])


def _ab_proj(h, g, w, wgate, *, tm=1024, tn=2048):
    T = h.shape[0]
    return pl.pallas_call(
        _ab_proj_kernel,
        grid=(T // tm, AB_MAIN // tn),
        in_specs=[
            pl.BlockSpec((tm, D_MODEL), lambda i, j: (i, 0)),
            pl.BlockSpec((1, D_MODEL), lambda i, j: (0, 0)),
            pl.BlockSpec((D_MODEL, tn), lambda i, j: (0, j)),
            pl.BlockSpec((D_MODEL, LANES), lambda i, j: (0, 0)),
        ],
        out_specs=[
            pl.BlockSpec((tm, tn), lambda i, j: (i, j)),
            pl.BlockSpec((tm, LANES), lambda i, j: (i, 0)),
        ],
        out_shape=[
            jax.ShapeDtypeStruct((T, AB_MAIN), F32),
            jax.ShapeDtypeStruct((T, LANES), F32),
        ],
        scratch_shapes=[pltpu.VMEM((tm, D_MODEL), BF16)],
        compiler_params=_params("parallel", "arbitrary"),
        name="ab_proj",
    )(h, g, w, wgate)


def _gmlp_kernel(u_ref, v_ref, g_ref, ws_ref, bs_ref, o_ref):
    zu = jax.nn.gelu(u_ref[...])
    vn = _rms(jax.nn.gelu(v_ref[...]), g_ref[...]).astype(BF16)
    row = lax.broadcasted_iota(jnp.int32, (A_CHUNK, A_CHUNK), 0)
    col = lax.broadcasted_iota(jnp.int32, (A_CHUNK, A_CHUNK), 1)
    causal = col <= row
    bs = bs_ref[...]
    for gi in range(A_GROUPS):
        sl = slice(gi * A_GROUP_DIM, (gi + 1) * A_GROUP_DIM)
        w = jnp.where(causal, ws_ref[gi], 0.0).astype(BF16)
        for c in range(u_ref.shape[0] // A_CHUNK):
            rows = slice(c * A_CHUNK, (c + 1) * A_CHUNK)
            sv = _dot(w, vn[rows, sl]) + bs[:, gi:gi + 1]
            o_ref[rows, sl] = (zu[rows, sl] * sv).astype(BF16)


def _gmlp(p, g, ws, bs_t, *, chunks_per_step=8):
    T = p.shape[0]
    tm = chunks_per_step * A_CHUNK
    return pl.pallas_call(
        _gmlp_kernel,
        grid=(T // tm,),
        in_specs=[
            pl.BlockSpec((tm, A_WIDTH), lambda c: (c, 0)),
            pl.BlockSpec((tm, A_WIDTH), lambda c: (c, 1)),
            pl.BlockSpec((1, A_WIDTH), lambda c: (0, 0)),
            pl.BlockSpec((A_GROUPS, A_CHUNK, A_CHUNK), lambda c: (0, 0, 0)),
            pl.BlockSpec((A_CHUNK, A_GROUPS), lambda c: (0, 0)),
        ],
        out_specs=pl.BlockSpec((tm, A_WIDTH), lambda c: (c, 0)),
        out_shape=jax.ShapeDtypeStruct((T, A_WIDTH), BF16),
        compiler_params=_params("parallel"),
        name="gmlp",
    )(p, p, g, ws, bs_t)


def _log_sigmoid(x):
    return jnp.minimum(x, 0.0) - jnp.log1p(jnp.exp(-jnp.abs(x)))


def _mlstm_kernel(qk_ref, v_ref, o_ref, gate_ref, gb_ref, cw_ref, mn_ref, y_ref,
                  xx_ref, c_ref, n_ref, m_ref):
    L, d = B_CHUNK, B_HEAD_DIM
    tail = SUBLANES

    @pl.when(pl.program_id(1) == 0)
    def _():
        xx_ref[...] = jnp.zeros_like(xx_ref)
        c_ref[...] = jnp.zeros_like(c_ref)
        n_ref[...] = jnp.zeros_like(n_ref)
        m_ref[...] = jnp.zeros_like(m_ref)

    x = qk_ref[...]
    prev = xx_ref[...]
    cw = cw_ref[...]
    sub = lax.broadcasted_iota(jnp.int32, (tail, 2 * B_WIDTH), 0)
    conv = cw[B_CONV - 1:B_CONV, :] * x
    for k in range(1, B_CONV):
        r = pltpu.roll(x, k, 0)
        head = jnp.where(sub < k, pltpu.roll(prev, k, 0), r[0:tail])
        conv = conv + cw[B_CONV - 1 - k:B_CONV - k, :] * jnp.concatenate([head, r[tail:]], axis=0)
    xx_ref[...] = x[L - tail:L]
    qk = conv * jax.nn.sigmoid(conv)

    gcol = gate_ref[...] + gb_ref[...]
    grow = gcol.T
    r_i = lax.broadcasted_iota(jnp.int32, (L, L), 0)
    c_i = lax.broadcasted_iota(jnp.int32, (L, L), 1)
    causal = c_i <= r_i
    tril = causal.astype(F32)
    triu = (r_i <= c_i).astype(F32)
    bcol_all = _dot_f32(tril, _log_sigmoid(gcol))
    brow_all = _dot_f32(_log_sigmoid(grow), triu)

    for hh in range(B_HEADS):
        q = qk[:, hh * d:(hh + 1) * d]
        k = qk[:, B_WIDTH + hh * d:B_WIDTH + (hh + 1) * d] * (d ** -0.5)
        v = v_ref[:, hh * d:(hh + 1) * d]
        qb, kb, vb = q.astype(BF16), k.astype(BF16), v.astype(BF16)
        i_col = gcol[:, hh:hh + 1]
        i_row = grow[hh:hh + 1, :]
        b_col = bcol_all[:, B_HEADS + hh:B_HEADS + hh + 1]
        b_row = brow_all[B_HEADS + hh:B_HEADS + hh + 1, :]
        b_last = b_col[L - 1:L, :]
        m_old = m_ref[hh:hh + 1, 0:1]

        logd = jnp.where(causal, b_col - b_row + i_row, NEG_INF)
        inter = b_col + m_old
        m_t = jnp.maximum(jnp.max(logd, axis=-1, keepdims=True), inter)
        dm = jnp.where(causal, jnp.exp(logd - m_t), 0.0)
        w_inter = jnp.exp(inter - m_t)
        s = _dot_nt(qb, kb) * dm
        ct = c_ref[hh]
        nvec = n_ref[hh:hh + 1, :]
        num = _dot(s.astype(BF16), vb) + w_inter * _dot(qb, ct.astype(BF16))
        den = jnp.sum(s, axis=-1, keepdims=True) + w_inter * jnp.sum(q * nvec, axis=-1, keepdims=True)
        hout = num / jnp.maximum(jnp.abs(den), jnp.exp(-m_t))

        m_new = m_t[L - 1:L, :]
        w_s = jnp.exp(b_last - b_col + i_col - m_new)
        w_prev = jnp.exp(b_last + m_old - m_new)
        c_ref[hh] = w_prev * ct + _dot_tn(kb, (w_s * v).astype(BF16))
        n_ref[hh:hh + 1, :] = w_prev * nvec + jnp.sum(w_s * k, axis=0, keepdims=True)
        m_ref[hh:hh + 1, :] = jnp.broadcast_to(m_new, (1, LANES))

        hn = _rms(hout, mn_ref[:, hh * d:(hh + 1) * d])
        og = jax.nn.sigmoid(o_ref[:, hh * d:(hh + 1) * d])
        y_ref[:, hh * d:(hh + 1) * d] = (og * hn).astype(BF16)


def _mlstm(p, gates, gate_b, conv_w, m_norm, batch):
    T = p.shape[0]
    L = B_CHUNK
    nc = T // batch // L
    qk_blk = (2 * A_WIDTH) // (2 * B_WIDTH)
    v_blk = (2 * A_WIDTH + 2 * B_WIDTH) // B_WIDTH
    row = lambda b, c: b * nc + c
    return pl.pallas_call(
        _mlstm_kernel,
        grid=(batch, nc),
        in_specs=[
            pl.BlockSpec((L, 2 * B_WIDTH), lambda b, c: (row(b, c), qk_blk)),
            pl.BlockSpec((L, B_WIDTH), lambda b, c: (row(b, c), v_blk)),
            pl.BlockSpec((L, B_WIDTH), lambda b, c: (row(b, c), v_blk + 1)),
            pl.BlockSpec((L, LANES), lambda b, c: (row(b, c), 0)),
            pl.BlockSpec((1, LANES), lambda b, c: (0, 0)),
            pl.BlockSpec((B_CONV, 2 * B_WIDTH), lambda b, c: (0, 0)),
            pl.BlockSpec((1, B_WIDTH), lambda b, c: (0, 0)),
        ],
        out_specs=pl.BlockSpec((L, B_WIDTH), lambda b, c: (row(b, c), 0)),
        out_shape=jax.ShapeDtypeStruct((T, B_WIDTH), BF16),
        scratch_shapes=[
            pltpu.VMEM((SUBLANES, 2 * B_WIDTH), F32),
            pltpu.VMEM((B_HEADS, B_HEAD_DIM, B_HEAD_DIM), F32),
            pltpu.VMEM((SUBLANES, B_HEAD_DIM), F32),
            pltpu.VMEM((SUBLANES, LANES), F32),
        ],
        compiler_params=_params("arbitrary", "arbitrary"),
        name="mlstm",
    )(p, p, p, gates, gate_b, conv_w, m_norm)


def _outproj_kernel(*refs, n_lhs):
    h_ref = refs[0]
    o_ref = refs[1 + 2 * n_lhs]
    acc = h_ref[...]
    for i in range(n_lhs):
        acc = acc + _dot(refs[1 + i][...], refs[1 + n_lhs + i][...])
    o_ref[...] = acc


def _outproj(h, lhs, w, *, tm=1024):
    T = h.shape[0]
    n = len(lhs)
    width = lhs[0].shape[1]
    assert all(a.shape[1] == width for a in lhs) and w.shape[0] == n * width
    in_specs = [pl.BlockSpec((tm, D_MODEL), lambda i: (i, 0))]
    in_specs += [pl.BlockSpec((tm, width), lambda i: (i, 0)) for _ in lhs]
    in_specs += [pl.BlockSpec((width, D_MODEL), functools.partial(lambda i, k: (k, 0), k=k)) for k in range(n)]
    return pl.pallas_call(
        functools.partial(_outproj_kernel, n_lhs=n),
        grid=(T // tm,),
        in_specs=in_specs,
        out_specs=pl.BlockSpec((tm, D_MODEL), lambda i: (i, 0)),
        out_shape=jax.ShapeDtypeStruct((T, D_MODEL), F32),
        compiler_params=_params("parallel"),
        name="outproj",
    )(h, *lhs, *([w] * n))


def _rope(x, ca, sn):
    half = ROPE_DIMS // 2
    lane = lax.broadcasted_iota(jnp.int32, x.shape, 1)
    partner = jnp.where(lane < half, lane + half, jnp.where(lane < ROPE_DIMS, lane - half, lane))
    return x * ca + jnp.take_along_axis(x, partner, axis=1) * sn


def _nsa_proj_kernel(h_ref, g_ref, w_ref, wgate_ref, ca_ref, sn_ref, o_ref, og_ref, n_ref):
    j = pl.program_id(1)

    @pl.when(j == 0)
    def _():
        n = _rms(h_ref[...], g_ref[...]).astype(BF16)
        n_ref[...] = n
        og_ref[...] = jax.nn.sigmoid(_dot(n, wgate_ref[...]))

    res = _dot(n_ref[...], w_ref[...])
    heads = o_ref.shape[1]
    is_q = j < C_HEADS // heads

    def head(k):
        return res[:, k * LANES:(k + 1) * LANES]

    @pl.when(is_q)
    def _():
        ca, sn = ca_ref[0], sn_ref[0]
        for k in range(heads):
            o_ref[0, k] = _rope(head(k), ca, sn).astype(BF16)

    @pl.when(jnp.logical_not(is_q))
    def _():
        ca, sn = ca_ref[0], sn_ref[0]
        for k in range(C_KV):
            o_ref[0, k] = _rope(head(k), ca, sn).astype(BF16)
            o_ref[0, C_KV + k] = head(C_KV + k).astype(BF16)


def _nsa_proj(h, g, w, wgate, ca, sn, batch, *, tm=1024):
    T = h.shape[0]
    S = T // batch
    nts = S // tm
    heads = 2 * C_KV
    tn = heads * C_HEAD_DIM
    n_q = C_HEADS // heads
    table = pl.BlockSpec((1, tm, LANES), lambda i, j: (jnp.where(j < n_q, 0, 1), i % nts, 0))
    return pl.pallas_call(
        _nsa_proj_kernel,
        grid=(T // tm, C_MAIN // tn),
        in_specs=[
            pl.BlockSpec((tm, D_MODEL), lambda i, j: (i, 0)),
            pl.BlockSpec((1, D_MODEL), lambda i, j: (0, 0)),
            pl.BlockSpec((D_MODEL, tn), lambda i, j: (0, j)),
            pl.BlockSpec((D_MODEL, LANES), lambda i, j: (0, 0)),
            table, table,
        ],
        out_specs=[
            pl.BlockSpec((1, heads, tm, C_HEAD_DIM), lambda i, j: (i // nts, j, i % nts, 0)),
            pl.BlockSpec((tm, LANES), lambda i, j: (i, 0)),
        ],
        out_shape=[
            jax.ShapeDtypeStruct((batch, HM_HEADS, S, C_HEAD_DIM), BF16),
            jax.ShapeDtypeStruct((T, LANES), F32),
        ],
        scratch_shapes=[pltpu.VMEM((tm, D_MODEL), BF16)],
        compiler_params=_params("parallel", "arbitrary"),
        name="nsa_proj",
    )(h, g, w, wgate, ca, sn)


def _compress_kernel(x_ref, pos_ref, w1_ref, w2_ref, o_ref, xs_ref):
    xs_ref[...] = x_ref[0].astype(F32)
    G, S, dh = xs_ref.shape
    n_rows = S // CMP_STRIDE
    ya = jnp.zeros((G * n_rows, dh), F32)
    yb = jnp.zeros((G * n_rows, dh), F32)
    for l in range(CMP_STRIDE):
        xl = xs_ref[:, pl.ds(l, n_rows, stride=CMP_STRIDE), :].reshape(G * n_rows, dh)
        ya = ya + _dot((xl + pos_ref[0, l:l + 1, :]).astype(BF16), w1_ref[0, l])
        lb = CMP_STRIDE + l
        yb = yb + _dot((xl + pos_ref[0, lb:lb + 1, :]).astype(BF16), w1_ref[0, lb])
    pre = ya + pltpu.roll(yb, G * n_rows - 1, 0)
    out = _dot(jax.nn.gelu(pre).astype(BF16), w2_ref[0]).reshape(G, n_rows, dh)
    rows = lax.broadcasted_iota(jnp.int32, out.shape, 1)
    o_ref[0, 0] = jnp.where(rows < n_rows - 1, out, 0.0).astype(BF16)


def _compress(qkv, pos, w1, w2):
    B_, _, S, dh = qkv.shape
    rows = S // CMP_STRIDE
    return pl.pallas_call(
        _compress_kernel,
        grid=(B_, 2),
        in_specs=[
            pl.BlockSpec((1, C_KV, S, dh), lambda b, kv: (b, HM_KCMP // C_KV + kv, 0, 0)),
            pl.BlockSpec((1, CMP_LEN, dh), lambda b, kv: (kv, 0, 0)),
            pl.BlockSpec((1, CMP_LEN, dh, dh), lambda b, kv: (kv, 0, 0, 0)),
            pl.BlockSpec((1, dh, dh), lambda b, kv: (kv, 0, 0)),
        ],
        out_specs=pl.BlockSpec((1, 1, C_KV, rows, dh), lambda b, kv: (b, kv, 0, 0, 0)),
        out_shape=jax.ShapeDtypeStruct((B_, 2, C_KV, rows, dh), BF16),
        scratch_shapes=[pltpu.VMEM((C_KV, S, dh), F32)],
        compiler_params=_params("parallel", "parallel"),
        name="nsa_compress",
    )(qkv, pos, w1, w2)


def _softmax_parts_t(s):
    m = jnp.max(s, axis=0, keepdims=True)
    p = jnp.exp2(s - m)
    return p, jnp.sum(p, axis=0, keepdims=True)


def _topk_membership_t(imp_t, n_top):
    n_blk, tq = imp_t.shape
    groups = n_blk // SUBLANES
    va = [imp_t[a * SUBLANES:(a + 1) * SUBLANES] for a in range(groups)]
    cnt = [jnp.zeros((SUBLANES, tq), F32) for _ in range(groups)]
    sub = lax.broadcasted_iota(jnp.int32, (SUBLANES, tq), 0)
    for i in range(n_blk):
        vi = jnp.broadcast_to(imp_t[i:i + 1, :], (SUBLANES, tq))
        for a in range(groups):
            if i < a * SUBLANES:
                ahead = jnp.where(vi >= va[a], 1.0, 0.0)
            elif i >= (a + 1) * SUBLANES:
                ahead = jnp.where(vi > va[a], 1.0, 0.0)
            else:
                ahead = jnp.where(sub > i - a * SUBLANES, jnp.where(vi >= va[a], 1.0, 0.0),
                                  jnp.where(vi > va[a], 1.0, 0.0))
            cnt[a] = cnt[a] + ahead
    return jnp.concatenate([jnp.where(c < n_top, 1.0, 0.0) for c in cnt], axis=0)


def _nsa_attn_kernel(q_ref, kc_ref, vc_ref, ks_ref, vs_ref, kw_ref, vw_ref, gate_ref, ovt_ref, y_ref,
                     qx_ref, m_ref, l_ref, acc_ref, oc_ref, ow_ref, *, tq, tk, n_slc, n_top):
    G, R, dh = C_KV, C_REP, C_HEAD_DIM
    t0 = pl.program_id(1) * tq
    pos = t0 + lax.broadcasted_iota(jnp.int32, (1, tq), 1)

    def q_of(g):
        return q_ref[0, g * R:(g + 1) * R].reshape(R * tq, dh)

    def per_head(x):
        return jnp.concatenate([x] * R, axis=1)

    n_cmp_rows = kc_ref.shape[3]
    cmp_end = lax.broadcasted_iota(jnp.int32, (n_cmp_rows, tq), 0) * CMP_STRIDE + (CMP_LEN - 1)
    ok_c = jnp.logical_and(cmp_end <= pos, cmp_end < n_cmp_rows * CMP_STRIDE)
    bias_c = per_head(jnp.where(ok_c, 0.0, NEG_INF))
    has_c = per_head(pos >= CMP_LEN - 1)
    slab = WINDOW + tq
    w0 = pl.multiple_of(jnp.maximum(t0 - WINDOW, 0), tq)
    kpos_w = w0 + lax.broadcasted_iota(jnp.int32, (slab, tq), 0)
    ok_w = jnp.logical_and(kpos_w <= pos, pos - kpos_w < WINDOW)
    bias_w = per_head(jnp.where(ok_w, 0.0, NEG_INF))
    blk_t = lax.broadcasted_iota(jnp.int32, (n_slc, tq), 0)
    cur_t = jnp.right_shift(pos, SEL_SHIFT)
    forced = jnp.logical_or(blk_t == 0, jnp.logical_or(blk_t == cur_t, blk_t == cur_t - 1))
    future = blk_t > cur_t

    s_cmp = [_dot_nt(kc_ref[0, 0, g], q_of(g)) + bias_c for g in range(G)]
    s_win = [_dot_nt(kw_ref[0, g, pl.ds(w0, slab), :], q_of(g)) + bias_w for g in range(G)]
    for g in range(G):
        p_c, l_c = _softmax_parts_t(s_cmp[g])
        p_c = p_c * jnp.where(has_c, 1.0 / l_c, 0.0)
        oc_ref[g] = _dot_tn(vc_ref[0, 0, g], p_c.astype(BF16))

        p_w, l_w = _softmax_parts_t(s_win[g])
        ow_ref[g] = _dot_tn(vw_ref[0, g, pl.ds(w0, slab), :], p_w.astype(BF16)) * (1.0 / l_w)

        p_sum = p_c[:, 0:tq]
        for r in range(1, R):
            p_sum = p_sum + p_c[:, r * tq:(r + 1) * tq]
        imp_t = jnp.where(forced, BIG, jnp.where(future, -BIG, _dot_f32(ovt_ref[...], p_sum)))
        sel_t = _topk_membership_t(imp_t, n_top)
        sel = jnp.concatenate([sel_t, jnp.zeros((LANES - n_slc, tq), F32)], axis=0).T
        penalty = jnp.where(sel > 0.5, 0.0, NEG_INF).astype(BF16)
        qx_ref[g] = jnp.concatenate([q_of(g), jnp.concatenate([penalty] * R, axis=0)], axis=1)

    m_ref[...] = jnp.full(m_ref.shape, NEG_INF, F32)
    l_ref[...] = jnp.zeros(l_ref.shape, F32)
    acc_ref[...] = jnp.zeros(acc_ref.shape, F32)
    blocks_per_tile = tk // SEL_LEN

    def sweep_tile(kt, diagonal):
        k0 = pl.multiple_of(kt * tk, tk)
        key_blk = kt * blocks_per_tile + jnp.right_shift(lax.broadcasted_iota(jnp.int32, (tk, LANES), 0), SEL_SHIFT)
        onehot = jnp.where(lax.broadcasted_iota(jnp.int32, (tk, LANES), 1) == key_blk, 1.0, 0.0).astype(BF16)
        s_all = []
        for g in range(G):
            k_ext = jnp.concatenate([ks_ref[0, g, pl.ds(k0, tk), :], onehot], axis=1)
            s = _dot_nt(k_ext, qx_ref[g])
            if diagonal:
                causal = k0 + lax.broadcasted_iota(jnp.int32, (tk, tq), 0) <= pos
                s = s + per_head(jnp.where(causal, 0.0, NEG_INF))
            s_all.append(s)
        for g, s in enumerate(s_all):
            m_old = m_ref[g]
            m_new = jnp.maximum(m_old, jnp.max(s, axis=0, keepdims=True))
            p = jnp.exp2(s - m_new)
            alpha = jnp.exp2(m_old - m_new)
            l_ref[g] = alpha * l_ref[g] + jnp.sum(p, axis=0, keepdims=True)
            acc_ref[g] = alpha * acc_ref[g] + _dot_tn(vs_ref[0, g, pl.ds(k0, tk), :], p.astype(BF16))
            m_ref[g] = m_new

    def sweep(kt, carry):
        sweep_tile(kt, diagonal=False)
        return carry

    last_tile = t0 // tk
    lax.fori_loop(0, last_tile, sweep, 0)
    sweep_tile(last_tile, diagonal=True)

    gates_t = gate_ref[...].T
    for g in range(G):
        os_t = acc_ref[g] * (1.0 / l_ref[g])
        oc_t, ow_t = oc_ref[g], ow_ref[g]
        for r in range(R):
            cols = slice(r * tq, (r + 1) * tq)
            hd = g * R + r
            merged_t = (gates_t[hd:hd + 1, :] * oc_t[:, cols]
                        + gates_t[C_HEADS + hd:C_HEADS + hd + 1, :] * os_t[:, cols]
                        + gates_t[2 * C_HEADS + hd:2 * C_HEADS + hd + 1, :] * ow_t[:, cols])
            y_ref[:, (g * R + r) * dh:(g * R + r + 1) * dh] = merged_t.T.astype(BF16)


def _nsa_attn(qkv, cmp_kv, gates, overlap_t, *, tq=256, tk=512):
    B_, _, S, dh = qkv.shape
    nq = S // tq
    n_slc = S // SEL_LEN
    n_top = min(SEL_TOPK, n_slc)
    assert n_top >= 3 and n_slc % SUBLANES == 0
    tk = min(tk, S)
    assert tq % LANES == 0 and tk % tq == 0 and n_slc <= LANES
    n_cmp_rows = cmp_kv.shape[3]
    G, R = C_KV, C_REP
    seq = lambda base: pl.BlockSpec((1, G, S, dh), lambda b, i: (b, base // G, 0, 0), pipeline_mode=pl.Buffered(1))
    return pl.pallas_call(
        functools.partial(_nsa_attn_kernel, tq=tq, tk=tk, n_slc=n_slc, n_top=n_top),
        grid=(B_, nq),
        in_specs=[
            pl.BlockSpec((1, C_HEADS, tq, dh), lambda b, i: (b, 0, i, 0)),
            pl.BlockSpec((1, 1, G, n_cmp_rows, dh), lambda b, i: (b, 0, 0, 0, 0)),
            pl.BlockSpec((1, 1, G, n_cmp_rows, dh), lambda b, i: (b, 1, 0, 0, 0)),
            seq(HM_KSEL), seq(HM_VSEL), seq(HM_KWIN), seq(HM_VWIN),
            pl.BlockSpec((tq, LANES), lambda b, i: (b * nq + i, 0)),
            pl.BlockSpec((n_slc, n_cmp_rows), lambda b, i: (0, 0)),
        ],
        out_specs=pl.BlockSpec((tq, C_HEADS * dh), lambda b, i: (b * nq + i, 0)),
        out_shape=jax.ShapeDtypeStruct((B_ * S, C_HEADS * dh), BF16),
        scratch_shapes=[
            pltpu.VMEM((G, R * tq, dh + LANES), BF16),
            pltpu.VMEM((G, 1, R * tq), F32),
            pltpu.VMEM((G, 1, R * tq), F32),
            pltpu.VMEM((G, dh, R * tq), F32),
            pltpu.VMEM((G, dh, R * tq), F32),
            pltpu.VMEM((G, dh, R * tq), F32),
        ],
        compiler_params=_params("parallel", "arbitrary"),
        name="nsa_attn",
    )(qkv, cmp_kv, cmp_kv, qkv, qkv, qkv, qkv, gates, overlap_t)


def _rope_tables(S):
    half = ROPE_DIMS // 2
    inv = 1.0 / (ROPE_THETA ** (jnp.arange(half, dtype=F32) / half))
    ang = jnp.arange(S, dtype=F32)[:, None] * inv[None, :]
    cos, sin = jnp.cos(ang), jnp.sin(ang)
    rest = LANES - ROPE_DIMS
    ca = jnp.concatenate([cos, cos, jnp.ones((S, rest), F32)], axis=-1)
    sn = jnp.concatenate([-sin, sin, jnp.zeros((S, rest), F32)], axis=-1)
    q_scale = C_HEAD_DIM ** -0.5 * LOG2E
    return jnp.stack([ca * q_scale, ca]), jnp.stack([sn * q_scale, sn])


def _overlap_matrix(n_cmp_rows, n_slc):
    ci = jnp.arange(n_cmp_rows)[:, None] * CMP_STRIDE
    sj = jnp.arange(n_slc)[None, :] * SEL_LEN
    ov = jnp.clip(jnp.minimum(ci + CMP_LEN, sj + SEL_LEN) - jnp.maximum(ci, sj), 0).astype(F32) / CMP_LEN
    return ov


def _gmlp_mlstm_layer(h, mix_g, w_in, gate_b, g_norm, g_ws, g_bs, conv_w, m_norm, w_out, batch):
    w_main = w_in.astype(BF16)
    w_gate = jnp.pad(w_in[:, AB_MAIN:], ((0, 0), (0, LANES - 2 * B_HEADS))).astype(BF16)
    p, gates = _ab_proj(h, mix_g.reshape(1, D_MODEL), w_main, w_gate)
    y_a = _gmlp(p, g_norm.reshape(1, A_WIDTH), g_ws, g_bs.T)
    gb = jnp.pad(gate_b.reshape(1, 2 * B_HEADS), ((0, 0), (0, LANES - 2 * B_HEADS)))
    y_b = _mlstm(p, gates, gb, conv_w, m_norm.reshape(1, B_WIDTH), batch)
    return _outproj(h, [y_a, y_b], w_out.astype(BF16))


def _nsa_layer(h, mix_g, w_in, cmp_pos, cmp_w1, cmp_w2, w_out, batch):
    T = h.shape[0]
    S = T // batch
    w_main = w_in.astype(BF16)
    wg = jnp.pad(w_in[:, C_MAIN:], ((0, 0), (0, LANES - 3 * C_HEADS))).astype(BF16)
    ca, sn = _rope_tables(S)
    qkv, gates = _nsa_proj(h, mix_g.reshape(1, D_MODEL), w_main, wg, ca, sn, batch)
    cmp_kv = _compress(qkv, cmp_pos, cmp_w1.reshape(2, CMP_LEN, C_HEAD_DIM, C_HEAD_DIM).astype(BF16),
                       cmp_w2.astype(BF16))
    overlap_t = _overlap_matrix(S // CMP_STRIDE, S // SEL_LEN).T
    y = _nsa_attn(qkv, cmp_kv, gates, overlap_t)
    return _outproj(h, [y], w_out.astype(BF16))


def kernel(x, ffn_norm, ffn_w_gate, ffn_w_up, ffn_w_down, mix_norm, ab_w_in, mlstm_gate_bias, gmlp_norm, gmlp_w_s,
           gmlp_b_s, mlstm_conv, mlstm_norm, ab_w_out, nsa_w_in, nsa_cmp_pos, nsa_cmp_w1, nsa_cmp_w2, nsa_w_out,
           final_norm):
    batch, S, _ = x.shape
    depth = ffn_norm.shape[0]
    h = x.reshape(batch * S, D_MODEL)
    fg = final_norm.reshape(1, D_MODEL)

    wg, wu, wd = ffn_w_gate.astype(BF16), ffn_w_up.astype(BF16), ffn_w_down.astype(BF16)

    def ffn(h, layer, half, final=False):
        return _ffn(h, ffn_norm[layer, half].reshape(1, D_MODEL), wg, wu, wd, fg, layer, half, final=final)

    for layer in range(depth):
        j = layer // 2
        h = ffn(h, layer, 0)
        if layer % 2 == 0:
            h = _gmlp_mlstm_layer(h, mix_norm[layer], ab_w_in[j], mlstm_gate_bias[j], gmlp_norm[j], gmlp_w_s[j],
                                  gmlp_b_s[j], mlstm_conv[j], mlstm_norm[j], ab_w_out[j], batch)
        else:
            h = _nsa_layer(h, mix_norm[layer], nsa_w_in[j], nsa_cmp_pos[j], nsa_cmp_w1[j], nsa_cmp_w2[j],
                           nsa_w_out[j], batch)
        h = ffn(h, layer, 1, final=(layer == depth - 1))
    return h.reshape(batch, S, D_MODEL)
```
